```python
import math
import jax, jax.numpy as jnp
from jax import lax
import numpy as np

D_MODEL = 1024
BATCH = 2
SEQ = 8192
DEPTH = 2
DEC_BATCH = 32
DEC_SEQ = 1
PAST_LEN = 8192
PAGE_SIZE = 128

H_A = 4
DK_A = 128
DV_A = 128
CONV_W = 4
CHUNK = 64
C_QKV = 2 * H_A * DK_A + H_A * DV_A
H_B = 8
G_B = 2
DH_B = 64
KV_B = 2 * G_B * DH_B
CMP_LEN = 32
CMP_STRIDE = 16
CMP_HID = 256
SLC_BLK = 64
N_SEL = 16
WINDOW = 512
Q_BLK = 128
D_FF = 2816
N_EXP = 8
TOP_K = 2
D_FF_EXP = 2816
IN_SPLITS = [C_QKV, H_A * DV_A, H_A, H_A, H_B * DH_B, KV_B, KV_B, KV_B, 3 * H_B, 2 * D_MODEL]
N_IN = C_QKV + H_A * DV_A + 2 * H_A + H_B * DH_B + 3 * KV_B + 3 * H_B + 2 * D_MODEL
ALPHA = (2 * DEPTH) ** 0.25
BETA_INIT = (8 * DEPTH) ** -0.25
EPS = 1e-5
F32 = jnp.float32

kernel_name = "hybrid_gdn_nsa_deepnorm_adaln_step"


def _split(x, sizes):
    offs = [int(o) for o in np.cumsum(sizes)[:-1]]
    return jnp.split(x, offs, axis=-1)


def _ln(x):
    xf = x.astype(F32)
    mu = jnp.mean(xf, -1, keepdims=True)
    var = jnp.mean(jnp.square(xf - mu), -1, keepdims=True)
    return (xf - mu) * lax.rsqrt(var + EPS)


def _ln_affine(x, g, b):
    return _ln(x) * g + b


def _l2norm(x):
    xf = x.astype(F32)
    return xf * lax.rsqrt(jnp.sum(xf * xf, -1, keepdims=True) + 1e-6)


def _gated_rmsnorm(o, z, w):
    of = o.astype(F32)
    o = of * lax.rsqrt(jnp.mean(of * of, -1, keepdims=True) + EPS) * w
    return o * jax.nn.silu(z.astype(F32))


def _masked_softmax(s, mask):
    s = jnp.where(mask, s.astype(F32), -1e30)
    m = jnp.max(s, -1, keepdims=True)
    e = jnp.where(mask, jnp.exp(s - m), 0.0)
    return e / jnp.maximum(jnp.sum(e, -1, keepdims=True), 1e-30)


def _short_conv(x, buf, w):
    T = x.shape[1]
    xp = jnp.concatenate([buf.astype(x.dtype), x], axis=1)
    y = sum(w[j] * xp[:, j:j + T] for j in range(CONV_W))
    return jax.nn.silu(y), xp[:, T:]


def _gated_delta_rule(q, k, v, beta, g, s0):
    B, T, H, DK = q.shape
    DV = v.shape[-1]
    n = -(-T // CHUNK)
    pad = n * CHUNK - T

    def prep(a):
        a = a.astype(F32)
        a = jnp.pad(a, [(0, 0), (0, pad)] + [(0, 0)] * (a.ndim - 2))
        a = a.reshape((B, n, CHUNK) + a.shape[2:])
        return jnp.moveaxis(a, 3, 1)

    q, k, v, beta, g = (prep(a) for a in (q, k, v, beta, g))
    q = q * (DK ** -0.5)
    gc = jnp.cumsum(g, axis=-1)
    idx = jnp.arange(CHUNK)
    incl = idx[:, None] >= idx[None, :]
    strict = idx[:, None] > idx[None, :]
    decay = jnp.exp(jnp.where(incl, gc[..., :, None] - gc[..., None, :], -jnp.inf))
    kb = k * beta[..., None]
    a = jnp.where(strict, jnp.einsum('bhnid,bhnjd->bhnij', kb, k) * decay, 0.0)
    rhs = jnp.concatenate([v * beta[..., None], kb * jnp.exp(gc)[..., None]], axis=-1)
    uw = lax.linalg.triangular_solve(a, rhs, left_side=True, lower=True, unit_diagonal=True)
    u, w = uw[..., :DV], uw[..., DV:]
    qk = jnp.einsum('bhnid,bhnjd->bhnij', q, k) * decay
    q_dec = q * jnp.exp(gc)[..., None]
    g_last = gc[..., -1]
    k_dec = k * jnp.exp(g_last[..., None] - gc)[..., None]

    def step(s, xs):
        u_n, w_n, qd_n, qk_n, kd_n, gl_n = xs
        v_new = u_n - jnp.einsum('bhck,bhkv->bhcv', w_n, s)
        o_n = jnp.einsum('bhck,bhkv->bhcv', qd_n, s) + jnp.einsum('bhij,bhjv->bhiv', qk_n, v_new)
        s = s * jnp.exp(gl_n)[..., None, None] + jnp.einsum('bhck,bhcv->bhkv', kd_n, v_new)
        return s, o_n

    xs = tuple(jnp.moveaxis(t_, 2, 0) for t_ in (u, w, q_dec, qk, k_dec, g_last))
    s_fin, o = lax.scan(step, s0.astype(F32), xs)
    o = jnp.transpose(o, (1, 0, 3, 2, 4)).reshape(B, n * CHUNK, H, DV)[:, :T]
    return o, s_fin


def _compress(rows, pos_emb, w1, w2):
    B, L = rows.shape[:2]
    r = CMP_LEN // CMP_STRIDE
    n_seg = L // CMP_STRIDE
    n_cmp = n_seg - r + 1
    seg = rows[:, :n_seg * CMP_STRIDE].reshape(B, n_seg, CMP_STRIDE, 2, G_B, DH_B)
    w1 = w1.reshape(2, r, CMP_STRIDE, DH_B, CMP_HID)
    pe = pos_emb.reshape(2, r, CMP_STRIDE, DH_B)
    hid = 0.0
    for m in range(r):
        part = seg[:, m:m + n_cmp] + jnp.transpose(pe[:, m], (1, 0, 2))[:, :, None, :]
        hid = hid + jnp.einsum('bnskgd,ksdh->bnkgh', part, w1[:, m])
    return jnp.einsum('bnkgh,khd->bnkgd', jax.nn.silu(hid), w2)


def _nsa(q, gates, rows_c, rows_s, win_pad, pos0, cmp_pos, cmp_w1, cmp_w2):
    B, T = q.shape[:2]
    L = rows_s.shape[1]
    R = H_B // G_B
    kv_cmp = _compress(rows_c, cmp_pos, cmp_w1, cmp_w2)
    kc, vc = kv_cmp[:, :, 0], kv_cmp[:, :, 1]
    n_cmp = kc.shape[1]
    cmp_end = jnp.arange(n_cmp) * CMP_STRIDE + CMP_LEN - 1
    n_slc = -(-L // SLC_BLK)
    kv_blk = jnp.pad(rows_s, [(0, 0), (0, n_slc * SLC_BLK - L), (0, 0), (0, 0), (0, 0)])
    kv_blk = kv_blk.reshape(B, n_slc, SLC_BLK, 2, G_B, DH_B).transpose(0, 4, 1, 2, 3, 5)
    ci = jnp.arange(n_cmp)[:, None] * CMP_STRIDE
    sj = jnp.arange(n_slc)[None, :] * SLC_BLK
    overlap = ((ci < sj + SLC_BLK) & (ci + CMP_LEN > sj)).astype(F32)
    n_sel = min(N_SEL, n_slc)
    qb = Q_BLK if T % Q_BLK == 0 else T
    n_qb = T // qb
    q_blocks = q.reshape(B, n_qb, qb, G_B, R, DH_B).transpose(1, 0, 2, 3, 4, 5)
    g_blocks = gates.reshape(B, n_qb, qb, H_B, 3).transpose(1, 0, 2, 3, 4)
    bi = jnp.arange(B)[:, None, None, None]
    gi = jnp.arange(G_B)[None, :, None, None]
    blk = jnp.arange(n_slc)
    scale = DH_B ** -0.5

    def block(args):
        iq, qbk, gbk = args
        t = pos0 + iq * qb + jnp.arange(qb)
        qf = qbk.astype(F32) * scale
        s = jnp.einsum('btgrd,bngd->bgrtn', qf, kc)
        p_c = _masked_softmax(s, cmp_end[None, :] <= t[:, None])
        o_c = jnp.einsum('bgrtn,bngd->btgrd', p_c, vc)
        imp = jnp.einsum('bgrtn,nj->bgtj', p_c, overlap)
        forced = (blk[None, :] == (t // SLC_BLK)[:, None]) | (blk[None, :] == 0)
        valid = blk[None, :] * SLC_BLK <= t[:, None]
        imp = jnp.where(forced, 1e9, jnp.where(valid, imp, -1.0))
        _, idx = lax.top_k(imp, n_sel)
        kv_sel = kv_blk[bi, gi, idx]
        s = jnp.einsum('btgrd,bgtnsd->bgrtns', qf, kv_sel[..., 0, :])
        kpos = idx[..., None] * SLC_BLK + jnp.arange(SLC_BLK)
        m_s = (kpos <= t[None, None, :, None, None])[:, :, None]
        p_s = _masked_softmax(s.reshape(B, G_B, R, qb, n_sel * SLC_BLK),
                              m_s.reshape(B, G_B, 1, qb, n_sel * SLC_BLK)).reshape(s.shape)
        o_s = jnp.einsum('bgrtns,bgtnsd->btgrd', p_s, kv_sel[..., 1, :])
        kv_w = lax.dynamic_slice_in_dim(win_pad, iq * qb, WINDOW + qb, axis=1)
        wpos = pos0 - WINDOW + iq * qb + jnp.arange(WINDOW + qb)
        dlt = t[:, None] - wpos[None, :]
        m_w = (dlt >= 0) & (dlt < WINDOW) & (wpos[None, :] >= 0)
        s = jnp.einsum('btgrd,bsgd->bgrts', qf, kv_w[:, :, 0])
        p_w = _masked_softmax(s, m_w)
        o_w = jnp.einsum('bgrts,bsgd->btgrd', p_w, kv_w[:, :, 1])
        gf = jax.nn.sigmoid(gbk.astype(F32)).reshape(B, qb, G_B, R, 3)
        return gf[..., 0:1] * o_c + gf[..., 1:2] * o_s + gf[..., 2:3] * o_w

    o = lax.map(block, (jnp.arange(n_qb), q_blocks, g_blocks))
    return o.transpose(1, 0, 2, 3, 4, 5).reshape(B, T, H_B * DH_B)


def _mixer(h, conv_buf, s0, past_c, past_s, win_buf, w_in, conv_w, a_log, dt_bias, norm_w_a,
           cmp_pos, cmp_w1, cmp_w2, w_up_a, w_up_b, w_out):
    B, T, _ = h.shape
    pos0 = past_s.shape[1]
    qkv_a, z_a, b_a, a_a, q_b, kv_c, kv_s, kv_w, g_b, g_m = _split(h @ w_in, IN_SPLITS)
    qkv_a, conv_new = _short_conv(qkv_a, conv_buf, conv_w)
    q_a, k_a, v_a = _split(qkv_a, [H_A * DK_A, H_A * DK_A, H_A * DV_A])
    q_a = _l2norm(q_a.reshape(B, T, H_A, DK_A))
    k_a = _l2norm(k_a.reshape(B, T, H_A, DK_A))
    v_a = v_a.reshape(B, T, H_A, DV_A)
    beta = jax.nn.sigmoid(b_a.astype(F32))
    g_dec = -jnp.exp(a_log) * jax.nn.softplus(a_a.astype(F32) + dt_bias)
    o_a, s_new = _gated_delta_rule(q_a, k_a, v_a, beta, g_dec, s0)
    o_a = _gated_rmsnorm(o_a, z_a.reshape(B, T, H_A, DV_A), norm_w_a).reshape(B, T, H_A * DV_A)
    kv_c = kv_c.reshape(B, T, 2, G_B, DH_B)
    kv_s = kv_s.reshape(B, T, 2, G_B, DH_B)
    kv_w = kv_w.reshape(B, T, 2, G_B, DH_B)
    rows_c = jnp.concatenate([past_c.astype(kv_c.dtype), kv_c], axis=1)
    rows_s = jnp.concatenate([past_s.astype(kv_s.dtype), kv_s], axis=1)
    win_all = jnp.concatenate([win_buf.astype(kv_w.dtype), kv_w], axis=1)
    n_w = win_all.shape[1]
    win_pad = jnp.pad(win_all, [(0, 0), (WINDOW + T - n_w, 0), (0, 0), (0, 0), (0, 0)])
    win_new = win_all[:, n_w - min(WINDOW, n_w):]
    o_b = _nsa(q_b.reshape(B, T, H_B, DH_B), g_b.reshape(B, T, H_B, 3), rows_c, rows_s, win_pad,
               pos0, cmp_pos, cmp_w1, cmp_w2)
    ga, gb = jnp.split(jax.nn.sigmoid(g_m.astype(F32)), 2, axis=-1)
    y = (ga * (o_a @ w_up_a) + gb * (o_b @ w_up_b)) @ w_out
    return y, s_new, conv_new, kv_c, kv_s, win_new


def _swiglu(h, w_gu, w_down):
    gt, up = jnp.split(h @ w_gu, 2, axis=-1)
    return (jax.nn.silu(gt) * up) @ w_down


def _moe(h, w_router, w_gu, w_down):
    B, T, D = h.shape
    x = h.reshape(B * T, D)
    probs = jax.nn.softmax((x @ w_router).astype(F32), axis=-1)
    top_p, top_i = lax.top_k(probs, TOP_K)
    top_p = top_p / jnp.sum(top_p, -1, keepdims=True)
    y = jnp.zeros((B * T, D), F32)
    for e in range(N_EXP):
        w_e = jnp.sum(jnp.where(top_i == e, top_p, 0.0), axis=-1)
        y = y + w_e[:, None] * _swiglu(x, w_gu[e], w_down[e])
    return y.reshape(B, T, D)


def setup_inputs(seed: int = 0) -> dict:
    key = jax.random.key(seed)
    ks = iter(jax.random.split(key, 40))

    def nrm(shape, s=1.0):
        return s * jax.random.normal(next(ks), shape, F32)

    n_pages = PAST_LEN // PAGE_SIZE
    n_used = DEC_BATCH * n_pages
    n_pool = n_used + max(1, n_used // 4)
    w_buf = min(WINDOW, PAST_LEN)
    n_dense = (DEPTH + 1) // 2
    n_moe = DEPTH // 2
    page_table = jax.random.permutation(next(ks), n_pool)[:n_used].reshape(DEC_BATCH, n_pages).astype(jnp.int32)
    dt = jnp.exp(jax.random.uniform(next(ks), (DEPTH, H_A), F32, math.log(1e-3), math.log(1e-1)))
    a_log = jnp.log(jax.random.uniform(next(ks), (DEPTH, H_A), F32, 1.0, 16.0))
    return {
        'x_prompt': nrm((BATCH, SEQ, D_MODEL)),
        'x_sample': nrm((DEC_BATCH, DEC_SEQ, D_MODEL)),
        'state_delta': nrm((DEPTH, DEC_BATCH, H_A, DK_A, DV_A), 0.1),
        'state_conv': nrm((DEPTH, DEC_BATCH, CONV_W - 1, C_QKV)),
        'cache_cmp_kv': nrm((DEPTH, n_pool, PAGE_SIZE, 2, G_B, DH_B)),
        'cache_slc_kv': nrm((DEPTH, n_pool, PAGE_SIZE, 2, G_B, DH_B)),
        'state_win_kv': nrm((DEPTH, DEC_BATCH, w_buf, 2, G_B, DH_B)),
        'page_table': page_table,
        'c_prompt': nrm((BATCH, D_MODEL)),
        'c_sample': nrm((DEC_BATCH, D_MODEL)),
        'w_ada': nrm((DEPTH, D_MODEL, 6 * D_MODEL), 0.1 * D_MODEL ** -0.5),
        'b_ada': nrm((DEPTH, 6 * D_MODEL), 0.01),
        'w_in': nrm((DEPTH, D_MODEL, N_IN), D_MODEL ** -0.5),
        'conv_w': nrm((DEPTH, CONV_W, C_QKV), CONV_W ** -0.5),
        'a_log': a_log,
        'dt_bias': dt + jnp.log(-jnp.expm1(-dt)),
        'norm_w_a': 1.0 + nrm((DEPTH, DV_A), 0.02),
        'cmp_pos': nrm((DEPTH, 2, CMP_LEN, DH_B), 0.02),
        'cmp_w1': nrm((DEPTH, 2, CMP_LEN * DH_B, CMP_HID), (CMP_LEN * DH_B) ** -0.5),
        'cmp_w2': nrm((DEPTH, 2, CMP_HID, DH_B), CMP_HID ** -0.5),
        'w_up_a': nrm((DEPTH, H_A * DV_A, D_MODEL), BETA_INIT * (H_A * DV_A) ** -0.5),
        'w_up_b': nrm((DEPTH, H_B * DH_B, D_MODEL), BETA_INIT * (H_B * DH_B) ** -0.5),
        'w_out': nrm((DEPTH, D_MODEL, D_MODEL), BETA_INIT * D_MODEL ** -0.5),
        'ln_g': 1.0 + nrm((DEPTH, 2, D_MODEL), 0.02),
        'ln_b': nrm((DEPTH, 2, D_MODEL), 0.02),
        'ffn_w_gu': nrm((n_dense, D_MODEL, 2 * D_FF), D_MODEL ** -0.5),
        'ffn_w_down': nrm((n_dense, D_FF, D_MODEL), BETA_INIT * D_FF ** -0.5),
        'moe_router': nrm((n_moe, D_MODEL, N_EXP), D_MODEL ** -0.5),
        'moe_w_gu': nrm((n_moe, N_EXP, D_MODEL, 2 * D_FF_EXP), D_MODEL ** -0.5),
        'moe_w_down': nrm((n_moe, N_EXP, D_FF_EXP, D_MODEL), BETA_INIT * D_FF_EXP ** -0.5),
    }


def reference(x_prompt, x_sample, state_delta, state_conv, cache_cmp_kv, cache_slc_kv, state_win_kv,
              page_table, c_prompt, c_sample, w_ada, b_ada, w_in, conv_w, a_log, dt_bias, norm_w_a,
              cmp_pos, cmp_w1, cmp_w2, w_up_a, w_up_b, w_out, ln_g, ln_b, ffn_w_gu, ffn_w_down,
              moe_router, moe_w_gu, moe_w_down):
    def layer(l, x, c, conv_buf, s0, past_c, past_s, win_buf):
        mod = jax.nn.silu(c.astype(F32)) @ w_ada[l] + b_ada[l]
        sh_m, sc_m, gt_m, sh_f, sc_f, gt_f = jnp.split(mod[:, None, :], 6, axis=-1)
        h = _ln(x) * (1.0 + sc_m) + sh_m
        y, s_new, conv_new, rows_c, rows_s, win_new = _mixer(
            h, conv_buf, s0, past_c, past_s, win_buf, w_in[l], conv_w[l], a_log[l], dt_bias[l],
            norm_w_a[l], cmp_pos[l], cmp_w1[l], cmp_w2[l], w_up_a[l], w_up_b[l], w_out[l])
        x = _ln_affine(ALPHA * x + gt_m * y, ln_g[l, 0], ln_b[l, 0])
        h = _ln(x) * (1.0 + sc_f) + sh_f
        if l % 2 == 0:
            f = _swiglu(h, ffn_w_gu[l // 2], ffn_w_down[l // 2])
        else:
            f = _moe(h, moe_router[l // 2], moe_w_gu[l // 2], moe_w_down[l // 2])
        x = _ln_affine(ALPHA * x + gt_f * f, ln_g[l, 1], ln_b[l, 1])
        return x, (s_new, conv_new, rows_c, rows_s, win_new)

    bp = x_prompt.shape[0]
    db = x_sample.shape[0]
    xp, xs = x_prompt, x_sample
    st_p, st_s = [], []
    for l in range(DEPTH):
        empty = jnp.zeros((bp, 0, 2, G_B, DH_B), F32)
        xp, sp = layer(l, xp, c_prompt, jnp.zeros((bp, CONV_W - 1, C_QKV), F32),
                       jnp.zeros((bp, H_A, DK_A, DV_A), F32), empty, empty, empty)
        past_c = cache_cmp_kv[l][page_table].reshape(db, -1, 2, G_B, DH_B)
        past_s = cache_slc_kv[l][page_table].reshape(db, -1, 2, G_B, DH_B)
        xs, ss = layer(l, xs, c_sample, state_conv[l], state_delta[l], past_c, past_s, state_win_kv[l])
        st_p.append(sp)
        st_s.append(ss)
    new_delta_p = jnp.stack([s[0] for s in st_p])
    new_conv_p = jnp.stack([s[1] for s in st_p])
    new_cmp_p = jnp.stack([s[2] for s in st_p])
    new_slc_p = jnp.stack([s[3] for s in st_p])
    new_win_p = jnp.stack([s[4] for s in st_p])
    new_delta_s = jnp.stack([s[0] for s in st_s])
    new_conv_s = jnp.stack([s[1] for s in st_s])
    new_cmp_s = jnp.stack([s[2] for s in st_s])
    new_slc_s = jnp.stack([s[3] for s in st_s])
    new_win_s = jnp.stack([s[4] for s in st_s])
    return (xp, xs, new_delta_p, new_conv_p, new_cmp_p, new_slc_p, new_win_p,
            new_delta_s, new_conv_s, new_cmp_s, new_slc_s, new_win_s)
```

```python
import functools
import math

import jax
import jax.numpy as jnp
import numpy as np
from jax import lax
from jax.experimental import pallas as pl
from jax.experimental.pallas import tpu as pltpu

F32 = jnp.float32
BF16 = jnp.bfloat16
HIGHEST = lax.Precision.HIGHEST

D_MODEL = 1024
N_LAYERS = 2
PAGE_ROWS = 128
GDN_HEADS = 4
GDN_DK = 128
GDN_DV = 128
CONV_TAPS = 4
GDN_CHUNK = 64
C_QKV = 2 * GDN_HEADS * GDN_DK + GDN_HEADS * GDN_DV
NSA_HEADS = 8
NSA_GROUPS = 2
NSA_REP = NSA_HEADS // NSA_GROUPS
NSA_DH = 64
KV_ROW = 2 * NSA_GROUPS * NSA_DH
CMP_LEN = 32
CMP_STRIDE = 16
CMP_HID = 256
SLC_BLK = 64
N_SEL = 16
WINDOW = 512
Q_BLK = 128
D_FF = 2816
N_EXP = 8
ALPHA = (2 * N_LAYERS) ** 0.25
EPS = 1e-5
NEG = -1e30

LANES = 128
SUBLANES = 8
VMEM_LIMIT = 56 * 1024 * 1024

_SMALL_W = LANES
IN_OUT_WIDTHS = (C_QKV, GDN_HEADS * GDN_DV, 2 * D_MODEL, NSA_HEADS * NSA_DH, KV_ROW, KV_ROW, KV_ROW, _SMALL_W)


def _cparams(sem):
    return pltpu.CompilerParams(dimension_semantics=sem, vmem_limit_bytes=VMEM_LIMIT)


def _silu(x):
    return x * jax.nn.sigmoid(x)


def _ln_rows(x):
    mu = jnp.mean(x, axis=-1, keepdims=True)
    xc = x - mu
    var = jnp.mean(xc * xc, axis=-1, keepdims=True)
    return xc * lax.rsqrt(var + EPS)


def _dot(a, b, precision=None):
    return jnp.dot(a, b, preferred_element_type=F32, precision=precision)


def _dot_nt(a, b, precision=None):
    return lax.dot_general(a, b, (((1,), (1,)), ((), ())), preferred_element_type=F32, precision=precision)


def _dot_tn(a, b, precision=None):
    return lax.dot_general(a, b, (((0,), (0,)), ((), ())), preferred_element_type=F32, precision=precision)


def _masked_softmax(s, mask):
    s = jnp.where(mask, s, NEG)
    m = jnp.max(s, axis=-1, keepdims=True)
    e = jnp.where(mask, jnp.exp(s - m), 0.0)
    return e / jnp.maximum(jnp.sum(e, axis=-1, keepdims=True), 1e-30)


def _select_top(imp, lane_idx, n_lanes):
    sel = jnp.zeros(imp.shape, F32)
    lane_idx = lane_idx.astype(F32)
    for _ in range(N_SEL):
        mx = jnp.max(imp, axis=-1, keepdims=True)
        idx = jnp.min(jnp.where(imp == mx, lane_idx, float(n_lanes)), axis=-1, keepdims=True)
        hit = lane_idx == idx
        sel = jnp.where(hit, 1.0, sel)
        imp = jnp.where(hit, -3e38, imp)
    return sel


def _adaln_body(c_ref, w_ref, b_ref, o_ref):
    s = _silu(c_ref[...]).astype(BF16)
    o_ref[...] = _dot(s, w_ref[...].astype(BF16)) + b_ref[...]


def _adaln(c_all, w_ada, b_ada):
    n_l, d, n = w_ada.shape
    r = c_all.shape[0]
    tn = 1536
    return pl.pallas_call(
        _adaln_body,
        grid=(n_l, n // tn),
        in_specs=[pl.BlockSpec((r, d), lambda l, j: (0, 0)),
                  pl.BlockSpec((None, d, tn), lambda l, j: (l, 0, j)),
                  pl.BlockSpec((None, 1, tn), lambda l, j: (l, 0, j))],
        out_specs=pl.BlockSpec((None, r, tn), lambda l, j: (l, 0, j)),
        out_shape=jax.ShapeDtypeStruct((n_l, r, n), F32),
        compiler_params=_cparams(("parallel", "parallel")),
        name="adaln_mod",
    )(c_all, w_ada, b_ada.reshape(n_l, 1, n))


def _inproj_body(x_ref, sc_ref, sh_ref, w_ref, *o_refs):
    h = (_ln_rows(x_ref[...]) * (1.0 + sc_ref[...]) + sh_ref[...]).astype(BF16)
    off = 0
    for o_ref, n in zip(o_refs, IN_OUT_WIDTHS):
        o_ref[...] = _dot(h, w_ref[:, off:off + n])
        off += n


def _mod_spec(mod, tm, rows_per_mod):
    if mod.shape[1] == 1:
        return pl.BlockSpec((None, 1, D_MODEL), lambda i, *_: ((i * tm) // rows_per_mod, 0, 0))
    return pl.BlockSpec((None, tm, D_MODEL), lambda i, *_: (0, i, 0))


def _inproj(x2d, sc, sh, w_perm, tm, rows_per_mod):
    m = x2d.shape[0]
    n_tot = w_perm.shape[1]
    return pl.pallas_call(
        _inproj_body,
        grid=(m // tm,),
        in_specs=[pl.BlockSpec((tm, D_MODEL), lambda i: (i, 0)),
                  _mod_spec(sc, tm, rows_per_mod), _mod_spec(sh, tm, rows_per_mod),
                  pl.BlockSpec((D_MODEL, n_tot), lambda i: (0, 0))],
        out_specs=[pl.BlockSpec((tm, n), lambda i: (i, 0)) for n in IN_OUT_WIDTHS],
        out_shape=[jax.ShapeDtypeStruct((m, n), F32) for n in IN_OUT_WIDTHS],
        compiler_params=_cparams(("parallel",)),
        name="ln_inproj",
    )(x2d, sc, sh, w_perm)


def _prep_body(qkv_ref, buf_ref, cw_ref, sm_ref, alog_ref, dt_ref, q_ref, k_ref, v_ref, bg_ref, xp_sc,
               *, tb, t_valid):
    i = pl.program_id(1)

    @pl.when(i == 0)
    def _():
        xp_sc[0:SUBLANES, :] = buf_ref[...]

    x = qkv_ref[...]
    xp_sc[SUBLANES:SUBLANES + tb, :] = x
    y = cw_ref[3:4, :] * x
    for j in range(CONV_TAPS - 1):
        lo = SUBLANES - (CONV_TAPS - 1) + j
        y = y + cw_ref[j:j + 1, :] * xp_sc[lo:lo + tb, :]
    xp_sc[0:SUBLANES, :] = xp_sc[tb:tb + SUBLANES, :]
    a = _silu(y)
    rows = i * tb + lax.broadcasted_iota(jnp.int32, (tb, 1), 0)
    live = rows < t_valid
    hk = GDN_HEADS * GDN_DK
    for h in range(GDN_HEADS):
        qh = a[:, h * GDN_DK:(h + 1) * GDN_DK]
        kh = a[:, hk + h * GDN_DK:hk + (h + 1) * GDN_DK]
        qn = qh * lax.rsqrt(jnp.sum(qh * qh, axis=-1, keepdims=True) + 1e-6) * (GDN_DK ** -0.5)
        kn = kh * lax.rsqrt(jnp.sum(kh * kh, axis=-1, keepdims=True) + 1e-6)
        q_ref[:, h * GDN_DK:(h + 1) * GDN_DK] = jnp.where(live, qn, 0.0)
        k_ref[:, h * GDN_DK:(h + 1) * GDN_DK] = jnp.where(live, kn, 0.0)
    v_ref[...] = jnp.where(live, a[:, 2 * hk:], 0.0)
    sm = sm_ref[...]
    beta = jax.nn.sigmoid(sm)
    z = sm + dt_ref[...]
    softplus = jnp.maximum(z, 0.0) + jnp.log(1.0 + jnp.exp(-jnp.abs(z)))
    g = -jnp.exp(alog_ref[...]) * softplus
    lane = lax.broadcasted_iota(jnp.int32, sm.shape, 1)
    bg_ref[...] = jnp.where(live, jnp.where(lane < GDN_HEADS, beta, g), 0.0)


def _gdn_prep(qkv, buf8, conv_w, small, alog_vec, dt_vec, t_valid):
    b, t, _ = qkv.shape
    tb = min(t, 512)
    hd = GDN_HEADS * GDN_DK
    row = lambda bi, i: (bi, i, 0)
    fixed = lambda bi, i: (0, 0)
    return pl.pallas_call(
        functools.partial(_prep_body, tb=tb, t_valid=t_valid),
        grid=(b, t // tb),
        in_specs=[pl.BlockSpec((None, tb, C_QKV), row),
                  pl.BlockSpec((None, SUBLANES, C_QKV), lambda bi, i: (bi, 0, 0)),
                  pl.BlockSpec((CONV_TAPS, C_QKV), fixed),
                  pl.BlockSpec((None, tb, _SMALL_W), row),
                  pl.BlockSpec((1, _SMALL_W), fixed), pl.BlockSpec((1, _SMALL_W), fixed)],
        out_specs=[pl.BlockSpec((None, tb, hd), row), pl.BlockSpec((None, tb, hd), row),
                   pl.BlockSpec((None, tb, hd), row), pl.BlockSpec((None, tb, _SMALL_W), row)],
        out_shape=[jax.ShapeDtypeStruct((b, t, hd), F32)] * 3 + [jax.ShapeDtypeStruct((b, t, _SMALL_W), F32)],
        scratch_shapes=[pltpu.VMEM((tb + SUBLANES, C_QKV), F32)],
        compiler_params=_cparams(("parallel", "arbitrary")),
        name="gdn_prep",
    )(qkv, buf8, conv_w, small, alog_vec, dt_vec)


def _gdn_body(q_ref, k_ref, v_ref, bg_ref, z_ref, s0_ref, nw_ref, o_ref, sfin_ref, s_sc, *, n_chunks):
    i = pl.program_id(1)
    c_len = GDN_CHUNK

    @pl.when(i == 0)
    def _():
        s_sc[...] = s0_ref[...]

    row = lax.broadcasted_iota(jnp.int32, (c_len, c_len), 0)
    col = lax.broadcasted_iota(jnp.int32, (c_len, c_len), 1)
    incl = row >= col
    strict = row > col
    incl_f = incl.astype(F32)
    strict_f = strict.astype(F32)
    eye = (row == col).astype(F32)
    nw = nw_ref[...]

    def chunk(c, carry):
        r0 = pl.multiple_of(c * c_len, c_len)
        bg = bg_ref[pl.ds(r0, c_len), :]
        for h in range(GDN_HEADS):
            sl = slice(h * GDN_DK, (h + 1) * GDN_DK)
            q = q_ref[pl.ds(r0, c_len), sl]
            k = k_ref[pl.ds(r0, c_len), sl]
            v = v_ref[pl.ds(r0, c_len), sl]
            beta = bg[:, h:h + 1]
            g = bg[:, GDN_HEADS + h:GDN_HEADS + h + 1]
            dlog = _dot(incl_f, g * strict_f, HIGHEST)
            gc = dlog[:, 0:1] + g[0:1, :]
            decay = jnp.where(incl, jnp.exp(dlog), 0.0)
            egc = jnp.exp(gc)
            g_last = gc[c_len - 1:c_len, :]
            kb = k * beta
            a = jnp.where(strict, _dot_nt(kb, k, HIGHEST) * decay, 0.0)
            p = -a
            tinv = eye + p
            for _ in range(5):
                p = _dot(p, p, HIGHEST)
                tinv = tinv + _dot(tinv, p, HIGHEST)
            rhs = jnp.concatenate([v * beta, kb * egc], axis=1)
            uw = _dot(tinv, rhs, HIGHEST)
            u = uw[:, :GDN_DV]
            w = uw[:, GDN_DV:]
            qk = jnp.where(incl, _dot_nt(q, k, HIGHEST) * decay, 0.0)
            s = s_sc[h]
            v_new = u - _dot(w, s, HIGHEST)
            o = _dot(q * egc, s, HIGHEST) + _dot(qk, v_new, HIGHEST)
            kd = k * jnp.exp(g_last - gc)
            s_sc[h] = s * jnp.exp(g_last) + _dot_tn(kd, v_new, HIGHEST)
            z = z_ref[pl.ds(r0, c_len), sl]
            on = o * lax.rsqrt(jnp.mean(o * o, axis=-1, keepdims=True) + EPS) * nw
            o_ref[pl.ds(r0, c_len), sl] = on * _silu(z)
        return carry

    lax.fori_loop(0, n_chunks, chunk, 0)

    @pl.when(i == pl.num_programs(1) - 1)
    def _():
        sfin_ref[...] = s_sc[...]


def _gdn(q, k, v, bg, z, s0, norm_w):
    b, t, hd = q.shape
    tb = min(t, 512)
    row = lambda bi, i: (bi, i, 0)
    st = lambda bi, i: (bi, 0, 0, 0)
    return pl.pallas_call(
        functools.partial(_gdn_body, n_chunks=tb // GDN_CHUNK),
        grid=(b, t // tb),
        in_specs=[pl.BlockSpec((None, tb, hd), row)] * 3
        + [pl.BlockSpec((None, tb, _SMALL_W), row), pl.BlockSpec((None, tb, hd), row),
           pl.BlockSpec((None, GDN_HEADS, GDN_DK, GDN_DV), st),
           pl.BlockSpec((1, GDN_DV), lambda bi, i: (0, 0))],
        out_specs=[pl.BlockSpec((None, tb, hd), row), pl.BlockSpec((None, GDN_HEADS, GDN_DK, GDN_DV), st)],
        out_shape=[jax.ShapeDtypeStruct((b, t, hd), F32),
                   jax.ShapeDtypeStruct((b, GDN_HEADS, GDN_DK, GDN_DV), F32)],
        scratch_shapes=[pltpu.VMEM((GDN_HEADS, GDN_DK, GDN_DV), F32)],
        compiler_params=_cparams(("parallel", "arbitrary")),
        name="gdn_chunked",
    )(q, k, v, bg, z, s0, norm_w)


_SEG_PER_PAGE = PAGE_ROWS // CMP_STRIDE
_SEG_W = CMP_STRIDE * KV_ROW
_CMP_PAGES_PER_STEP = 8


def _compress_body(pt_ref, *refs, pages_per_seq):
    npg = _CMP_PAGES_PER_STEP
    page_refs = refs[:npg]
    pe_ref, w1_ref, w2_ref, o_ref, carry_sc = refs[npg:]
    step = pl.program_id(0)
    first = (step % (pages_per_seq // npg)) == 0
    x = jnp.concatenate([r[...] for r in page_refs], axis=0)
    n_rows = x.shape[0]
    gh = NSA_GROUPS * CMP_HID
    part = []
    for m in range(CMP_LEN // CMP_STRIDE):
        xm = (x + pe_ref[m]).astype(BF16)
        acc = [jnp.zeros((n_rows, gh), F32) for _ in range(2)]
        for s in range(CMP_STRIDE):
            for kv in range(2):
                lo = s * KV_ROW + kv * LANES
                acc[kv] = acc[kv] + _dot(xm[:, lo:lo + LANES], w1_ref[m, s, kv])
        part.append(jnp.concatenate(acc, axis=1))
    prev = jnp.where(first, 0.0, carry_sc[...])
    rows = lax.broadcasted_iota(jnp.int32, (n_rows, 1), 0)
    shifted = jnp.where(rows == 0, prev, pltpu.roll(part[0], 1, 0))
    carry_sc[...] = part[0][n_rows - 1:n_rows, :]
    hid = _silu(shifted + part[1]).astype(BF16)
    for kv in range(2):
        o_ref[:, kv * LANES:(kv + 1) * LANES] = _dot(hid[:, kv * gh:(kv + 1) * gh], w2_ref[kv]).astype(BF16)


def _compress(pool, table, n_seq, pe, w1bd, w2bd):
    n_pages = pool.shape[0]
    pages_per_seq = table.shape[0] // n_seq
    npg = _CMP_PAGES_PER_STEP
    segs = pool.reshape(n_pages, _SEG_PER_PAGE, _SEG_W)
    n_steps = table.shape[0] // npg
    page_spec = lambda p: pl.BlockSpec((None, _SEG_PER_PAGE, _SEG_W), lambda i, pt: (pt[i * npg + p], 0, 0))
    fixed = lambda nd: (lambda i, pt: (0,) * nd)
    out = pl.pallas_call(
        functools.partial(_compress_body, pages_per_seq=pages_per_seq),
        grid_spec=pltpu.PrefetchScalarGridSpec(
            num_scalar_prefetch=1,
            grid=(n_steps,),
            in_specs=[page_spec(p) for p in range(npg)]
            + [pl.BlockSpec(pe.shape, fixed(3)), pl.BlockSpec(w1bd.shape, fixed(5)),
               pl.BlockSpec(w2bd.shape, fixed(3))],
            out_specs=pl.BlockSpec((npg * _SEG_PER_PAGE, KV_ROW), lambda i, pt: (i, 0)),
            scratch_shapes=[pltpu.VMEM((1, 2 * NSA_GROUPS * CMP_HID), F32)],
        ),
        out_shape=jax.ShapeDtypeStruct((n_steps * npg * _SEG_PER_PAGE, KV_ROW), BF16),
        compiler_params=_cparams(("arbitrary",)),
        name="nsa_compress",
    )(table, *([segs] * npg), pe, w1bd, w2bd)
    return out.reshape(n_seq, pages_per_seq * _SEG_PER_PAGE, KV_ROW)


def _compress_weights(cmp_pos, cmp_w1, cmp_w2):
    n_m = CMP_LEN // CMP_STRIDE
    pe = cmp_pos.reshape(2, n_m, CMP_STRIDE, 1, NSA_DH)
    pe = jnp.broadcast_to(pe, (2, n_m, CMP_STRIDE, NSA_GROUPS, NSA_DH))
    pe = jnp.transpose(pe, (1, 2, 0, 3, 4)).reshape(n_m, 1, _SEG_W)
    w1 = cmp_w1.reshape(2, n_m, CMP_STRIDE, NSA_DH, CMP_HID)
    w1 = jnp.transpose(w1, (1, 2, 0, 3, 4))
    eye = jnp.eye(NSA_GROUPS, dtype=F32)
    w1bd = jnp.einsum('mskdh,gG->mskgdGh', w1, eye).reshape(
        n_m, CMP_STRIDE, 2, NSA_GROUPS * NSA_DH, NSA_GROUPS * CMP_HID).astype(BF16)
    w2bd = jnp.einsum('khd,gG->kghGd', cmp_w2, eye).reshape(
        2, NSA_GROUPS * CMP_HID, NSA_GROUPS * NSA_DH).astype(BF16)
    return pe, w1bd, w2bd


def _overlap_matrix(n_rows, n_blk_lanes):
    ci = (np.arange(n_rows)[:, None] - 1) * CMP_STRIDE
    sj = np.arange(n_blk_lanes)[None, :] * SLC_BLK
    ov = (ci < sj + SLC_BLK) & (ci + CMP_LEN > sj) & (ci >= 0)
    return jnp.asarray(ov.astype(np.float32))


def _nsa_prompt_body(q_ref, sm_ref, cmp_ref, ks_ref, kw_ref, ov_ref, o_ref):
    iq = pl.program_id(1)
    qb = Q_BLK
    t0 = iq * qb
    n_cmp_rows = cmp_ref.shape[0]
    n_blk = ov_ref.shape[1]
    w_rows = WINDOW + qb
    t_col = t0 + lax.broadcasted_iota(jnp.int32, (qb, 1), 0)
    gates = jax.nn.sigmoid(sm_ref[...])
    ci = lax.broadcasted_iota(jnp.int32, (1, n_cmp_rows), 1)
    cmp_mask = ((ci >= 1) & (CMP_STRIDE * ci + (CMP_LEN - CMP_STRIDE - 1) <= t_col))[None]
    blk = lax.broadcasted_iota(jnp.int32, (1, n_blk), 1)
    forced = (blk == t_col // SLC_BLK) | (blk == 0)
    valid = blk * SLC_BLK <= t_col
    start = pl.multiple_of(jnp.maximum(t0 - WINDOW, 0), qb)
    wpos = start + lax.broadcasted_iota(jnp.int32, (1, w_rows), 1)
    dlt = t_col - wpos
    win_mask = ((dlt >= 0) & (dlt < WINDOW))[None]
    e_row = lax.broadcasted_iota(jnp.int32, (n_blk, qb), 0)
    e_col = lax.broadcasted_iota(jnp.int32, (n_blk, qb), 1) // SLC_BLK
    key_lane = lax.broadcasted_iota(jnp.int32, (1, qb), 1)
    scale = NSA_DH ** -0.5
    rq = NSA_REP * qb

    for g in range(NSA_GROUPS):
        klo = g * NSA_DH
        vlo = NSA_GROUPS * NSA_DH + g * NSA_DH
        q = jnp.concatenate(
            [q_ref[:, (g * NSA_REP + r) * NSA_DH:(g * NSA_REP + r + 1) * NSA_DH] for r in range(NSA_REP)],
            axis=0)
        q = (q * scale).astype(BF16)
        s = _dot_nt(q, cmp_ref[:, klo:klo + NSA_DH]).reshape(NSA_REP, qb, n_cmp_rows)
        p_c = _masked_softmax(s, cmp_mask)
        o_c = _dot(p_c.reshape(rq, n_cmp_rows).astype(BF16), cmp_ref[:, vlo:vlo + NSA_DH])
        imp = _dot(jnp.sum(p_c, axis=0), ov_ref[...], HIGHEST)
        imp = jnp.where(forced, 1e9, jnp.where(valid, imp, -1.0))
        sel = _select_top(imp, blk, n_blk).astype(BF16)

        def kv_step(kb, carry):
            m, l, acc = carry
            r0 = pl.multiple_of(kb * qb, qb)
            kblk = ks_ref[pl.ds(r0, qb), klo:klo + NSA_DH]
            vblk = ks_ref[pl.ds(r0, qb), vlo:vlo + NSA_DH]
            sc = _dot_nt(q, kblk).reshape(NSA_REP, qb, qb)
            expand = (e_row == (qb // SLC_BLK) * kb + e_col).astype(BF16)
            picked = _dot(sel, expand)
            msk = ((picked > 0.5) & (r0 + key_lane <= t_col))[None]
            sc = jnp.where(msk, sc, NEG)
            m_new = jnp.maximum(m, jnp.max(sc, axis=-1, keepdims=True))
            corr = jnp.exp(m - m_new)
            e = jnp.where(msk, jnp.exp(sc - m_new), 0.0)
            l = corr * l + jnp.sum(e, axis=-1, keepdims=True)
            pv = _dot(e.reshape(rq, qb).astype(BF16), vblk).reshape(NSA_REP, qb, NSA_DH)
            return m_new, l, corr * acc + pv

        init = (jnp.full((NSA_REP, qb, 1), NEG, F32), jnp.zeros((NSA_REP, qb, 1), F32),
                jnp.zeros((NSA_REP, qb, NSA_DH), F32))
        _, l, acc = lax.fori_loop(0, iq + 1, kv_step, init)
        o_s = acc / jnp.maximum(l, 1e-30)
        kw = kw_ref[pl.ds(start, w_rows), klo:klo + NSA_DH]
        vw = kw_ref[pl.ds(start, w_rows), vlo:vlo + NSA_DH]
        p_w = _masked_softmax(_dot_nt(q, kw).reshape(NSA_REP, qb, w_rows), win_mask)
        o_w = _dot(p_w.reshape(rq, w_rows).astype(BF16), vw)
        for r in range(NSA_REP):
            h = g * NSA_REP + r
            lane0 = 2 * GDN_HEADS + 3 * h
            rs = slice(r * qb, (r + 1) * qb)
            o_ref[:, h * NSA_DH:(h + 1) * NSA_DH] = (
                gates[:, lane0:lane0 + 1] * o_c[rs] + gates[:, lane0 + 1:lane0 + 2] * o_s[r]
                + gates[:, lane0 + 2:lane0 + 3] * o_w[rs])


def _nsa_prompt(q_b, small, cmp_rows, kv_s, kv_w):
    b, t, _ = q_b.shape
    n_blk = max(t // SLC_BLK, LANES)
    ov = _overlap_matrix(cmp_rows.shape[1], n_blk)
    whole = lambda bi, i: (bi, 0, 0)
    return pl.pallas_call(
        _nsa_prompt_body,
        grid=(b, t // Q_BLK),
        in_specs=[pl.BlockSpec((None, Q_BLK, NSA_HEADS * NSA_DH), lambda bi, i: (bi, i, 0)),
                  pl.BlockSpec((None, Q_BLK, _SMALL_W), lambda bi, i: (bi, i, 0)),
                  pl.BlockSpec((None,) + cmp_rows.shape[1:], whole),
                  pl.BlockSpec((None, t, KV_ROW), whole), pl.BlockSpec((None, t, KV_ROW), whole),
                  pl.BlockSpec(ov.shape, lambda bi, i: (0, 0))],
        out_specs=pl.BlockSpec((None, Q_BLK, NSA_HEADS * NSA_DH), lambda bi, i: (bi, i, 0)),
        out_shape=jax.ShapeDtypeStruct((b, t, NSA_HEADS * NSA_DH), F32),
        compiler_params=_cparams(("parallel", "arbitrary")),
        name="nsa_prompt",
    )(q_b, small, cmp_rows, kv_s, kv_w, ov)


_DEC_PAGES_PER_STEP = 8


def _nsa_decode_body(pt_ref, qbd_ref, sm_ref, cmp_ref, win_ref, new_ref, ov_ref, *refs, n_past_blk):
    npg = _DEC_PAGES_PER_STEP
    page_refs = refs[:npg]
    o_ref, sel_sc, m_sc, l_sc, acc_sc, oc_sc, ow_sc = refs[npg:]
    j = pl.program_id(1)
    n_blk = ov_ref.shape[1]
    gd = NSA_GROUPS * NSA_DH
    qf = qbd_ref[...] * (NSA_DH ** -0.5)
    q = qf.astype(BF16)
    head = lax.broadcasted_iota(jnp.int32, (NSA_HEADS, gd), 0)
    lane = lax.broadcasted_iota(jnp.int32, (NSA_HEADS, gd), 1)
    own = (lane // NSA_DH == head // NSA_REP).astype(F32)

    @pl.when(j == 0)
    def _():
        n_rows = cmp_ref.shape[0]
        ci = lax.broadcasted_iota(jnp.int32, (1, n_rows), 1)
        p_c = _masked_softmax(_dot_nt(q, cmp_ref[:, 0:gd]), ci >= 1)
        oc_sc[...] = _dot(p_c.astype(BF16), cmp_ref[:, gd:2 * gd]) * own
        hr = lax.broadcasted_iota(jnp.int32, (NSA_HEADS, NSA_HEADS), 0) // NSA_REP
        hc = lax.broadcasted_iota(jnp.int32, (NSA_HEADS, NSA_HEADS), 1) // NSA_REP
        p_grp = _dot((hr == hc).astype(F32), p_c, HIGHEST)
        imp = _dot(p_grp, ov_ref[...], HIGHEST)
        blk = lax.broadcasted_iota(jnp.int32, (1, n_blk), 1)
        imp = jnp.where((blk == n_past_blk) | (blk == 0), 1e9, jnp.where(blk < n_past_blk, imp, -2e38))
        sel_sc[...] = _select_top(imp, blk, n_blk)
        wi = lax.broadcasted_iota(jnp.int32, (1, win_ref.shape[0]), 1)
        p_w = _masked_softmax(_dot_nt(q, win_ref[:, 0:gd]), wi >= 0)
        ow_sc[...] = _dot(p_w.astype(BF16), win_ref[:, gd:2 * gd]) * own
        m_sc[...] = jnp.full(m_sc.shape, NEG, F32)
        l_sc[...] = jnp.zeros(l_sc.shape, F32)
        acc_sc[...] = jnp.zeros(acc_sc.shape, F32)

    sel = sel_sc[...].astype(BF16)
    e_row = lax.broadcasted_iota(jnp.int32, (n_blk, PAGE_ROWS), 0)
    e_col = lax.broadcasted_iota(jnp.int32, (n_blk, PAGE_ROWS), 1) // SLC_BLK
    for p in range(npg):
        page = page_refs[p][...].astype(BF16)
        first_blk = (PAGE_ROWS // SLC_BLK) * (j * npg + p)
        picked = _dot(sel, (e_row == first_blk + e_col).astype(BF16))
        msk = picked > 0.5
        sc = jnp.where(msk, _dot_nt(q, page[:, 0:gd]), NEG)
        m_old = m_sc[...]
        m_new = jnp.maximum(m_old, jnp.max(sc, axis=-1, keepdims=True))
        corr = jnp.exp(m_old - m_new)
        e = jnp.where(msk, jnp.exp(sc - m_new), 0.0)
        l_sc[...] = corr * l_sc[...] + jnp.sum(e, axis=-1, keepdims=True)
        acc_sc[...] = corr * acc_sc[...] + _dot(e.astype(BF16), page[:, gd:2 * gd])
        m_sc[...] = m_new

    @pl.when(j == pl.num_programs(1) - 1)
    def _():
        new = new_ref[...]
        s_new = jnp.sum(qf * new[:, 0:gd], axis=-1, keepdims=True)
        m_old = m_sc[...]
        m_new = jnp.maximum(m_old, s_new)
        corr = jnp.exp(m_old - m_new)
        e = jnp.exp(s_new - m_new)
        l = corr * l_sc[...] + e
        acc = corr * acc_sc[...] + e * new[:, gd:2 * gd]
        o_s = acc / jnp.maximum(l, 1e-30) * own
        gates = jax.nn.sigmoid(sm_ref[...])
        glane = lax.broadcasted_iota(jnp.int32, (NSA_HEADS, _SMALL_W), 1)
        ghead = lax.broadcasted_iota(jnp.int32, (NSA_HEADS, _SMALL_W), 0)
        gate = [jnp.sum(jnp.where(glane == 2 * GDN_HEADS + 3 * ghead + br, gates, 0.0), axis=-1, keepdims=True)
                for br in range(3)]
        o = gate[0] * oc_sc[...] + gate[1] * o_s + gate[2] * ow_sc[...]
        o_ref[...] = o[:, 0:NSA_DH] + o[:, NSA_DH:gd]


def _nsa_decode(qbd, small, cmp_rows, win_rows, new_rows, pool, table):
    b = qbd.shape[0]
    npg = _DEC_PAGES_PER_STEP
    pages_per_seq = table.shape[0] // b
    n_past_blk = pages_per_seq * (PAGE_ROWS // SLC_BLK)
    n_blk = -(-(n_past_blk + 1) // LANES) * LANES
    ov = _overlap_matrix(cmp_rows.shape[1], n_blk)
    per_seq = lambda bi, j, pt: (bi, 0, 0)
    page_spec = lambda p: pl.BlockSpec(
        (None, PAGE_ROWS, KV_ROW), lambda bi, j, pt: (pt[bi * pages_per_seq + j * npg + p], 0, 0))
    gd = NSA_GROUPS * NSA_DH
    out = pl.pallas_call(
        functools.partial(_nsa_decode_body, n_past_blk=n_past_blk),
        grid_spec=pltpu.PrefetchScalarGridSpec(
            num_scalar_prefetch=1,
            grid=(b, pages_per_seq // npg),
            in_specs=[pl.BlockSpec((None, NSA_HEADS, gd), per_seq),
                      pl.BlockSpec((None, 1, _SMALL_W), per_seq),
                      pl.BlockSpec((None,) + cmp_rows.shape[1:], per_seq),
                      pl.BlockSpec((None,) + win_rows.shape[1:], per_seq),
                      pl.BlockSpec((None, 1, KV_ROW), per_seq),
                      pl.BlockSpec(ov.shape, lambda bi, j, pt: (0, 0))]
            + [page_spec(p) for p in range(npg)],
            out_specs=pl.BlockSpec((None, NSA_HEADS, NSA_DH), per_seq),
            scratch_shapes=[pltpu.VMEM((NSA_HEADS, n_blk), F32), pltpu.VMEM((NSA_HEADS, 1), F32),
                            pltpu.VMEM((NSA_HEADS, 1), F32), pltpu.VMEM((NSA_HEADS, gd), F32),
                            pltpu.VMEM((NSA_HEADS, gd), F32), pltpu.VMEM((NSA_HEADS, gd), F32)],
        ),
        out_shape=jax.ShapeDtypeStruct((b, NSA_HEADS, NSA_DH), F32),
        compiler_params=_cparams(("parallel", "arbitrary")),
        name="nsa_decode",
    )(table, qbd, small, cmp_rows, win_rows, new_rows, ov, *([pool] * npg))
    return out.reshape(b, NSA_HEADS * NSA_DH)


def _outproj_body(oa_ref, ob_ref, gm_ref, x_ref, gt_ref, wa_ref, wb_ref, wo_ref, lg_ref, lb_ref, o_ref):
    ya = _dot(oa_ref[...].astype(BF16), wa_ref[...])
    yb = _dot(ob_ref[...].astype(BF16), wb_ref[...])
    gm = jax.nn.sigmoid(gm_ref[...])
    u = (gm[:, :D_MODEL] * ya + gm[:, D_MODEL:] * yb).astype(BF16)
    y = _dot(u, wo_ref[...])
    xr = ALPHA * x_ref[...] + gt_ref[...] * y
    o_ref[...] = _ln_rows(xr) * lg_ref[...] + lb_ref[...]


def _outproj(o_a, o_b, g_m, x2d, gt, w_up_a, w_up_b, w_out, ln_g, ln_b, tm, rows_per_mod):
    m = x2d.shape[0]
    row = lambda n: pl.BlockSpec((tm, n), lambda i: (i, 0))
    fixed = lambda shape: pl.BlockSpec(shape, lambda i: (0, 0))
    return pl.pallas_call(
        _outproj_body,
        grid=(m // tm,),
        in_specs=[row(o_a.shape[1]), row(o_b.shape[1]), row(2 * D_MODEL), row(D_MODEL),
                  _mod_spec(gt, tm, rows_per_mod),
                  fixed(w_up_a.shape), fixed(w_up_b.shape), fixed(w_out.shape),
                  fixed((1, D_MODEL)), fixed((1, D_MODEL))],
        out_specs=row(D_MODEL),
        out_shape=jax.ShapeDtypeStruct((m, D_MODEL), F32),
        compiler_params=_cparams(("parallel",)),
        name="mixer_outproj",
    )(o_a, o_b, g_m, x2d, gt, w_up_a, w_up_b, w_out, ln_g, ln_b)


_FF_TILE = 256


def _ffn_body(x_ref, sc_ref, sh_ref, gt_ref, wr_ref, wg_ref, wu_ref, wd_ref, lg_ref, lb_ref, o_ref,
              h_sc, w_sc, acc_sc, *, routed):
    e = pl.program_id(1)
    f = pl.program_id(2)

    @pl.when((e == 0) & (f == 0))
    def _():
        h = _ln_rows(x_ref[...]) * (1.0 + sc_ref[...]) + sh_ref[...]
        h_sc[...] = h.astype(BF16)
        acc_sc[...] = jnp.zeros(acc_sc.shape, F32)
        if routed:
            lane = lax.broadcasted_iota(jnp.int32, (1, LANES), 1)
            logits = jnp.where(lane < N_EXP, _dot(h, wr_ref[...], HIGHEST), NEG)
            ex = jnp.exp(logits - jnp.max(logits, axis=-1, keepdims=True))
            probs = ex / jnp.sum(ex, axis=-1, keepdims=True)
            lane_f = lane.astype(F32)
            p1 = jnp.max(probs, axis=-1, keepdims=True)
            i1 = jnp.min(jnp.where(probs == p1, lane_f, float(LANES)), axis=-1, keepdims=True)
            rest = jnp.where(lane_f == i1, -1.0, probs)
            p2 = jnp.max(rest, axis=-1, keepdims=True)
            i2 = jnp.min(jnp.where(rest == p2, lane_f, float(LANES)), axis=-1, keepdims=True)
            w_sc[...] = (jnp.where(lane_f == i1, p1, 0.0) + jnp.where(lane_f == i2, p2, 0.0)) / (p1 + p2)

    hb = h_sc[...]
    hid = _silu(_dot(hb, wg_ref[...])) * _dot(hb, wu_ref[...])
    if routed:
        lane = lax.broadcasted_iota(jnp.int32, (1, LANES), 1)
        hid = hid * jnp.sum(jnp.where(lane == e, w_sc[...], 0.0), axis=-1, keepdims=True)
    acc_sc[...] += _dot(hid.astype(BF16), wd_ref[...])

    @pl.when((e == pl.num_programs(1) - 1) & (f == pl.num_programs(2) - 1))
    def _():
        xr = ALPHA * x_ref[...] + gt_ref[...] * acc_sc[...]
        o_ref[...] = _ln_rows(xr) * lg_ref[...] + lb_ref[...]


def _ffn(x2d, sc, sh, gt, w_router, w_gu, w_down, ln_g, ln_b, tm, rows_per_mod, routed):
    m = x2d.shape[0]
    n_e = w_gu.shape[0]
    tf = _FF_TILE
    n_f = D_FF // tf
    row = pl.BlockSpec((tm, D_MODEL), lambda i, e, f: (i, 0))
    fixed = lambda shape: pl.BlockSpec(shape, lambda i, e, f: (0, 0))
    mod = lambda a: _mod_spec(a, tm, rows_per_mod)
    return pl.pallas_call(
        functools.partial(_ffn_body, routed=routed),
        grid=(m // tm, n_e, n_f),
        in_specs=[row, mod(sc), mod(sh), mod(gt), fixed(w_router.shape),
                  pl.BlockSpec((None, D_MODEL, tf), lambda i, e, f: (e, 0, f)),
                  pl.BlockSpec((None, D_MODEL, tf), lambda i, e, f: (e, 0, f + n_f)),
                  pl.BlockSpec((None, tf, D_MODEL), lambda i, e, f: (e, f, 0)),
                  fixed((1, D_MODEL)), fixed((1, D_MODEL))],
        out_specs=row,
        out_shape=jax.ShapeDtypeStruct((m, D_MODEL), F32),
        scratch_shapes=[pltpu.VMEM((tm, D_MODEL), BF16), pltpu.VMEM((tm, LANES), F32),
                        pltpu.VMEM((tm, D_MODEL), F32)],
        compiler_params=_cparams(("parallel", "arbitrary", "arbitrary")),
        name="ffn_routed" if routed else "ffn_dense",
    )(x2d, sc, sh, gt, w_router, w_gu, w_gu, w_down, ln_g, ln_b)


def _permute_w_in(w):
    sizes = [C_QKV, GDN_HEADS * GDN_DV, GDN_HEADS, GDN_HEADS, NSA_HEADS * NSA_DH, KV_ROW, KV_ROW, KV_ROW,
             3 * NSA_HEADS, 2 * D_MODEL]
    offs = np.cumsum([0] + sizes)
    qkv, z, b_a, a_a, q_b, kv_c, kv_s, kv_w, g_b, g_m = (w[:, offs[i]:offs[i + 1]] for i in range(len(sizes)))
    pad = jnp.zeros((w.shape[0], _SMALL_W - 2 * GDN_HEADS - 3 * NSA_HEADS), w.dtype)
    return jnp.concatenate([qkv, z, g_m, q_b, kv_c, kv_s, kv_w, b_a, a_a, g_b, pad], axis=1).astype(BF16)


def _layer_weights(l, w_in, conv_w, a_log, dt_bias, norm_w_a, cmp_pos, cmp_w1, cmp_w2, w_up_a, w_up_b, w_out,
                   ln_g, ln_b):
    lane_pad = lambda v: jnp.zeros((1, _SMALL_W), F32).at[0, GDN_HEADS:2 * GDN_HEADS].set(v)
    pe, w1bd, w2bd = _compress_weights(cmp_pos[l], cmp_w1[l], cmp_w2[l])
    return dict(
        w_in=_permute_w_in(w_in[l]), conv_w=conv_w[l], alog=lane_pad(a_log[l]), dt=lane_pad(dt_bias[l]),
        norm_w=norm_w_a[l].reshape(1, GDN_DV), pe=pe, w1bd=w1bd, w2bd=w2bd,
        w_up_a=w_up_a[l].astype(BF16), w_up_b=w_up_b[l].astype(BF16), w_out=w_out[l].astype(BF16),
        ln_g0=ln_g[l, 0].reshape(1, D_MODEL), ln_b0=ln_b[l, 0].reshape(1, D_MODEL),
        ln_g1=ln_g[l, 1].reshape(1, D_MODEL), ln_b1=ln_b[l, 1].reshape(1, D_MODEL))


def _mixer_common(x2d, mods, wts, b, t, t_pad, conv_buf, s0, tm, rows_per_mod):
    sh_m, sc_m = mods[0], mods[1]
    qkv, z, g_m, q_b, kv_c, kv_s, kv_w, small = _inproj(x2d, sc_m, sh_m, wts['w_in'], tm, rows_per_mod)
    seq = lambda a: a.reshape(b, t, a.shape[-1])
    padt = lambda a: jnp.pad(seq(a), ((0, 0), (0, t_pad - t), (0, 0)))
    buf8 = jnp.pad(conv_buf, ((0, 0), (SUBLANES - (CONV_TAPS - 1), 0), (0, 0)))
    qa, ka, va, bg = _gdn_prep(padt(qkv), buf8, wts['conv_w'], padt(small), wts['alog'], wts['dt'], t)
    o_a, s_new = _gdn(qa, ka, va, bg, padt(z), s0, wts['norm_w'])
    o_a = o_a[:, :t].reshape(b * t, GDN_HEADS * GDN_DV)
    conv_new = jnp.concatenate([conv_buf, seq(qkv)], axis=1)[:, -(CONV_TAPS - 1):]
    return (g_m, q_b, kv_c, kv_s, kv_w, small), o_a, s_new, conv_new


def _rows5(a, b, t):
    return a.reshape(b, t, 2, NSA_GROUPS, NSA_DH)


def _prompt_layer(x, mods, wts, ffn_args, routed):
    b, t, _ = x.shape
    x2d = x.reshape(b * t, D_MODEL)
    tm = 256
    zeros_buf = jnp.zeros((b, CONV_TAPS - 1, C_QKV), F32)
    zeros_s = jnp.zeros((b, GDN_HEADS, GDN_DK, GDN_DV), F32)
    (g_m, q_b, kv_c, kv_s, kv_w, small), o_a, s_new, conv_new = _mixer_common(
        x2d, mods, wts, b, t, t, zeros_buf, zeros_s, tm, t)
    pages = (b * t) // PAGE_ROWS
    cmp_rows = _compress(kv_c.reshape(pages, PAGE_ROWS, KV_ROW), jnp.arange(pages, dtype=jnp.int32), b,
                         wts['pe'], wts['w1bd'], wts['w2bd'])
    seq = lambda a: a.reshape(b, t, a.shape[-1])
    o_b = _nsa_prompt(seq(q_b), seq(small), cmp_rows, seq(kv_s).astype(BF16), seq(kv_w).astype(BF16))
    x1 = _outproj(o_a, o_b.reshape(b * t, -1), g_m, x2d, mods[2], wts['w_up_a'], wts['w_up_b'], wts['w_out'],
                  wts['ln_g0'], wts['ln_b0'], 512, t)
    x2 = _ffn(x1, mods[4], mods[3], mods[5], *ffn_args, wts['ln_g1'], wts['ln_b1'], min(1024, t), t, routed)
    win_new = _rows5(kv_w, b, t)[:, t - min(WINDOW, t):]
    return x2.reshape(b, t, D_MODEL), (s_new, conv_new, _rows5(kv_c, b, t), _rows5(kv_s, b, t), win_new)


def _sample_layer(x, mods, wts, ffn_args, routed, conv_buf, s0, pool_c, pool_s, win_buf, table):
    b = x.shape[0]
    x2d = x.reshape(b, D_MODEL)
    (g_m, q_b, kv_c, kv_s, kv_w, small), o_a, s_new, conv_new = _mixer_common(
        x2d, mods, wts, b, 1, GDN_CHUNK, conv_buf, s0, b, b)
    cmp_rows = _compress(pool_c, table, b, wts['pe'], wts['w1bd'], wts['w2bd'])
    win_new = jnp.concatenate([win_buf.reshape(b, -1, KV_ROW)[:, 1:], kv_w[:, None, :]], axis=1)
    q4 = q_b.reshape(b, NSA_GROUPS, NSA_REP, 1, NSA_DH)
    eye = jnp.eye(NSA_GROUPS, dtype=F32).reshape(NSA_GROUPS, 1, NSA_GROUPS, 1)
    qbd = (q4 * eye).reshape(b, NSA_HEADS, NSA_GROUPS * NSA_DH)
    o_b = _nsa_decode(qbd, small.reshape(b, 1, _SMALL_W), cmp_rows, win_new.astype(BF16),
                      kv_s.reshape(b, 1, KV_ROW), pool_s, table)
    x1 = _outproj(o_a, o_b, g_m, x2d, mods[2], wts['w_up_a'], wts['w_up_b'], wts['w_out'],
                  wts['ln_g0'], wts['ln_b0'], b, b)
    x2 = _ffn(x1, mods[4], mods[3], mods[5], *ffn_args, wts['ln_g1'], wts['ln_b1'], b, b, routed)
    return x2.reshape(b, 1, D_MODEL), (s_new, conv_new, _rows5(kv_c, b, 1), _rows5(kv_s, b, 1),
                                       _rows5(win_new, b, win_new.shape[1]))


def kernel(x_prompt, x_sample, state_delta, state_conv, cache_cmp_kv, cache_slc_kv, state_win_kv, page_table,
           c_prompt, c_sample, w_ada, b_ada, w_in, conv_w, a_log, dt_bias, norm_w_a, cmp_pos, cmp_w1, cmp_w2,
           w_up_a, w_up_b, w_out, ln_g, ln_b, ffn_w_gu, ffn_w_down, moe_router, moe_w_gu, moe_w_down):
    bp = x_prompt.shape[0]
    db = x_sample.shape[0]
    n_layers = w_in.shape[0]
    n_pool = cache_cmp_kv.shape[1]
    mod_all = _adaln(jnp.concatenate([c_prompt, c_sample], axis=0), w_ada, b_ada)
    table = page_table.reshape(-1)
    xp, xs = x_prompt, x_sample
    st_p, st_s = [], []
    for l in range(n_layers):
        wts = _layer_weights(l, w_in, conv_w, a_log, dt_bias, norm_w_a, cmp_pos, cmp_w1, cmp_w2, w_up_a, w_up_b,
                             w_out, ln_g, ln_b)
        routed = l % 2 == 1
        if routed:
            router = jnp.pad(moe_router[l // 2], ((0, 0), (0, LANES - N_EXP)))
            ffn_args = (router, moe_w_gu[l // 2].astype(BF16), moe_w_down[l // 2].astype(BF16))
        else:
            unused_router = jnp.zeros((D_MODEL, LANES), F32)
            ffn_args = (unused_router, ffn_w_gu[l // 2][None].astype(BF16), ffn_w_down[l // 2][None].astype(BF16))
        mod6 = mod_all[l].reshape(bp + db, 6, D_MODEL)
        mods_p = [mod6[:bp, i].reshape(bp, 1, D_MODEL) for i in range(6)]
        mods_s = [mod6[bp:, i].reshape(1, db, D_MODEL) for i in range(6)]
        xp, sp = _prompt_layer(xp, mods_p, wts, ffn_args, routed)
        xs, ss = _sample_layer(xs, mods_s, wts, ffn_args, routed, state_conv[l], state_delta[l],
                               cache_cmp_kv[l].reshape(n_pool, PAGE_ROWS, KV_ROW),
                               cache_slc_kv[l].reshape(n_pool, PAGE_ROWS, KV_ROW), state_win_kv[l], table)
        st_p.append(sp)
        st_s.append(ss)
    stack = lambda sts, i: jnp.stack([s[i] for s in sts])
    return (xp, xs) + tuple(stack(st_p, i) for i in range(5)) + tuple(stack(st_s, i) for i in range(5))
```

```python
import functools
import math

import jax
import jax.numpy as jnp
import numpy as np
from jax import lax
from jax.experimental import pallas as pl
from jax.experimental.pallas import tpu as pltpu

F32 = jnp.float32
BF16 = jnp.bfloat16
HIGHEST = lax.Precision.HIGHEST

D_MODEL = 1024
N_LAYERS = 2
PAGE_ROWS = 128
GDN_HEADS = 4
GDN_DK = 128
GDN_DV = 128
CONV_TAPS = 4
GDN_CHUNK = 64
C_QKV = 2 * GDN_HEADS * GDN_DK + GDN_HEADS * GDN_DV
NSA_HEADS = 8
NSA_GROUPS = 2
NSA_REP = NSA_HEADS // NSA_GROUPS
NSA_DH = 64
KV_ROW = 2 * NSA_GROUPS * NSA_DH
CMP_LEN = 32
CMP_STRIDE = 16
CMP_HID = 256
SLC_BLK = 64
N_SEL = 16
WINDOW = 512
Q_BLK = 128
D_FF = 2816
N_EXP = 8
ALPHA = (2 * N_LAYERS) ** 0.25
EPS = 1e-5
NEG = -1e30

LANES = 128
SUBLANES = 8
VMEM_LIMIT = 56 * 1024 * 1024

_SMALL_W = LANES
IN_OUT_WIDTHS = (C_QKV, GDN_HEADS * GDN_DV, 2 * D_MODEL, NSA_HEADS * NSA_DH, KV_ROW, KV_ROW, KV_ROW, _SMALL_W)


def _cparams(sem):
    return pltpu.CompilerParams(dimension_semantics=sem, vmem_limit_bytes=VMEM_LIMIT)


def _silu(x):
    return x * jax.nn.sigmoid(x)


def _ln_rows(x):
    mu = jnp.mean(x, axis=-1, keepdims=True)
    xc = x - mu
    var = jnp.mean(xc * xc, axis=-1, keepdims=True)
    return xc * lax.rsqrt(var + EPS)


def _dot(a, b, precision=None):
    return jnp.dot(a, b, preferred_element_type=F32, precision=precision)


def _dot_nt(a, b, precision=None):
    return lax.dot_general(a, b, (((1,), (1,)), ((), ())), preferred_element_type=F32, precision=precision)


def _dot_tn(a, b, precision=None):
    return lax.dot_general(a, b, (((0,), (0,)), ((), ())), preferred_element_type=F32, precision=precision)


def _masked_softmax(s, mask, axis=-1):
    s = jnp.where(mask, s, NEG)
    m = jnp.max(s, axis=axis, keepdims=True)
    e = jnp.where(mask, jnp.exp(s - m), 0.0)
    return e * (1.0 / jnp.maximum(jnp.sum(e, axis=axis, keepdims=True), 1e-30))


def _select_top(imp, lane_idx, n_lanes, axis=-1):
    sel = jnp.zeros(imp.shape, F32)
    lane_idx = lane_idx.astype(F32)
    for _ in range(N_SEL):
        mx = jnp.max(imp, axis=axis, keepdims=True)
        idx = jnp.min(jnp.where(imp == mx, lane_idx, float(n_lanes)), axis=axis, keepdims=True)
        hit = lane_idx == idx
        sel = jnp.where(hit, 1.0, sel)
        imp = jnp.where(hit, -3e38, imp)
    return sel


def _adaln_body(c_ref, w_ref, b_ref, o_ref):
    s = _silu(c_ref[...]).astype(BF16)
    o_ref[...] = _dot(s, w_ref[...].astype(BF16)) + b_ref[...]


def _adaln(c_all, w_ada, b_ada):
    n_l, d, n = w_ada.shape
    r = c_all.shape[0]
    tn = 1536
    return pl.pallas_call(
        _adaln_body,
        grid=(n_l, n // tn),
        in_specs=[pl.BlockSpec((r, d), lambda l, j: (0, 0)),
                  pl.BlockSpec((None, d, tn), lambda l, j: (l, 0, j)),
                  pl.BlockSpec((None, 1, tn), lambda l, j: (l, 0, j))],
        out_specs=pl.BlockSpec((None, r, tn), lambda l, j: (l, 0, j)),
        out_shape=jax.ShapeDtypeStruct((n_l, r, n), F32),
        compiler_params=_cparams(("parallel", "parallel")),
        name="adaln_mod",
    )(c_all, w_ada, b_ada.reshape(n_l, 1, n))


def _inproj_body(x_ref, sc_ref, sh_ref, w_ref, *o_refs):
    h = (_ln_rows(x_ref[...]) * (1.0 + sc_ref[...]) + sh_ref[...]).astype(BF16)
    off = 0
    for o_ref, n in zip(o_refs, IN_OUT_WIDTHS):
        o_ref[...] = _dot(h, w_ref[:, off:off + n])
        off += n


def _mod_spec(mod, tm, rows_per_mod):
    if mod.shape[1] == 1:
        return pl.BlockSpec((None, 1, D_MODEL), lambda i, *_: ((i * tm) // rows_per_mod, 0, 0))
    return pl.BlockSpec((None, tm, D_MODEL), lambda i, *_: (0, i, 0))


def _inproj(x2d, sc, sh, w_perm, tm, rows_per_mod):
    m = x2d.shape[0]
    n_tot = w_perm.shape[1]
    return pl.pallas_call(
        _inproj_body,
        grid=(m // tm,),
        in_specs=[pl.BlockSpec((tm, D_MODEL), lambda i: (i, 0)),
                  _mod_spec(sc, tm, rows_per_mod), _mod_spec(sh, tm, rows_per_mod),
                  pl.BlockSpec((D_MODEL, n_tot), lambda i: (0, 0))],
        out_specs=[pl.BlockSpec((tm, n), lambda i: (i, 0)) for n in IN_OUT_WIDTHS],
        out_shape=[jax.ShapeDtypeStruct((m, n), F32) for n in IN_OUT_WIDTHS],
        compiler_params=_cparams(("parallel",)),
        name="ln_inproj",
    )(x2d, sc, sh, w_perm)


def _prep_body(qkv_ref, buf_ref, cw_ref, sm_ref, alog_ref, dt_ref, q_ref, k_ref, v_ref, bg_ref, xp_sc,
               *, tb, t_valid):
    i = pl.program_id(1)

    @pl.when(i == 0)
    def _():
        xp_sc[0:SUBLANES, :] = buf_ref[...]

    x = qkv_ref[...]
    xp_sc[SUBLANES:SUBLANES + tb, :] = x
    y = cw_ref[3:4, :] * x
    for j in range(CONV_TAPS - 1):
        lo = SUBLANES - (CONV_TAPS - 1) + j
        y = y + cw_ref[j:j + 1, :] * xp_sc[lo:lo + tb, :]
    xp_sc[0:SUBLANES, :] = xp_sc[tb:tb + SUBLANES, :]
    a = _silu(y)
    rows = i * tb + lax.broadcasted_iota(jnp.int32, (tb, 1), 0)
    live = rows < t_valid
    hk = GDN_HEADS * GDN_DK
    for h in range(GDN_HEADS):
        qh = a[:, h * GDN_DK:(h + 1) * GDN_DK]
        kh = a[:, hk + h * GDN_DK:hk + (h + 1) * GDN_DK]
        qn = qh * lax.rsqrt(jnp.sum(qh * qh, axis=-1, keepdims=True) + 1e-6) * (GDN_DK ** -0.5)
        kn = kh * lax.rsqrt(jnp.sum(kh * kh, axis=-1, keepdims=True) + 1e-6)
        q_ref[:, h * GDN_DK:(h + 1) * GDN_DK] = jnp.where(live, qn, 0.0)
        k_ref[:, h * GDN_DK:(h + 1) * GDN_DK] = jnp.where(live, kn, 0.0)
    v_ref[...] = jnp.where(live, a[:, 2 * hk:], 0.0)
    sm = sm_ref[...]
    beta = jax.nn.sigmoid(sm)
    z = sm + dt_ref[...]
    softplus = jnp.maximum(z, 0.0) + jnp.log(1.0 + jnp.exp(-jnp.abs(z)))
    g = -jnp.exp(alog_ref[...]) * softplus
    lane = lax.broadcasted_iota(jnp.int32, sm.shape, 1)
    bg_ref[...] = jnp.where(live, jnp.where(lane < GDN_HEADS, beta, g), 0.0)


def _gdn_prep(qkv, buf8, conv_w, small, alog_vec, dt_vec, t_valid):
    b, t, _ = qkv.shape
    tb = min(t, 512)
    hd = GDN_HEADS * GDN_DK
    row = lambda bi, i: (bi, i, 0)
    fixed = lambda bi, i: (0, 0)
    return pl.pallas_call(
        functools.partial(_prep_body, tb=tb, t_valid=t_valid),
        grid=(b, t // tb),
        in_specs=[pl.BlockSpec((None, tb, C_QKV), row),
                  pl.BlockSpec((None, SUBLANES, C_QKV), lambda bi, i: (bi, 0, 0)),
                  pl.BlockSpec((CONV_TAPS, C_QKV), fixed),
                  pl.BlockSpec((None, tb, _SMALL_W), row),
                  pl.BlockSpec((1, _SMALL_W), fixed), pl.BlockSpec((1, _SMALL_W), fixed)],
        out_specs=[pl.BlockSpec((None, tb, hd), row), pl.BlockSpec((None, tb, hd), row),
                   pl.BlockSpec((None, tb, hd), row), pl.BlockSpec((None, tb, _SMALL_W), row)],
        out_shape=[jax.ShapeDtypeStruct((b, t, hd), F32)] * 3 + [jax.ShapeDtypeStruct((b, t, _SMALL_W), F32)],
        scratch_shapes=[pltpu.VMEM((tb + SUBLANES, C_QKV), F32)],
        compiler_params=_cparams(("parallel", "arbitrary")),
        name="gdn_prep",
    )(qkv, buf8, conv_w, small, alog_vec, dt_vec)


def _bmm(a, b, precision=HIGHEST):
    return lax.dot_general(a, b, (((2,), (1,)), ((0,), (0,))), preferred_element_type=F32, precision=precision)


def _bmm_nt(a, b, precision=HIGHEST):
    return lax.dot_general(a, b, (((2,), (2,)), ((0,), (0,))), preferred_element_type=F32, precision=precision)


def _gdn_ut_body(q_ref, k_ref, v_ref, bg_ref, u_ref, w_ref, qd_ref, kd_ref, qk_ref, gl_ref, *, n_chunks):
    c_len = GDN_CHUNK
    nc = n_chunks
    tb = nc * c_len
    row = lax.broadcasted_iota(jnp.int32, (nc, c_len, c_len), 1)
    col = lax.broadcasted_iota(jnp.int32, (nc, c_len, c_len), 2)
    incl = row >= col
    strict = row > col
    incl_f = incl.astype(F32)
    strict_f = strict.astype(F32)
    eye = (row == col).astype(F32)
    bg = bg_ref[...]
    lane = lax.broadcasted_iota(jnp.int32, (tb, _SMALL_W), 1)
    gl_all = jnp.zeros((tb, _SMALL_W), F32)
    split = lambda x: x.reshape(nc, c_len, x.shape[-1])
    for h in range(GDN_HEADS):
        sl = slice(h * GDN_DK, (h + 1) * GDN_DK)
        q = split(q_ref[:, sl])
        k = split(k_ref[:, sl])
        v = split(v_ref[:, sl])
        beta = split(bg[:, h:h + 1])
        g = split(bg[:, GDN_HEADS + h:GDN_HEADS + h + 1])
        dlog = _bmm(incl_f, g * strict_f)
        gc = dlog[:, :, 0:1] + g[:, 0:1, :]
        decay = jnp.where(incl, jnp.exp(dlog), 0.0)
        egc = jnp.exp(gc)
        g_last = gc[:, c_len - 1:c_len, :]
        kb = k * beta
        a = jnp.where(strict, _bmm_nt(kb, k) * decay, 0.0)
        p = -a
        tinv = eye + p
        for _ in range(5):
            p = _bmm(p, p)
            tinv = tinv + _bmm(tinv, p)
        u_ref[:, sl] = _bmm(tinv, v * beta).reshape(tb, GDN_DV)
        w_ref[:, sl] = _bmm(tinv, kb * egc).reshape(tb, GDN_DK)
        qk = jnp.where(incl, _bmm_nt(q, k) * decay, 0.0)
        qk_ref[:, h * c_len:(h + 1) * c_len] = qk.reshape(tb, c_len)
        qd_ref[:, sl] = (q * egc).reshape(tb, GDN_DK)
        kd_ref[:, sl] = (k * jnp.exp(g_last - gc)).reshape(tb, GDN_DK)
        gl = jnp.broadcast_to(jnp.exp(g_last), (nc, c_len, 1)).reshape(tb, 1)
        gl_all = jnp.where(lane == h, gl, gl_all)
    gl_ref[...] = gl_all


def _gdn_scan_body(u_ref, w_ref, qd_ref, kd_ref, qk_ref, gl_ref, z_ref, s0_ref, nw_ref, o_ref, sfin_ref, s_sc,
                   *, n_chunks, n_seq):
    i = pl.program_id(1)
    c_len = GDN_CHUNK

    @pl.when(i == 0)
    def _():
        s_sc[...] = s0_ref[...]

    nw = nw_ref[...]

    def chunk(c, carry):
        r0 = pl.multiple_of(c * c_len, c_len)
        rows = pl.ds(r0, c_len)
        gl_row = gl_ref[:, pl.ds(r0, 1), :]
        for h in range(GDN_HEADS):
            sl = slice(h * GDN_DK, (h + 1) * GDN_DK)
            s = s_sc[:, h]
            v_new = u_ref[:, rows, sl] - _bmm(w_ref[:, rows, sl], s)
            o = _bmm(qd_ref[:, rows, sl], s) + _bmm(qk_ref[:, rows, h * c_len:(h + 1) * c_len], v_new)
            kd = kd_ref[:, rows, sl]
            upd = jnp.stack([_dot_tn(kd[b], v_new[b], HIGHEST) for b in range(n_seq)])
            s_sc[:, h] = s * gl_row[:, :, h:h + 1] + upd
            z = z_ref[:, rows, sl]
            on = o * lax.rsqrt(jnp.mean(o * o, axis=-1, keepdims=True) + EPS) * nw
            o_ref[:, rows, sl] = on * _silu(z)
        return carry

    lax.fori_loop(0, n_chunks, chunk, 0)

    @pl.when(i == pl.num_programs(1) - 1)
    def _():
        sfin_ref[...] = s_sc[...]


_GDN_SEQ_PER_STEP = 8


def _gdn(q, k, v, bg, z, s0, norm_w):
    b, t, hd = q.shape
    tb = min(t, 512)
    nc = tb // GDN_CHUNK
    row = lambda bi, i: (bi, i, 0)
    wide = lambda n: pl.BlockSpec((None, tb, n), row)
    qkw = GDN_HEADS * GDN_CHUNK
    u, w, qd, kd, qk, gl = pl.pallas_call(
        functools.partial(_gdn_ut_body, n_chunks=nc),
        grid=(b, t // tb),
        in_specs=[wide(hd)] * 3 + [wide(_SMALL_W)],
        out_specs=[wide(hd)] * 4 + [wide(qkw), wide(_SMALL_W)],
        out_shape=[jax.ShapeDtypeStruct((b, t, hd), F32)] * 4
        + [jax.ShapeDtypeStruct((b, t, qkw), F32), jax.ShapeDtypeStruct((b, t, _SMALL_W), F32)],
        compiler_params=_cparams(("parallel", "parallel")),
        name="gdn_ut",
    )(q, k, v, bg)
    nb = min(b, _GDN_SEQ_PER_STEP)
    grp = lambda gi, i: (gi, i, 0)
    st = lambda gi, i: (gi, 0, 0, 0)
    seqs = lambda n: pl.BlockSpec((nb, tb, n), grp)
    return pl.pallas_call(
        functools.partial(_gdn_scan_body, n_chunks=nc, n_seq=nb),
        grid=(b // nb, t // tb),
        in_specs=[seqs(hd)] * 4 + [seqs(qkw), seqs(_SMALL_W), seqs(hd),
                                   pl.BlockSpec((nb, GDN_HEADS, GDN_DK, GDN_DV), st),
                                   pl.BlockSpec((1, GDN_DV), lambda gi, i: (0, 0))],
        out_specs=[seqs(hd), pl.BlockSpec((nb, GDN_HEADS, GDN_DK, GDN_DV), st)],
        out_shape=[jax.ShapeDtypeStruct((b, t, hd), F32),
                   jax.ShapeDtypeStruct((b, GDN_HEADS, GDN_DK, GDN_DV), F32)],
        scratch_shapes=[pltpu.VMEM((nb, GDN_HEADS, GDN_DK, GDN_DV), F32)],
        compiler_params=_cparams(("parallel", "arbitrary")),
        name="gdn_scan",
    )(u, w, qd, kd, qk, gl, z, s0, norm_w)


_SEG_PER_PAGE = PAGE_ROWS // CMP_STRIDE
_SEG_W = CMP_STRIDE * KV_ROW
_CMP_PAGES_PER_STEP = 8


def _compress_body(pt_ref, *refs, pages_per_seq):
    npg = _CMP_PAGES_PER_STEP
    page_refs = refs[:npg]
    pe_ref, w1_ref, w2_ref, o_ref, carry_sc = refs[npg:]
    step = pl.program_id(0)
    first = (step % (pages_per_seq // npg)) == 0
    x = jnp.concatenate([r[...] for r in page_refs], axis=0)
    n_rows = x.shape[0]
    gh = NSA_GROUPS * CMP_HID
    part = []
    for m in range(CMP_LEN // CMP_STRIDE):
        xm = (x + pe_ref[m]).astype(BF16)
        acc = [jnp.zeros((n_rows, gh), F32) for _ in range(2)]
        for s in range(CMP_STRIDE):
            for kv in range(2):
                lo = s * KV_ROW + kv * LANES
                acc[kv] = acc[kv] + _dot(xm[:, lo:lo + LANES], w1_ref[m, s, kv])
        part.append(jnp.concatenate(acc, axis=1))
    prev = jnp.where(first, 0.0, carry_sc[...])
    rows = lax.broadcasted_iota(jnp.int32, (n_rows, 1), 0)
    shifted = jnp.where(rows == 0, prev, pltpu.roll(part[0], 1, 0))
    carry_sc[...] = part[0][n_rows - 1:n_rows, :]
    hid = _silu(shifted + part[1]).astype(BF16)
    for kv in range(2):
        o_ref[:, kv * LANES:(kv + 1) * LANES] = _dot(hid[:, kv * gh:(kv + 1) * gh], w2_ref[kv]).astype(BF16)


def _compress(pool, table, n_seq, pe, w1bd, w2bd):
    n_pages = pool.shape[0]
    pages_per_seq = table.shape[0] // n_seq
    npg = _CMP_PAGES_PER_STEP
    segs = pool.reshape(n_pages, _SEG_PER_PAGE, _SEG_W)
    n_steps = table.shape[0] // npg
    page_spec = lambda p: pl.BlockSpec((None, _SEG_PER_PAGE, _SEG_W), lambda i, pt: (pt[i * npg + p], 0, 0))
    fixed = lambda nd: (lambda i, pt: (0,) * nd)
    out = pl.pallas_call(
        functools.partial(_compress_body, pages_per_seq=pages_per_seq),
        grid_spec=pltpu.PrefetchScalarGridSpec(
            num_scalar_prefetch=1,
            grid=(n_steps,),
            in_specs=[page_spec(p) for p in range(npg)]
            + [pl.BlockSpec(pe.shape, fixed(3)), pl.BlockSpec(w1bd.shape, fixed(5)),
               pl.BlockSpec(w2bd.shape, fixed(3))],
            out_specs=pl.BlockSpec((npg * _SEG_PER_PAGE, KV_ROW), lambda i, pt: (i, 0)),
            scratch_shapes=[pltpu.VMEM((1, 2 * NSA_GROUPS * CMP_HID), F32)],
        ),
        out_shape=jax.ShapeDtypeStruct((n_steps * npg * _SEG_PER_PAGE, KV_ROW), BF16),
        compiler_params=_cparams(("arbitrary",)),
        name="nsa_compress",
    )(table, *([segs] * npg), pe, w1bd, w2bd)
    return out.reshape(n_seq, pages_per_seq * _SEG_PER_PAGE, KV_ROW)


def _compress_weights(cmp_pos, cmp_w1, cmp_w2):
    n_m = CMP_LEN // CMP_STRIDE
    pe = cmp_pos.reshape(2, n_m, CMP_STRIDE, 1, NSA_DH)
    pe = jnp.broadcast_to(pe, (2, n_m, CMP_STRIDE, NSA_GROUPS, NSA_DH))
    pe = jnp.transpose(pe, (1, 2, 0, 3, 4)).reshape(n_m, 1, _SEG_W)
    w1 = cmp_w1.reshape(2, n_m, CMP_STRIDE, NSA_DH, CMP_HID)
    w1 = jnp.transpose(w1, (1, 2, 0, 3, 4))
    eye = jnp.eye(NSA_GROUPS, dtype=F32)
    w1bd = jnp.einsum('mskdh,gG->mskgdGh', w1, eye).reshape(
        n_m, CMP_STRIDE, 2, NSA_GROUPS * NSA_DH, NSA_GROUPS * CMP_HID).astype(BF16)
    w2bd = jnp.einsum('khd,gG->kghGd', cmp_w2, eye).reshape(
        2, NSA_GROUPS * CMP_HID, NSA_GROUPS * NSA_DH).astype(BF16)
    return pe, w1bd, w2bd


def _overlap_matrix(n_rows, n_blk_lanes):
    ci = (np.arange(n_rows)[:, None] - 1) * CMP_STRIDE
    sj = np.arange(n_blk_lanes)[None, :] * SLC_BLK
    ov = (ci < sj + SLC_BLK) & (ci + CMP_LEN > sj) & (ci >= 0)
    return jnp.asarray(ov.astype(np.float32))


_SEL_KEY_BLOCK = 512


def _nsa_prompt_body(qt_ref, smt_ref, ck_ref, cvt_ref, sk_ref, svt_ref, wk_ref, wvt_ref, ovt_ref, o_ref,
                     m_sc, l_sc, acc_sc):
    iq = pl.program_id(1)
    qb = Q_BLK
    t0 = iq * qb
    n_cmp_rows = ck_ref.shape[0]
    n_blk = ovt_ref.shape[0]
    w_rows = WINDOW + qb
    gd = NSA_GROUPS * NSA_DH
    t_row = t0 + lax.broadcasted_iota(jnp.int32, (1, qb), 1)
    gates = jax.nn.sigmoid(smt_ref[...])
    ci = lax.broadcasted_iota(jnp.int32, (n_cmp_rows, 1), 0)
    cmp_mask = (ci >= 1) & (CMP_STRIDE * ci + (CMP_LEN - CMP_STRIDE - 1) <= t_row)
    blk = lax.broadcasted_iota(jnp.int32, (n_blk, 1), 0)
    forced = (blk == t_row // SLC_BLK) | (blk == 0)
    valid = blk * SLC_BLK <= t_row
    start = pl.multiple_of(jnp.maximum(t0 - WINDOW, 0), qb)
    wpos = start + lax.broadcasted_iota(jnp.int32, (w_rows, 1), 0)
    dlt = t_row - wpos
    win_mask = (dlt >= 0) & (dlt < WINDOW)
    kblk = min(_SEL_KEY_BLOCK, sk_ref.shape[0])
    e_key = lax.broadcasted_iota(jnp.int32, (kblk, n_blk), 0) // SLC_BLK
    e_blk = lax.broadcasted_iota(jnp.int32, (kblk, n_blk), 1)
    key_col = lax.broadcasted_iota(jnp.int32, (kblk, 1), 0)
    feat = lax.broadcasted_iota(jnp.int32, (gd, 1), 0)
    scale = NSA_DH ** -0.5
    heads = lambda x: [x[:, r * qb:(r + 1) * qb] for r in range(NSA_REP)]

    for g in range(NSA_GROUPS):
        vrows = slice(g * NSA_DH, (g + 1) * NSA_DH)
        qt = jnp.concatenate([qt_ref[(g * NSA_REP + r) * NSA_DH:(g * NSA_REP + r + 1) * NSA_DH, :]
                              for r in range(NSA_REP)], axis=1) * scale
        qt = jnp.concatenate([qt, jnp.zeros_like(qt)] if g == 0 else [jnp.zeros_like(qt), qt], axis=0)
        qt = qt.astype(BF16)
        s_c = _dot(ck_ref[...], qt)
        p_c = [_masked_softmax(s, cmp_mask, axis=0) for s in heads(s_c)]
        o_c = _dot(cvt_ref[vrows, :], jnp.concatenate(p_c, axis=1).astype(BF16))
        imp = _dot(ovt_ref[...], p_c[0] + p_c[1] + p_c[2] + p_c[3], HIGHEST)
        imp = jnp.where(forced, 1e9, jnp.where(valid, imp, -1.0))
        sel = _select_top(imp, blk, n_blk, axis=0).astype(BF16)

        m_sc[...] = jnp.full(m_sc.shape, NEG, F32)
        l_sc[...] = jnp.zeros(l_sc.shape, F32)
        acc_sc[...] = jnp.zeros(acc_sc.shape, F32)

        def kv_step(kb, carry):
            r0 = pl.multiple_of(kb * kblk, kblk)
            s_s = _dot(sk_ref[pl.ds(r0, kblk), :], qt)
            expand = (e_blk == (kblk // SLC_BLK) * kb + e_key).astype(BF16)
            msk = (_dot(expand, sel) > 0.5) & (r0 + key_col <= t_row)
            m_old = m_sc[...]
            l_old = l_sc[...]
            m_new, l_new, es = [], [], []
            for r, s in enumerate(heads(s_s)):
                s = jnp.where(msk, s, NEG)
                m_r = jnp.maximum(m_old[:, r * qb:(r + 1) * qb], jnp.max(s, axis=0, keepdims=True))
                e = jnp.exp(s - m_r)
                m_new.append(m_r)
                es.append(e.astype(BF16))
                l_new.append(jnp.sum(e, axis=0, keepdims=True))
            m_new = jnp.concatenate(m_new, axis=1)
            corr = jnp.exp(m_old - m_new)
            m_sc[...] = m_new
            l_sc[...] = corr * l_old + jnp.concatenate(l_new, axis=1)
            pv = _dot(svt_ref[vrows, pl.ds(r0, kblk)], jnp.concatenate(es, axis=1))
            acc_sc[...] = corr * acc_sc[...] + pv
            return carry

        lax.fori_loop(0, (t0 + qb + kblk - 1) // kblk, kv_step, 0)
        o_s = acc_sc[...] * (1.0 / jnp.maximum(l_sc[...], 1e-30))
        s_w = _dot(wk_ref[pl.ds(start, w_rows), :], qt)
        p_w = jnp.concatenate([_masked_softmax(s, win_mask, axis=0).astype(BF16) for s in heads(s_w)], axis=1)
        o_w = _dot(wvt_ref[vrows, pl.ds(start, w_rows)], p_w)
        for r in range(NSA_REP):
            h = g * NSA_REP + r
            row0 = 2 * GDN_HEADS + 3 * h
            cs = slice(r * qb, (r + 1) * qb)
            o_ref[h * NSA_DH:(h + 1) * NSA_DH, :] = (
                gates[row0:row0 + 1, :] * o_c[:, cs] + gates[row0 + 1:row0 + 2, :] * o_s[:, cs]
                + gates[row0 + 2:row0 + 3, :] * o_w[:, cs])


def _nsa_prompt(q_b, small, cmp_rows, kv_s, kv_w):
    b, t, _ = q_b.shape
    gd = NSA_GROUPS * NSA_DH
    n_blk = max(t // SLC_BLK, LANES)
    ovt = _overlap_matrix(cmp_rows.shape[1], n_blk).T
    tr = lambda a: jnp.swapaxes(a, 1, 2)
    keys = lambda kv: kv[:, :, :gd].astype(BF16)
    vals_t = lambda kv: tr(kv[:, :, gd:]).astype(BF16)
    whole = lambda bi, i: (bi, 0, 0)
    tok = lambda bi, i: (bi, 0, i)
    rq = NSA_REP * Q_BLK
    out_t = pl.pallas_call(
        _nsa_prompt_body,
        grid=(b, t // Q_BLK),
        in_specs=[pl.BlockSpec((None, NSA_HEADS * NSA_DH, Q_BLK), tok),
                  pl.BlockSpec((None, _SMALL_W, Q_BLK), tok),
                  pl.BlockSpec((None, cmp_rows.shape[1], gd), whole),
                  pl.BlockSpec((None, gd, cmp_rows.shape[1]), whole),
                  pl.BlockSpec((None, t, gd), whole), pl.BlockSpec((None, gd, t), whole),
                  pl.BlockSpec((None, t, gd), whole), pl.BlockSpec((None, gd, t), whole),
                  pl.BlockSpec(ovt.shape, lambda bi, i: (0, 0))],
        out_specs=pl.BlockSpec((None, NSA_HEADS * NSA_DH, Q_BLK), tok),
        out_shape=jax.ShapeDtypeStruct((b, NSA_HEADS * NSA_DH, t), F32),
        scratch_shapes=[pltpu.VMEM((1, rq), F32), pltpu.VMEM((1, rq), F32), pltpu.VMEM((NSA_DH, rq), F32)],
        compiler_params=_cparams(("parallel", "arbitrary")),
        name="nsa_prompt",
    )(tr(q_b), tr(small), cmp_rows[:, :, :gd], tr(cmp_rows[:, :, gd:]), keys(kv_s), vals_t(kv_s),
      keys(kv_w), vals_t(kv_w), ovt)
    return tr(out_t)


_DEC_PAGES_PER_STEP = 8


def _nsa_decode_body(pt_ref, qbd_ref, sm_ref, cmp_ref, win_ref, new_ref, ov_ref, *refs, n_past_blk):
    npg = _DEC_PAGES_PER_STEP
    page_refs = refs[:npg]
    o_ref, sel_sc, m_sc, l_sc, acc_sc, oc_sc, ow_sc = refs[npg:]
    j = pl.program_id(1)
    n_blk = ov_ref.shape[1]
    gd = NSA_GROUPS * NSA_DH
    qf = qbd_ref[...] * (NSA_DH ** -0.5)
    q = qf.astype(BF16)
    head = lax.broadcasted_iota(jnp.int32, (NSA_HEADS, gd), 0)
    lane = lax.broadcasted_iota(jnp.int32, (NSA_HEADS, gd), 1)
    own = (lane // NSA_DH == head // NSA_REP).astype(F32)

    @pl.when(j == 0)
    def _():
        n_rows = cmp_ref.shape[0]
        ci = lax.broadcasted_iota(jnp.int32, (1, n_rows), 1)
        p_c = _masked_softmax(_dot_nt(q, cmp_ref[:, 0:gd]), ci >= 1)
        oc_sc[...] = _dot(p_c.astype(BF16), cmp_ref[:, gd:2 * gd]) * own
        hr = lax.broadcasted_iota(jnp.int32, (NSA_HEADS, NSA_HEADS), 0) // NSA_REP
        hc = lax.broadcasted_iota(jnp.int32, (NSA_HEADS, NSA_HEADS), 1) // NSA_REP
        p_grp = _dot((hr == hc).astype(F32), p_c, HIGHEST)
        imp = _dot(p_grp, ov_ref[...], HIGHEST)
        blk = lax.broadcasted_iota(jnp.int32, (1, n_blk), 1)
        imp = jnp.where((blk == n_past_blk) | (blk == 0), 1e9, jnp.where(blk < n_past_blk, imp, -2e38))
        sel_sc[...] = _select_top(imp, blk, n_blk)
        wi = lax.broadcasted_iota(jnp.int32, (1, win_ref.shape[0]), 1)
        p_w = _masked_softmax(_dot_nt(q, win_ref[:, 0:gd]), wi >= 0)
        ow_sc[...] = _dot(p_w.astype(BF16), win_ref[:, gd:2 * gd]) * own
        m_sc[...] = jnp.full(m_sc.shape, NEG, F32)
        l_sc[...] = jnp.zeros(l_sc.shape, F32)
        acc_sc[...] = jnp.zeros(acc_sc.shape, F32)

    sel = sel_sc[...].astype(BF16)
    e_row = lax.broadcasted_iota(jnp.int32, (n_blk, PAGE_ROWS), 0)
    e_col = lax.broadcasted_iota(jnp.int32, (n_blk, PAGE_ROWS), 1) // SLC_BLK
    for p in range(npg):
        page = page_refs[p][...].astype(BF16)
        first_blk = (PAGE_ROWS // SLC_BLK) * (j * npg + p)
        picked = _dot(sel, (e_row == first_blk + e_col).astype(BF16))
        msk = picked > 0.5
        sc = jnp.where(msk, _dot_nt(q, page[:, 0:gd]), NEG)
        m_old = m_sc[...]
        m_new = jnp.maximum(m_old, jnp.max(sc, axis=-1, keepdims=True))
        corr = jnp.exp(m_old - m_new)
        e = jnp.where(msk, jnp.exp(sc - m_new), 0.0)
        l_sc[...] = corr * l_sc[...] + jnp.sum(e, axis=-1, keepdims=True)
        acc_sc[...] = corr * acc_sc[...] + _dot(e.astype(BF16), page[:, gd:2 * gd])
        m_sc[...] = m_new

    @pl.when(j == pl.num_programs(1) - 1)
    def _():
        new = new_ref[...]
        s_new = jnp.sum(qf * new[:, 0:gd], axis=-1, keepdims=True)
        m_old = m_sc[...]
        m_new = jnp.maximum(m_old, s_new)
        corr = jnp.exp(m_old - m_new)
        e = jnp.exp(s_new - m_new)
        l = corr * l_sc[...] + e
        acc = corr * acc_sc[...] + e * new[:, gd:2 * gd]
        o_s = acc / jnp.maximum(l, 1e-30) * own
        gates = jax.nn.sigmoid(sm_ref[...])
        glane = lax.broadcasted_iota(jnp.int32, (NSA_HEADS, _SMALL_W), 1)
        ghead = lax.broadcasted_iota(jnp.int32, (NSA_HEADS, _SMALL_W), 0)
        gate = [jnp.sum(jnp.where(glane == 2 * GDN_HEADS + 3 * ghead + br, gates, 0.0), axis=-1, keepdims=True)
                for br in range(3)]
        o = gate[0] * oc_sc[...] + gate[1] * o_s + gate[2] * ow_sc[...]
        o_ref[...] = o[:, 0:NSA_DH] + o[:, NSA_DH:gd]


def _nsa_decode(qbd, small, cmp_rows, win_rows, new_rows, pool, table):
    b = qbd.shape[0]
    npg = _DEC_PAGES_PER_STEP
    pages_per_seq = table.shape[0] // b
    n_past_blk = pages_per_seq * (PAGE_ROWS // SLC_BLK)
    n_blk = -(-(n_past_blk + 1) // LANES) * LANES
    ov = _overlap_matrix(cmp_rows.shape[1], n_blk)
    per_seq = lambda bi, j, pt: (bi, 0, 0)
    page_spec = lambda p: pl.BlockSpec(
        (None, PAGE_ROWS, KV_ROW), lambda bi, j, pt: (pt[bi * pages_per_seq + j * npg + p], 0, 0))
    gd = NSA_GROUPS * NSA_DH
    out = pl.pallas_call(
        functools.partial(_nsa_decode_body, n_past_blk=n_past_blk),
        grid_spec=pltpu.PrefetchScalarGridSpec(
            num_scalar_prefetch=1,
            grid=(b, pages_per_seq // npg),
            in_specs=[pl.BlockSpec((None, NSA_HEADS, gd), per_seq),
                      pl.BlockSpec((None, 1, _SMALL_W), per_seq),
                      pl.BlockSpec((None,) + cmp_rows.shape[1:], per_seq),
                      pl.BlockSpec((None,) + win_rows.shape[1:], per_seq),
                      pl.BlockSpec((None, 1, KV_ROW), per_seq),
                      pl.BlockSpec(ov.shape, lambda bi, j, pt: (0, 0))]
            + [page_spec(p) for p in range(npg)],
            out_specs=pl.BlockSpec((None, NSA_HEADS, NSA_DH), per_seq),
            scratch_shapes=[pltpu.VMEM((NSA_HEADS, n_blk), F32), pltpu.VMEM((NSA_HEADS, 1), F32),
                            pltpu.VMEM((NSA_HEADS, 1), F32), pltpu.VMEM((NSA_HEADS, gd), F32),
                            pltpu.VMEM((NSA_HEADS, gd), F32), pltpu.VMEM((NSA_HEADS, gd), F32)],
        ),
        out_shape=jax.ShapeDtypeStruct((b, NSA_HEADS, NSA_DH), F32),
        compiler_params=_cparams(("parallel", "arbitrary")),
        name="nsa_decode",
    )(table, qbd, small, cmp_rows, win_rows, new_rows, ov, *([pool] * npg))
    return out.reshape(b, NSA_HEADS * NSA_DH)


def _outproj_body(oa_ref, ob_ref, gm_ref, x_ref, gt_ref, wa_ref, wb_ref, wo_ref, lg_ref, lb_ref, o_ref):
    ya = _dot(oa_ref[...].astype(BF16), wa_ref[...])
    yb = _dot(ob_ref[...].astype(BF16), wb_ref[...])
    gm = jax.nn.sigmoid(gm_ref[...])
    u = (gm[:, :D_MODEL] * ya + gm[:, D_MODEL:] * yb).astype(BF16)
    y = _dot(u, wo_ref[...])
    xr = ALPHA * x_ref[...] + gt_ref[...] * y
    o_ref[...] = _ln_rows(xr) * lg_ref[...] + lb_ref[...]


def _outproj(o_a, o_b, g_m, x2d, gt, w_up_a, w_up_b, w_out, ln_g, ln_b, tm, rows_per_mod):
    m = x2d.shape[0]
    row = lambda n: pl.BlockSpec((tm, n), lambda i: (i, 0))
    fixed = lambda shape: pl.BlockSpec(shape, lambda i: (0, 0))
    return pl.pallas_call(
        _outproj_body,
        grid=(m // tm,),
        in_specs=[row(o_a.shape[1]), row(o_b.shape[1]), row(2 * D_MODEL), row(D_MODEL),
                  _mod_spec(gt, tm, rows_per_mod),
                  fixed(w_up_a.shape), fixed(w_up_b.shape), fixed(w_out.shape),
                  fixed((1, D_MODEL)), fixed((1, D_MODEL))],
        out_specs=row(D_MODEL),
        out_shape=jax.ShapeDtypeStruct((m, D_MODEL), F32),
        compiler_params=_cparams(("parallel",)),
        name="mixer_outproj",
    )(o_a, o_b, g_m, x2d, gt, w_up_a, w_up_b, w_out, ln_g, ln_b)


_FF_TILE = 256


def _ffn_body(x_ref, sc_ref, sh_ref, gt_ref, wr_ref, wg_ref, wu_ref, wd_ref, lg_ref, lb_ref, o_ref,
              h_sc, w_sc, acc_sc, *, routed):
    e = pl.program_id(1)
    f = pl.program_id(2)

    @pl.when((e == 0) & (f == 0))
    def _():
        h = _ln_rows(x_ref[...]) * (1.0 + sc_ref[...]) + sh_ref[...]
        h_sc[...] = h.astype(BF16)
        acc_sc[...] = jnp.zeros(acc_sc.shape, F32)
        if routed:
            lane = lax.broadcasted_iota(jnp.int32, (1, LANES), 1)
            logits = jnp.where(lane < N_EXP, _dot(h, wr_ref[...], HIGHEST), NEG)
            ex = jnp.exp(logits - jnp.max(logits, axis=-1, keepdims=True))
            probs = ex / jnp.sum(ex, axis=-1, keepdims=True)
            lane_f = lane.astype(F32)
            p1 = jnp.max(probs, axis=-1, keepdims=True)
            i1 = jnp.min(jnp.where(probs == p1, lane_f, float(LANES)), axis=-1, keepdims=True)
            rest = jnp.where(lane_f == i1, -1.0, probs)
            p2 = jnp.max(rest, axis=-1, keepdims=True)
            i2 = jnp.min(jnp.where(rest == p2, lane_f, float(LANES)), axis=-1, keepdims=True)
            w_sc[...] = (jnp.where(lane_f == i1, p1, 0.0) + jnp.where(lane_f == i2, p2, 0.0)) / (p1 + p2)

    hb = h_sc[...]
    hid = _silu(_dot(hb, wg_ref[...])) * _dot(hb, wu_ref[...])
    if routed:
        lane = lax.broadcasted_iota(jnp.int32, (1, LANES), 1)
        hid = hid * jnp.sum(jnp.where(lane == e, w_sc[...], 0.0), axis=-1, keepdims=True)
    acc_sc[...] += _dot(hid.astype(BF16), wd_ref[...])

    @pl.when((e == pl.num_programs(1) - 1) & (f == pl.num_programs(2) - 1))
    def _():
        xr = ALPHA * x_ref[...] + gt_ref[...] * acc_sc[...]
        o_ref[...] = _ln_rows(xr) * lg_ref[...] + lb_ref[...]


def _ffn(x2d, sc, sh, gt, w_router, w_gu, w_down, ln_g, ln_b, tm, rows_per_mod, routed):
    m = x2d.shape[0]
    n_e = w_gu.shape[0]
    tf = _FF_TILE
    n_f = D_FF // tf
    row = pl.BlockSpec((tm, D_MODEL), lambda i, e, f: (i, 0))
    fixed = lambda shape: pl.BlockSpec(shape, lambda i, e, f: (0, 0))
    mod = lambda a: _mod_spec(a, tm, rows_per_mod)
    return pl.pallas_call(
        functools.partial(_ffn_body, routed=routed),
        grid=(m // tm, n_e, n_f),
        in_specs=[row, mod(sc), mod(sh), mod(gt), fixed(w_router.shape),
                  pl.BlockSpec((None, D_MODEL, tf), lambda i, e, f: (e, 0, f)),
                  pl.BlockSpec((None, D_MODEL, tf), lambda i, e, f: (e, 0, f + n_f)),
                  pl.BlockSpec((None, tf, D_MODEL), lambda i, e, f: (e, f, 0)),
                  fixed((1, D_MODEL)), fixed((1, D_MODEL))],
        out_specs=row,
        out_shape=jax.ShapeDtypeStruct((m, D_MODEL), F32),
        scratch_shapes=[pltpu.VMEM((tm, D_MODEL), BF16), pltpu.VMEM((tm, LANES), F32),
                        pltpu.VMEM((tm, D_MODEL), F32)],
        compiler_params=_cparams(("parallel", "arbitrary", "arbitrary")),
        name="ffn_routed" if routed else "ffn_dense",
    )(x2d, sc, sh, gt, w_router, w_gu, w_gu, w_down, ln_g, ln_b)


def _permute_w_in(w):
    sizes = [C_QKV, GDN_HEADS * GDN_DV, GDN_HEADS, GDN_HEADS, NSA_HEADS * NSA_DH, KV_ROW, KV_ROW, KV_ROW,
             3 * NSA_HEADS, 2 * D_MODEL]
    offs = np.cumsum([0] + sizes)
    qkv, z, b_a, a_a, q_b, kv_c, kv_s, kv_w, g_b, g_m = (w[:, offs[i]:offs[i + 1]] for i in range(len(sizes)))
    pad = jnp.zeros((w.shape[0], _SMALL_W - 2 * GDN_HEADS - 3 * NSA_HEADS), w.dtype)
    return jnp.concatenate([qkv, z, g_m, q_b, kv_c, kv_s, kv_w, b_a, a_a, g_b, pad], axis=1).astype(BF16)


def _layer_weights(l, w_in, conv_w, a_log, dt_bias, norm_w_a, cmp_pos, cmp_w1, cmp_w2, w_up_a, w_up_b, w_out,
                   ln_g, ln_b):
    lane_pad = lambda v: jnp.zeros((1, _SMALL_W), F32).at[0, GDN_HEADS:2 * GDN_HEADS].set(v)
    pe, w1bd, w2bd = _compress_weights(cmp_pos[l], cmp_w1[l], cmp_w2[l])
    return dict(
        w_in=_permute_w_in(w_in[l]), conv_w=conv_w[l], alog=lane_pad(a_log[l]), dt=lane_pad(dt_bias[l]),
        norm_w=norm_w_a[l].reshape(1, GDN_DV), pe=pe, w1bd=w1bd, w2bd=w2bd,
        w_up_a=w_up_a[l].astype(BF16), w_up_b=w_up_b[l].astype(BF16), w_out=w_out[l].astype(BF16),
        ln_g0=ln_g[l, 0].reshape(1, D_MODEL), ln_b0=ln_b[l, 0].reshape(1, D_MODEL),
        ln_g1=ln_g[l, 1].reshape(1, D_MODEL), ln_b1=ln_b[l, 1].reshape(1, D_MODEL))


def _mixer_common(x2d, mods, wts, b, t, t_pad, conv_buf, s0, tm, rows_per_mod):
    sh_m, sc_m = mods[0], mods[1]
    qkv, z, g_m, q_b, kv_c, kv_s, kv_w, small = _inproj(x2d, sc_m, sh_m, wts['w_in'], tm, rows_per_mod)
    seq = lambda a: a.reshape(b, t, a.shape[-1])
    padt = lambda a: jnp.pad(seq(a), ((0, 0), (0, t_pad - t), (0, 0)))
    buf8 = jnp.pad(conv_buf, ((0, 0), (SUBLANES - (CONV_TAPS - 1), 0), (0, 0)))
    qa, ka, va, bg = _gdn_prep(padt(qkv), buf8, wts['conv_w'], padt(small), wts['alog'], wts['dt'], t)
    o_a, s_new = _gdn(qa, ka, va, bg, padt(z), s0, wts['norm_w'])
    o_a = o_a[:, :t].reshape(b * t, GDN_HEADS * GDN_DV)
    conv_new = jnp.concatenate([conv_buf, seq(qkv)], axis=1)[:, -(CONV_TAPS - 1):]
    return (g_m, q_b, kv_c, kv_s, kv_w, small), o_a, s_new, conv_new


def _rows5(a, b, t):
    return a.reshape(b, t, 2, NSA_GROUPS, NSA_DH)


def _prompt_layer(x, mods, wts, ffn_args, routed):
    b, t, _ = x.shape
    x2d = x.reshape(b * t, D_MODEL)
    tm = 256
    zeros_buf = jnp.zeros((b, CONV_TAPS - 1, C_QKV), F32)
    zeros_s = jnp.zeros((b, GDN_HEADS, GDN_DK, GDN_DV), F32)
    (g_m, q_b, kv_c, kv_s, kv_w, small), o_a, s_new, conv_new = _mixer_common(
        x2d, mods, wts, b, t, t, zeros_buf, zeros_s, tm, t)
    pages = (b * t) // PAGE_ROWS
    cmp_rows = _compress(kv_c.reshape(pages, PAGE_ROWS, KV_ROW), jnp.arange(pages, dtype=jnp.int32), b,
                         wts['pe'], wts['w1bd'], wts['w2bd'])
    seq = lambda a: a.reshape(b, t, a.shape[-1])
    o_b = _nsa_prompt(seq(q_b), seq(small), cmp_rows, seq(kv_s), seq(kv_w))
    x1 = _outproj(o_a, o_b.reshape(b * t, -1), g_m, x2d, mods[2], wts['w_up_a'], wts['w_up_b'], wts['w_out'],
                  wts['ln_g0'], wts['ln_b0'], 512, t)
    x2 = _ffn(x1, mods[4], mods[3], mods[5], *ffn_args, wts['ln_g1'], wts['ln_b1'], min(1024, t), t, routed)
    win_new = _rows5(kv_w, b, t)[:, t - min(WINDOW, t):]
    return x2.reshape(b, t, D_MODEL), (s_new, conv_new, _rows5(kv_c, b, t), _rows5(kv_s, b, t), win_new)


def _sample_layer(x, mods, wts, ffn_args, routed, conv_buf, s0, pool_c, pool_s, win_buf, table):
    b = x.shape[0]
    x2d = x.reshape(b, D_MODEL)
    (g_m, q_b, kv_c, kv_s, kv_w, small), o_a, s_new, conv_new = _mixer_common(
        x2d, mods, wts, b, 1, GDN_CHUNK, conv_buf, s0, b, b)
    cmp_rows = _compress(pool_c, table, b, wts['pe'], wts['w1bd'], wts['w2bd'])
    win_new = jnp.concatenate([win_buf.reshape(b, -1, KV_ROW)[:, 1:], kv_w[:, None, :]], axis=1)
    q4 = q_b.reshape(b, NSA_GROUPS, NSA_REP, 1, NSA_DH)
    eye = jnp.eye(NSA_GROUPS, dtype=F32).reshape(NSA_GROUPS, 1, NSA_GROUPS, 1)
    qbd = (q4 * eye).reshape(b, NSA_HEADS, NSA_GROUPS * NSA_DH)
    o_b = _nsa_decode(qbd, small.reshape(b, 1, _SMALL_W), cmp_rows, win_new.astype(BF16),
                      kv_s.reshape(b, 1, KV_ROW), pool_s, table)
    x1 = _outproj(o_a, o_b, g_m, x2d, mods[2], wts['w_up_a'], wts['w_up_b'], wts['w_out'],
                  wts['ln_g0'], wts['ln_b0'], b, b)
    x2 = _ffn(x1, mods[4], mods[3], mods[5], *ffn_args, wts['ln_g1'], wts['ln_b1'], b, b, routed)
    return x2.reshape(b, 1, D_MODEL), (s_new, conv_new, _rows5(kv_c, b, 1), _rows5(kv_s, b, 1),
                                       _rows5(win_new, b, win_new.shape[1]))


def kernel(x_prompt, x_sample, state_delta, state_conv, cache_cmp_kv, cache_slc_kv, state_win_kv, page_table,
           c_prompt, c_sample, w_ada, b_ada, w_in, conv_w, a_log, dt_bias, norm_w_a, cmp_pos, cmp_w1, cmp_w2,
           w_up_a, w_up_b, w_out, ln_g, ln_b, ffn_w_gu, ffn_w_down, moe_router, moe_w_gu, moe_w_down):
    bp = x_prompt.shape[0]
    db = x_sample.shape[0]
    n_layers = w_in.shape[0]
    n_pool = cache_cmp_kv.shape[1]
    mod_all = _adaln(jnp.concatenate([c_prompt, c_sample], axis=0), w_ada, b_ada)
    table = page_table.reshape(-1)
    pool_c = cache_cmp_kv.reshape(n_layers * n_pool, PAGE_ROWS, KV_ROW)
    pool_s = cache_slc_kv.reshape(n_layers * n_pool, PAGE_ROWS, KV_ROW)
    xp, xs = x_prompt, x_sample
    st_p, st_s = [], []
    for l in range(n_layers):
        wts = _layer_weights(l, w_in, conv_w, a_log, dt_bias, norm_w_a, cmp_pos, cmp_w1, cmp_w2, w_up_a, w_up_b,
                             w_out, ln_g, ln_b)
        routed = l % 2 == 1
        if routed:
            router = jnp.pad(moe_router[l // 2], ((0, 0), (0, LANES - N_EXP)))
            ffn_args = (router, moe_w_gu[l // 2].astype(BF16), moe_w_down[l // 2].astype(BF16))
        else:
            unused_router = jnp.zeros((D_MODEL, LANES), F32)
            ffn_args = (unused_router, ffn_w_gu[l // 2][None].astype(BF16), ffn_w_down[l // 2][None].astype(BF16))
        mod6 = mod_all[l].reshape(bp + db, 6, D_MODEL)
        mods_p = [mod6[:bp, i].reshape(bp, 1, D_MODEL) for i in range(6)]
        mods_s = [mod6[bp:, i].reshape(1, db, D_MODEL) for i in range(6)]
        xp, sp = _prompt_layer(xp, mods_p, wts, ffn_args, routed)
        xs, ss = _sample_layer(xs, mods_s, wts, ffn_args, routed, state_conv[l], state_delta[l],
                               pool_c, pool_s, state_win_kv[l], table + l * n_pool)
        st_p.append(sp)
        st_s.append(ss)
    stack = lambda sts, i: jnp.stack([s[i] for s in sts])
    return (xp, xs) + tuple(stack(st_p, i) for i in range(5)) + tuple(stack(st_s, i) for i in range(5))
```

```python
import functools
import math

import jax
import jax.numpy as jnp
import numpy as np
from jax import lax
from jax.experimental import pallas as pl
from jax.experimental.pallas import tpu as pltpu

F32 = jnp.float32
BF16 = jnp.bfloat16
HIGHEST = lax.Precision.HIGHEST

D_MODEL = 1024
N_LAYERS = 2
PAGE_ROWS = 128
GDN_HEADS = 4
GDN_DK = 128
GDN_DV = 128
CONV_TAPS = 4
GDN_CHUNK = 64
C_QKV = 2 * GDN_HEADS * GDN_DK + GDN_HEADS * GDN_DV
NSA_HEADS = 8
NSA_GROUPS = 2
NSA_REP = NSA_HEADS // NSA_GROUPS
NSA_DH = 64
KV_ROW = 2 * NSA_GROUPS * NSA_DH
CMP_LEN = 32
CMP_STRIDE = 16
CMP_HID = 256
SLC_BLK = 64
N_SEL = 16
WINDOW = 512
Q_BLK = 128
D_FF = 2816
N_EXP = 8
ALPHA = (2 * N_LAYERS) ** 0.25
EPS = 1e-5
NEG = -1e30

LANES = 128
SUBLANES = 8
VMEM_LIMIT = 56 * 1024 * 1024

_SMALL_W = LANES
IN_OUT_WIDTHS = (C_QKV, GDN_HEADS * GDN_DV, 2 * D_MODEL, NSA_HEADS * NSA_DH, KV_ROW, KV_ROW, KV_ROW, _SMALL_W)


def _cparams(sem):
    return pltpu.CompilerParams(dimension_semantics=sem, vmem_limit_bytes=VMEM_LIMIT)


def _silu(x):
    return x * jax.nn.sigmoid(x)


def _ln_rows(x):
    mu = jnp.mean(x, axis=-1, keepdims=True)
    xc = x - mu
    var = jnp.mean(xc * xc, axis=-1, keepdims=True)
    return xc * lax.rsqrt(var + EPS)


def _dot(a, b, precision=None):
    return jnp.dot(a, b, preferred_element_type=F32, precision=precision)


def _dot_nt(a, b, precision=None):
    return lax.dot_general(a, b, (((1,), (1,)), ((), ())), preferred_element_type=F32, precision=precision)


def _dot_tn(a, b, precision=None):
    return lax.dot_general(a, b, (((0,), (0,)), ((), ())), preferred_element_type=F32, precision=precision)


def _masked_softmax(s, mask, axis=-1):
    s = jnp.where(mask, s, NEG)
    m = jnp.max(s, axis=axis, keepdims=True)
    e = jnp.where(mask, jnp.exp(s - m), 0.0)
    return e * (1.0 / jnp.maximum(jnp.sum(e, axis=axis, keepdims=True), 1e-30))


def _select_top(imp, lane_idx, n_lanes, axis=-1):
    sel = jnp.zeros(imp.shape, F32)
    lane_idx = lane_idx.astype(F32)
    for _ in range(N_SEL):
        mx = jnp.max(imp, axis=axis, keepdims=True)
        idx = jnp.min(jnp.where(imp == mx, lane_idx, float(n_lanes)), axis=axis, keepdims=True)
        hit = lane_idx == idx
        sel = jnp.where(hit, 1.0, sel)
        imp = jnp.where(hit, -3e38, imp)
    return sel


def _adaln_body(c_ref, w_ref, b_ref, o_ref):
    s = _silu(c_ref[...]).astype(BF16)
    o_ref[...] = _dot(s, w_ref[...].astype(BF16)) + b_ref[...]


def _adaln(c_all, w_ada, b_ada):
    n_l, d, n = w_ada.shape
    r = c_all.shape[0]
    tn = 1536
    return pl.pallas_call(
        _adaln_body,
        grid=(n_l, n // tn),
        in_specs=[pl.BlockSpec((r, d), lambda l, j: (0, 0)),
                  pl.BlockSpec((None, d, tn), lambda l, j: (l, 0, j)),
                  pl.BlockSpec((None, 1, tn), lambda l, j: (l, 0, j))],
        out_specs=pl.BlockSpec((None, r, tn), lambda l, j: (l, 0, j)),
        out_shape=jax.ShapeDtypeStruct((n_l, r, n), F32),
        compiler_params=_cparams(("parallel", "parallel")),
        name="adaln_mod",
    )(c_all, w_ada, b_ada.reshape(n_l, 1, n))


def _inproj_body(x_ref, sc_ref, sh_ref, w_ref, *o_refs):
    h = (_ln_rows(x_ref[...]) * (1.0 + sc_ref[...]) + sh_ref[...]).astype(BF16)
    off = 0
    for o_ref, n in zip(o_refs, IN_OUT_WIDTHS):
        o_ref[...] = _dot(h, w_ref[:, off:off + n])
        off += n


def _mod_spec(mod, tm, rows_per_mod):
    if mod.shape[1] == 1:
        return pl.BlockSpec((None, 1, D_MODEL), lambda i, *_: ((i * tm) // rows_per_mod, 0, 0))
    return pl.BlockSpec((None, tm, D_MODEL), lambda i, *_: (0, i, 0))


def _inproj(x2d, sc, sh, w_perm, tm, rows_per_mod):
    m = x2d.shape[0]
    n_tot = w_perm.shape[1]
    return pl.pallas_call(
        _inproj_body,
        grid=(m // tm,),
        in_specs=[pl.BlockSpec((tm, D_MODEL), lambda i: (i, 0)),
                  _mod_spec(sc, tm, rows_per_mod), _mod_spec(sh, tm, rows_per_mod),
                  pl.BlockSpec((D_MODEL, n_tot), lambda i: (0, 0))],
        out_specs=[pl.BlockSpec((tm, n), lambda i: (i, 0)) for n in IN_OUT_WIDTHS],
        out_shape=[jax.ShapeDtypeStruct((m, n), F32) for n in IN_OUT_WIDTHS],
        compiler_params=_cparams(("parallel",)),
        name="ln_inproj",
    )(x2d, sc, sh, w_perm)


def _prep_body(qkv_ref, buf_ref, cw_ref, sm_ref, alog_ref, dt_ref, q_ref, k_ref, v_ref, bg_ref, xp_sc,
               *, tb, t_valid):
    i = pl.program_id(1)

    @pl.when(i == 0)
    def _():
        xp_sc[0:SUBLANES, :] = buf_ref[...]

    x = qkv_ref[...]
    xp_sc[SUBLANES:SUBLANES + tb, :] = x
    y = cw_ref[3:4, :] * x
    for j in range(CONV_TAPS - 1):
        lo = SUBLANES - (CONV_TAPS - 1) + j
        y = y + cw_ref[j:j + 1, :] * xp_sc[lo:lo + tb, :]
    xp_sc[0:SUBLANES, :] = xp_sc[tb:tb + SUBLANES, :]
    a = _silu(y)
    rows = i * tb + lax.broadcasted_iota(jnp.int32, (tb, 1), 0)
    live = rows < t_valid
    hk = GDN_HEADS * GDN_DK
    for h in range(GDN_HEADS):
        qh = a[:, h * GDN_DK:(h + 1) * GDN_DK]
        kh = a[:, hk + h * GDN_DK:hk + (h + 1) * GDN_DK]
        qn = qh * lax.rsqrt(jnp.sum(qh * qh, axis=-1, keepdims=True) + 1e-6) * (GDN_DK ** -0.5)
        kn = kh * lax.rsqrt(jnp.sum(kh * kh, axis=-1, keepdims=True) + 1e-6)
        q_ref[:, h * GDN_DK:(h + 1) * GDN_DK] = jnp.where(live, qn, 0.0)
        k_ref[:, h * GDN_DK:(h + 1) * GDN_DK] = jnp.where(live, kn, 0.0)
    v_ref[...] = jnp.where(live, a[:, 2 * hk:], 0.0)
    sm = sm_ref[...]
    beta = jax.nn.sigmoid(sm)
    z = sm + dt_ref[...]
    softplus = jnp.maximum(z, 0.0) + jnp.log(1.0 + jnp.exp(-jnp.abs(z)))
    g = -jnp.exp(alog_ref[...]) * softplus
    lane = lax.broadcasted_iota(jnp.int32, sm.shape, 1)
    bg_ref[...] = jnp.where(live, jnp.where(lane < GDN_HEADS, beta, g), 0.0)


def _gdn_prep(qkv, buf8, conv_w, small, alog_vec, dt_vec, t_valid):
    b, t, _ = qkv.shape
    tb = min(t, 512)
    hd = GDN_HEADS * GDN_DK
    row = lambda bi, i: (bi, i, 0)
    fixed = lambda bi, i: (0, 0)
    return pl.pallas_call(
        functools.partial(_prep_body, tb=tb, t_valid=t_valid),
        grid=(b, t // tb),
        in_specs=[pl.BlockSpec((None, tb, C_QKV), row),
                  pl.BlockSpec((None, SUBLANES, C_QKV), lambda bi, i: (bi, 0, 0)),
                  pl.BlockSpec((CONV_TAPS, C_QKV), fixed),
                  pl.BlockSpec((None, tb, _SMALL_W), row),
                  pl.BlockSpec((1, _SMALL_W), fixed), pl.BlockSpec((1, _SMALL_W), fixed)],
        out_specs=[pl.BlockSpec((None, tb, hd), row), pl.BlockSpec((None, tb, hd), row),
                   pl.BlockSpec((None, tb, hd), row), pl.BlockSpec((None, tb, _SMALL_W), row)],
        out_shape=[jax.ShapeDtypeStruct((b, t, hd), F32)] * 3 + [jax.ShapeDtypeStruct((b, t, _SMALL_W), F32)],
        scratch_shapes=[pltpu.VMEM((tb + SUBLANES, C_QKV), F32)],
        compiler_params=_cparams(("parallel", "arbitrary")),
        name="gdn_prep",
    )(qkv, buf8, conv_w, small, alog_vec, dt_vec)


def _bmm(a, b, precision=HIGHEST):
    return lax.dot_general(a, b, (((2,), (1,)), ((0,), (0,))), preferred_element_type=F32, precision=precision)


def _bmm_nt(a, b, precision=HIGHEST):
    return lax.dot_general(a, b, (((2,), (2,)), ((0,), (0,))), preferred_element_type=F32, precision=precision)


def _gdn_ut_body(q_ref, k_ref, v_ref, bg_ref, u_ref, w_ref, qd_ref, kd_ref, qk_ref, gl_ref, *, n_chunks):
    c_len = GDN_CHUNK
    nc = n_chunks
    tb = nc * c_len
    row = lax.broadcasted_iota(jnp.int32, (nc, c_len, c_len), 1)
    col = lax.broadcasted_iota(jnp.int32, (nc, c_len, c_len), 2)
    incl = row >= col
    strict = row > col
    incl_f = incl.astype(F32)
    strict_f = strict.astype(F32)
    eye = (row == col).astype(F32)
    bg = bg_ref[...]
    lane = lax.broadcasted_iota(jnp.int32, (tb, _SMALL_W), 1)
    gl_all = jnp.zeros((tb, _SMALL_W), F32)
    split = lambda x: x.reshape(nc, c_len, x.shape[-1])
    for h in range(GDN_HEADS):
        sl = slice(h * GDN_DK, (h + 1) * GDN_DK)
        q = split(q_ref[:, sl])
        k = split(k_ref[:, sl])
        v = split(v_ref[:, sl])
        beta = split(bg[:, h:h + 1])
        g = split(bg[:, GDN_HEADS + h:GDN_HEADS + h + 1])
        dlog = _bmm(incl_f, g * strict_f)
        gc = dlog[:, :, 0:1] + g[:, 0:1, :]
        decay = jnp.where(incl, jnp.exp(dlog), 0.0)
        egc = jnp.exp(gc)
        g_last = gc[:, c_len - 1:c_len, :]
        kb = k * beta
        a = jnp.where(strict, _bmm_nt(kb, k) * decay, 0.0)
        p = -a
        tinv = eye + p
        for _ in range(5):
            p = _bmm(p, p)
            tinv = tinv + _bmm(tinv, p)
        u_ref[:, sl] = _bmm(tinv, v * beta).reshape(tb, GDN_DV)
        w_ref[:, sl] = _bmm(tinv, kb * egc).reshape(tb, GDN_DK)
        qk = jnp.where(incl, _bmm_nt(q, k) * decay, 0.0)
        qk_ref[:, h * c_len:(h + 1) * c_len] = qk.reshape(tb, c_len)
        qd_ref[:, sl] = (q * egc).reshape(tb, GDN_DK)
        kd_ref[:, sl] = (k * jnp.exp(g_last - gc)).reshape(tb, GDN_DK)
        gl = jnp.broadcast_to(jnp.exp(g_last), (nc, c_len, 1)).reshape(tb, 1)
        gl_all = jnp.where(lane == h, gl, gl_all)
    gl_ref[...] = gl_all


def _gdn_scan_body(u_ref, w_ref, qd_ref, kd_ref, qk_ref, gl_ref, z_ref, s0_ref, nw_ref, o_ref, sfin_ref, s_sc,
                   *, n_chunks, n_seq):
    i = pl.program_id(1)
    c_len = GDN_CHUNK

    @pl.when(i == 0)
    def _():
        s_sc[...] = s0_ref[...]

    nw = nw_ref[...]

    def chunk(c, carry):
        r0 = pl.multiple_of(c * c_len, c_len)
        rows = pl.ds(r0, c_len)
        gl_row = gl_ref[:, pl.ds(r0, 1), :]
        for h in range(GDN_HEADS):
            sl = slice(h * GDN_DK, (h + 1) * GDN_DK)
            s = s_sc[:, h]
            v_new = u_ref[:, rows, sl] - _bmm(w_ref[:, rows, sl], s)
            o = _bmm(qd_ref[:, rows, sl], s) + _bmm(qk_ref[:, rows, h * c_len:(h + 1) * c_len], v_new)
            kd = kd_ref[:, rows, sl]
            upd = jnp.stack([_dot_tn(kd[b], v_new[b], HIGHEST) for b in range(n_seq)])
            s_sc[:, h] = s * gl_row[:, :, h:h + 1] + upd
            z = z_ref[:, rows, sl]
            on = o * lax.rsqrt(jnp.mean(o * o, axis=-1, keepdims=True) + EPS) * nw
            o_ref[:, rows, sl] = on * _silu(z)
        return carry

    lax.fori_loop(0, n_chunks, chunk, 0)

    @pl.when(i == pl.num_programs(1) - 1)
    def _():
        sfin_ref[...] = s_sc[...]


_GDN_SEQ_PER_STEP = 8


def _gdn(q, k, v, bg, z, s0, norm_w):
    b, t, hd = q.shape
    tb = min(t, 512)
    nc = tb // GDN_CHUNK
    row = lambda bi, i: (bi, i, 0)
    wide = lambda n: pl.BlockSpec((None, tb, n), row)
    qkw = GDN_HEADS * GDN_CHUNK
    u, w, qd, kd, qk, gl = pl.pallas_call(
        functools.partial(_gdn_ut_body, n_chunks=nc),
        grid=(b, t // tb),
        in_specs=[wide(hd)] * 3 + [wide(_SMALL_W)],
        out_specs=[wide(hd)] * 4 + [wide(qkw), wide(_SMALL_W)],
        out_shape=[jax.ShapeDtypeStruct((b, t, hd), F32)] * 4
        + [jax.ShapeDtypeStruct((b, t, qkw), F32), jax.ShapeDtypeStruct((b, t, _SMALL_W), F32)],
        compiler_params=_cparams(("parallel", "parallel")),
        name="gdn_ut",
    )(q, k, v, bg)
    nb = min(b, _GDN_SEQ_PER_STEP)
    grp = lambda gi, i: (gi, i, 0)
    st = lambda gi, i: (gi, 0, 0, 0)
    seqs = lambda n: pl.BlockSpec((nb, tb, n), grp)
    return pl.pallas_call(
        functools.partial(_gdn_scan_body, n_chunks=nc, n_seq=nb),
        grid=(b // nb, t // tb),
        in_specs=[seqs(hd)] * 4 + [seqs(qkw), seqs(_SMALL_W), seqs(hd),
                                   pl.BlockSpec((nb, GDN_HEADS, GDN_DK, GDN_DV), st),
                                   pl.BlockSpec((1, GDN_DV), lambda gi, i: (0, 0))],
        out_specs=[seqs(hd), pl.BlockSpec((nb, GDN_HEADS, GDN_DK, GDN_DV), st)],
        out_shape=[jax.ShapeDtypeStruct((b, t, hd), F32),
                   jax.ShapeDtypeStruct((b, GDN_HEADS, GDN_DK, GDN_DV), F32)],
        scratch_shapes=[pltpu.VMEM((nb, GDN_HEADS, GDN_DK, GDN_DV), F32)],
        compiler_params=_cparams(("parallel", "arbitrary")),
        name="gdn_scan",
    )(u, w, qd, kd, qk, gl, z, s0, norm_w)


_SEG_PER_PAGE = PAGE_ROWS // CMP_STRIDE
_CMP_PAGES_PER_STEP = 8


def _compress_body(pt_ref, *refs, pages_per_seq, feature_major):
    npg = _CMP_PAGES_PER_STEP
    page_refs = refs[:npg]
    pe_ref, w1_ref, w2_ref, o_ref, carry_sc, rows_sc = refs[npg:]
    step = pl.program_id(0)
    first = (step % (pages_per_seq // npg)) == 0
    for p, r in enumerate(page_refs):
        for kv in range(2):
            half = slice(kv * LANES, (kv + 1) * LANES)
            rows_sc[kv, p * PAGE_ROWS:(p + 1) * PAGE_ROWS, :] = r[half, :].T if feature_major else r[:, half]
    n_rows = npg * _SEG_PER_PAGE
    n_m = CMP_LEN // CMP_STRIDE
    gh = NSA_GROUPS * CMP_HID
    acc = [[jnp.zeros((n_rows, gh), F32) for _ in range(2)] for _ in range(n_m)]
    for s in range(CMP_STRIDE):
        for kv in range(2):
            x = rows_sc[kv, pl.ds(s, n_rows, stride=CMP_STRIDE), :]
            for m in range(n_m):
                xm = (x + pe_ref[m, s:s + 1, kv * LANES:(kv + 1) * LANES]).astype(BF16)
                acc[m][kv] = acc[m][kv] + _dot(xm, w1_ref[m, s, kv])
    part = [jnp.concatenate(a, axis=1) for a in acc]
    prev = jnp.where(first, 0.0, carry_sc[...])
    rows = lax.broadcasted_iota(jnp.int32, (n_rows, 1), 0)
    shifted = jnp.where(rows == 0, prev, pltpu.roll(part[0], 1, 0))
    carry_sc[...] = part[0][n_rows - 1:n_rows, :]
    hid = _silu(shifted + part[1]).astype(BF16)
    for kv in range(2):
        o_ref[:, kv * LANES:(kv + 1) * LANES] = _dot(hid[:, kv * gh:(kv + 1) * gh], w2_ref[kv]).astype(BF16)


def _compress(pool, table, n_seq, pe, w1bd, w2bd, feature_major):
    pages_per_seq = table.shape[0] // n_seq
    npg = _CMP_PAGES_PER_STEP
    n_steps = table.shape[0] // npg
    page_spec = lambda p: pl.BlockSpec((None,) + pool.shape[1:], lambda i, pt: (pt[i * npg + p], 0, 0))
    fixed = lambda nd: (lambda i, pt: (0,) * nd)
    out = pl.pallas_call(
        functools.partial(_compress_body, pages_per_seq=pages_per_seq, feature_major=feature_major),
        grid_spec=pltpu.PrefetchScalarGridSpec(
            num_scalar_prefetch=1,
            grid=(n_steps,),
            in_specs=[page_spec(p) for p in range(npg)]
            + [pl.BlockSpec(pe.shape, fixed(3)), pl.BlockSpec(w1bd.shape, fixed(5)),
               pl.BlockSpec(w2bd.shape, fixed(3))],
            out_specs=pl.BlockSpec((npg * _SEG_PER_PAGE, KV_ROW), lambda i, pt: (i, 0)),
            scratch_shapes=[pltpu.VMEM((1, 2 * NSA_GROUPS * CMP_HID), F32),
                            pltpu.VMEM((2, npg * PAGE_ROWS, LANES), F32)],
        ),
        out_shape=jax.ShapeDtypeStruct((n_steps * npg * _SEG_PER_PAGE, KV_ROW), BF16),
        compiler_params=_cparams(("arbitrary",)),
        name="nsa_compress",
    )(table, *([pool] * npg), pe, w1bd, w2bd)
    return out.reshape(n_seq, pages_per_seq * _SEG_PER_PAGE, KV_ROW)


def _compress_weights(cmp_pos, cmp_w1, cmp_w2):
    n_m = CMP_LEN // CMP_STRIDE
    pe = cmp_pos.reshape(2, n_m, CMP_STRIDE, 1, NSA_DH)
    pe = jnp.broadcast_to(pe, (2, n_m, CMP_STRIDE, NSA_GROUPS, NSA_DH))
    pe = jnp.transpose(pe, (1, 2, 0, 3, 4)).reshape(n_m, CMP_STRIDE, KV_ROW)
    w1 = cmp_w1.reshape(2, n_m, CMP_STRIDE, NSA_DH, CMP_HID)
    w1 = jnp.transpose(w1, (1, 2, 0, 3, 4))
    eye = jnp.eye(NSA_GROUPS, dtype=F32)
    w1bd = jnp.einsum('mskdh,gG->mskgdGh', w1, eye).reshape(
        n_m, CMP_STRIDE, 2, NSA_GROUPS * NSA_DH, NSA_GROUPS * CMP_HID).astype(BF16)
    w2bd = jnp.einsum('khd,gG->kghGd', cmp_w2, eye).reshape(
        2, NSA_GROUPS * CMP_HID, NSA_GROUPS * NSA_DH).astype(BF16)
    return pe, w1bd, w2bd


def _overlap_matrix(n_rows, n_blk_lanes):
    ci = (np.arange(n_rows)[:, None] - 1) * CMP_STRIDE
    sj = np.arange(n_blk_lanes)[None, :] * SLC_BLK
    ov = (ci < sj + SLC_BLK) & (ci + CMP_LEN > sj) & (ci >= 0)
    return jnp.asarray(ov.astype(np.float32))


_SEL_KEY_BLOCK = 1024


def _nsa_prompt_body(qt_ref, smt_ref, ck_ref, cvt_ref, sk_ref, svt_ref, wk_ref, wvt_ref, ovt_ref, o_ref,
                     m_sc, l_sc, acc_sc):
    iq = pl.program_id(1)
    qb = Q_BLK
    t0 = iq * qb
    n_cmp_rows = ck_ref.shape[0]
    n_blk = ovt_ref.shape[0]
    w_rows = WINDOW + qb
    gd = NSA_GROUPS * NSA_DH
    t_row = t0 + lax.broadcasted_iota(jnp.int32, (1, qb), 1)
    gates = jax.nn.sigmoid(smt_ref[...])
    ci = lax.broadcasted_iota(jnp.int32, (n_cmp_rows, 1), 0)
    cmp_mask = (ci >= 1) & (CMP_STRIDE * ci + (CMP_LEN - CMP_STRIDE - 1) <= t_row)
    blk = lax.broadcasted_iota(jnp.int32, (n_blk, 1), 0)
    forced = (blk == t_row // SLC_BLK) | (blk == 0)
    valid = blk * SLC_BLK <= t_row
    start = pl.multiple_of(jnp.maximum(t0 - WINDOW, 0), qb)
    wpos = start + lax.broadcasted_iota(jnp.int32, (w_rows, 1), 0)
    dlt = t_row - wpos
    win_mask = (dlt >= 0) & (dlt < WINDOW)
    kblk = min(_SEL_KEY_BLOCK, sk_ref.shape[0])
    e_key = lax.broadcasted_iota(jnp.int32, (kblk, n_blk), 0) // SLC_BLK
    e_blk = lax.broadcasted_iota(jnp.int32, (kblk, n_blk), 1)
    key_col = lax.broadcasted_iota(jnp.int32, (kblk, 1), 0)
    feat = lax.broadcasted_iota(jnp.int32, (gd, 1), 0)
    scale = NSA_DH ** -0.5
    heads = lambda x: [x[:, r * qb:(r + 1) * qb] for r in range(NSA_REP)]

    for g in range(NSA_GROUPS):
        vrows = slice(g * NSA_DH, (g + 1) * NSA_DH)
        qt = jnp.concatenate([qt_ref[(g * NSA_REP + r) * NSA_DH:(g * NSA_REP + r + 1) * NSA_DH, :]
                              for r in range(NSA_REP)], axis=1) * scale
        qt = jnp.concatenate([qt, jnp.zeros_like(qt)] if g == 0 else [jnp.zeros_like(qt), qt], axis=0)
        qt = qt.astype(BF16)
        s_c = _dot(ck_ref[...], qt)
        p_c = [_masked_softmax(s, cmp_mask, axis=0) for s in heads(s_c)]
        o_c = _dot(cvt_ref[vrows, :], jnp.concatenate(p_c, axis=1).astype(BF16))
        imp = _dot(ovt_ref[...], p_c[0] + p_c[1] + p_c[2] + p_c[3], HIGHEST)
        imp = jnp.where(forced, 1e9, jnp.where(valid, imp, -1.0))
        sel = _select_top(imp, blk, n_blk, axis=0).astype(BF16)

        m_sc[...] = jnp.full(m_sc.shape, NEG, F32)
        l_sc[...] = jnp.zeros(l_sc.shape, F32)
        acc_sc[...] = jnp.zeros(acc_sc.shape, F32)

        def kv_step(kb, carry):
            r0 = pl.multiple_of(kb * kblk, kblk)
            s_s = _dot(sk_ref[pl.ds(r0, kblk), :], qt)
            expand = (e_blk == (kblk // SLC_BLK) * kb + e_key).astype(BF16)
            msk = (_dot(expand, sel) > 0.5) & (r0 + key_col <= t_row)
            m_old = m_sc[...]
            l_old = l_sc[...]
            m_new, l_new, es = [], [], []
            for r, s in enumerate(heads(s_s)):
                s = jnp.where(msk, s, NEG)
                m_r = jnp.maximum(m_old[:, r * qb:(r + 1) * qb], jnp.max(s, axis=0, keepdims=True))
                e = jnp.exp(s - m_r)
                m_new.append(m_r)
                es.append(e.astype(BF16))
                l_new.append(jnp.sum(e, axis=0, keepdims=True))
            m_new = jnp.concatenate(m_new, axis=1)
            corr = jnp.exp(m_old - m_new)
            m_sc[...] = m_new
            l_sc[...] = corr * l_old + jnp.concatenate(l_new, axis=1)
            pv = _dot(svt_ref[vrows, pl.ds(r0, kblk)], jnp.concatenate(es, axis=1))
            acc_sc[...] = corr * acc_sc[...] + pv
            return carry

        lax.fori_loop(0, (t0 + qb + kblk - 1) // kblk, kv_step, 0)
        o_s = acc_sc[...] * (1.0 / jnp.maximum(l_sc[...], 1e-30))
        s_w = _dot(wk_ref[pl.ds(start, w_rows), :], qt)
        p_w = jnp.concatenate([_masked_softmax(s, win_mask, axis=0).astype(BF16) for s in heads(s_w)], axis=1)
        o_w = _dot(wvt_ref[vrows, pl.ds(start, w_rows)], p_w)
        for r in range(NSA_REP):
            h = g * NSA_REP + r
            row0 = 2 * GDN_HEADS + 3 * h
            cs = slice(r * qb, (r + 1) * qb)
            o_ref[h * NSA_DH:(h + 1) * NSA_DH, :] = (
                gates[row0:row0 + 1, :] * o_c[:, cs] + gates[row0 + 1:row0 + 2, :] * o_s[:, cs]
                + gates[row0 + 2:row0 + 3, :] * o_w[:, cs])


def _nsa_prompt(q_b, small, cmp_rows, kv_s, kv_w):
    b, t, _ = q_b.shape
    gd = NSA_GROUPS * NSA_DH
    n_blk = max(t // SLC_BLK, LANES)
    ovt = _overlap_matrix(cmp_rows.shape[1], n_blk).T
    tr = lambda a: jnp.swapaxes(a, 1, 2)
    keys = lambda kv: kv[:, :, :gd].astype(BF16)
    vals_t = lambda kv: tr(kv[:, :, gd:]).astype(BF16)
    whole = lambda bi, i: (bi, 0, 0)
    tok = lambda bi, i: (bi, 0, i)
    rq = NSA_REP * Q_BLK
    out_t = pl.pallas_call(
        _nsa_prompt_body,
        grid=(b, t // Q_BLK),
        in_specs=[pl.BlockSpec((None, NSA_HEADS * NSA_DH, Q_BLK), tok),
                  pl.BlockSpec((None, _SMALL_W, Q_BLK), tok),
                  pl.BlockSpec((None, cmp_rows.shape[1], gd), whole),
                  pl.BlockSpec((None, gd, cmp_rows.shape[1]), whole),
                  pl.BlockSpec((None, t, gd), whole), pl.BlockSpec((None, gd, t), whole),
                  pl.BlockSpec((None, t, gd), whole), pl.BlockSpec((None, gd, t), whole),
                  pl.BlockSpec(ovt.shape, lambda bi, i: (0, 0))],
        out_specs=pl.BlockSpec((None, NSA_HEADS * NSA_DH, Q_BLK), tok),
        out_shape=jax.ShapeDtypeStruct((b, NSA_HEADS * NSA_DH, t), F32),
        scratch_shapes=[pltpu.VMEM((1, rq), F32), pltpu.VMEM((1, rq), F32), pltpu.VMEM((NSA_DH, rq), F32)],
        compiler_params=_cparams(("parallel", "arbitrary")),
        name="nsa_prompt",
    )(tr(q_b), tr(small), cmp_rows[:, :, :gd], tr(cmp_rows[:, :, gd:]), keys(kv_s), vals_t(kv_s),
      keys(kv_w), vals_t(kv_w), ovt)
    return tr(out_t)


_DEC_PAGES_PER_STEP = 8


def _nsa_decode_body(pt_ref, qbd_ref, sm_ref, cmp_ref, wint_ref, neww_ref, news_ref, ov_ref, *refs, n_past_blk):
    npg = _DEC_PAGES_PER_STEP
    page_refs = refs[:npg]
    o_ref, sel_sc, m_sc, l_sc, acc_sc, oc_sc, ow_sc = refs[npg:]
    j = pl.program_id(1)
    n_blk = ov_ref.shape[1]
    gd = NSA_GROUPS * NSA_DH
    qf = qbd_ref[...] * (NSA_DH ** -0.5)
    q = qf.astype(BF16)
    head = lax.broadcasted_iota(jnp.int32, (NSA_HEADS, gd), 0)
    lane = lax.broadcasted_iota(jnp.int32, (NSA_HEADS, gd), 1)
    own = (lane // NSA_DH == head // NSA_REP).astype(F32)

    @pl.when(j == 0)
    def _():
        n_rows = cmp_ref.shape[0]
        ci = lax.broadcasted_iota(jnp.int32, (1, n_rows), 1)
        p_c = _masked_softmax(_dot_nt(q, cmp_ref[:, 0:gd]), ci >= 1)
        oc_sc[...] = _dot(p_c.astype(BF16), cmp_ref[:, gd:2 * gd]) * own
        hr = lax.broadcasted_iota(jnp.int32, (NSA_HEADS, NSA_HEADS), 0) // NSA_REP
        hc = lax.broadcasted_iota(jnp.int32, (NSA_HEADS, NSA_HEADS), 1) // NSA_REP
        p_grp = _dot((hr == hc).astype(F32), p_c, HIGHEST)
        imp = _dot(p_grp, ov_ref[...], HIGHEST)
        blk = lax.broadcasted_iota(jnp.int32, (1, n_blk), 1)
        imp = jnp.where((blk == n_past_blk) | (blk == 0), 1e9, jnp.where(blk < n_past_blk, imp, -2e38))
        sel_sc[...] = _select_top(imp, blk, n_blk)
        n_win = wint_ref.shape[1]
        wi = lax.broadcasted_iota(jnp.int32, (1, n_win), 1)
        neww = neww_ref[...]
        s_w = jnp.where(wi >= 1, _dot(q, wint_ref[0:gd, :].astype(BF16)), NEG)
        s_cur = jnp.sum(qf * neww[:, 0:gd], axis=-1, keepdims=True)
        m_w = jnp.maximum(jnp.max(s_w, axis=-1, keepdims=True), s_cur)
        e_w = jnp.where(wi >= 1, jnp.exp(s_w - m_w), 0.0)
        e_cur = jnp.exp(s_cur - m_w)
        den = jnp.maximum(jnp.sum(e_w, axis=-1, keepdims=True) + e_cur, 1e-30)
        pv_w = _dot_nt(e_w.astype(BF16), wint_ref[gd:2 * gd, :].astype(BF16)) + e_cur * neww[:, gd:2 * gd]
        ow_sc[...] = pv_w * (1.0 / den) * own
        m_sc[...] = jnp.full(m_sc.shape, NEG, F32)
        l_sc[...] = jnp.zeros(l_sc.shape, F32)
        acc_sc[...] = jnp.zeros(acc_sc.shape, F32)

    n_keys = npg * PAGE_ROWS
    kt = jnp.concatenate([r[0:gd, :] for r in page_refs], axis=1).astype(BF16)
    vt = jnp.concatenate([r[gd:2 * gd, :] for r in page_refs], axis=1).astype(BF16)
    e_row = lax.broadcasted_iota(jnp.int32, (n_blk, n_keys), 0)
    e_col = lax.broadcasted_iota(jnp.int32, (n_blk, n_keys), 1) // SLC_BLK
    first_blk = (PAGE_ROWS // SLC_BLK) * (j * npg)
    msk = _dot(sel_sc[...].astype(BF16), (e_row == first_blk + e_col).astype(BF16)) > 0.5
    sc = jnp.where(msk, _dot(q, kt), NEG)
    m_old = m_sc[...]
    m_new = jnp.maximum(m_old, jnp.max(sc, axis=-1, keepdims=True))
    corr = jnp.exp(m_old - m_new)
    e = jnp.where(msk, jnp.exp(sc - m_new), 0.0)
    l_sc[...] = corr * l_sc[...] + jnp.sum(e, axis=-1, keepdims=True)
    acc_sc[...] = corr * acc_sc[...] + _dot_nt(e.astype(BF16), vt)
    m_sc[...] = m_new

    @pl.when(j == pl.num_programs(1) - 1)
    def _():
        new = news_ref[...]
        s_new = jnp.sum(qf * new[:, 0:gd], axis=-1, keepdims=True)
        m_old = m_sc[...]
        m_new = jnp.maximum(m_old, s_new)
        corr = jnp.exp(m_old - m_new)
        e = jnp.exp(s_new - m_new)
        l = corr * l_sc[...] + e
        acc = corr * acc_sc[...] + e * new[:, gd:2 * gd]
        o_s = acc / jnp.maximum(l, 1e-30) * own
        gates = jax.nn.sigmoid(sm_ref[...])
        glane = lax.broadcasted_iota(jnp.int32, (NSA_HEADS, _SMALL_W), 1)
        ghead = lax.broadcasted_iota(jnp.int32, (NSA_HEADS, _SMALL_W), 0)
        gate = [jnp.sum(jnp.where(glane == 2 * GDN_HEADS + 3 * ghead + br, gates, 0.0), axis=-1, keepdims=True)
                for br in range(3)]
        o = gate[0] * oc_sc[...] + gate[1] * o_s + gate[2] * ow_sc[...]
        o_ref[...] = o[:, 0:NSA_DH] + o[:, NSA_DH:gd]


def _nsa_decode(qbd, small, cmp_rows, win_t, new_w, new_s, pool, table):
    b = qbd.shape[0]
    npg = _DEC_PAGES_PER_STEP
    pages_per_seq = table.shape[0] // b
    n_past_blk = pages_per_seq * (PAGE_ROWS // SLC_BLK)
    n_blk = -(-(n_past_blk + 1) // LANES) * LANES
    ov = _overlap_matrix(cmp_rows.shape[1], n_blk)
    per_seq = lambda bi, j, pt: (bi, 0, 0)
    page_spec = lambda p: pl.BlockSpec(
        (None, KV_ROW, PAGE_ROWS), lambda bi, j, pt: (pt[bi * pages_per_seq + j * npg + p], 0, 0))
    gd = NSA_GROUPS * NSA_DH
    out = pl.pallas_call(
        functools.partial(_nsa_decode_body, n_past_blk=n_past_blk),
        grid_spec=pltpu.PrefetchScalarGridSpec(
            num_scalar_prefetch=1,
            grid=(b, pages_per_seq // npg),
            in_specs=[pl.BlockSpec((None, NSA_HEADS, gd), per_seq),
                      pl.BlockSpec((None, 1, _SMALL_W), per_seq),
                      pl.BlockSpec((None,) + cmp_rows.shape[1:], per_seq),
                      pl.BlockSpec((None,) + win_t.shape[1:], per_seq),
                      pl.BlockSpec((None, 1, KV_ROW), per_seq), pl.BlockSpec((None, 1, KV_ROW), per_seq),
                      pl.BlockSpec(ov.shape, lambda bi, j, pt: (0, 0))]
            + [page_spec(p) for p in range(npg)],
            out_specs=pl.BlockSpec((None, NSA_HEADS, NSA_DH), per_seq),
            scratch_shapes=[pltpu.VMEM((NSA_HEADS, n_blk), F32), pltpu.VMEM((NSA_HEADS, 1), F32),
                            pltpu.VMEM((NSA_HEADS, 1), F32), pltpu.VMEM((NSA_HEADS, gd), F32),
                            pltpu.VMEM((NSA_HEADS, gd), F32), pltpu.VMEM((NSA_HEADS, gd), F32)],
        ),
        out_shape=jax.ShapeDtypeStruct((b, NSA_HEADS, NSA_DH), F32),
        compiler_params=_cparams(("parallel", "arbitrary")),
        name="nsa_decode",
    )(table, qbd, small, cmp_rows, win_t, new_w, new_s, ov, *([pool] * npg))
    return out.reshape(b, NSA_HEADS * NSA_DH)


def _outproj_body(oa_ref, ob_ref, gm_ref, x_ref, gt_ref, wa_ref, wb_ref, wo_ref, lg_ref, lb_ref, o_ref):
    ya = _dot(oa_ref[...].astype(BF16), wa_ref[...])
    yb = _dot(ob_ref[...].astype(BF16), wb_ref[...])
    gm = jax.nn.sigmoid(gm_ref[...])
    u = (gm[:, :D_MODEL] * ya + gm[:, D_MODEL:] * yb).astype(BF16)
    y = _dot(u, wo_ref[...])
    xr = ALPHA * x_ref[...] + gt_ref[...] * y
    o_ref[...] = _ln_rows(xr) * lg_ref[...] + lb_ref[...]


def _outproj(o_a, o_b, g_m, x2d, gt, w_up_a, w_up_b, w_out, ln_g, ln_b, tm, rows_per_mod):
    m = x2d.shape[0]
    row = lambda n: pl.BlockSpec((tm, n), lambda i: (i, 0))
    fixed = lambda shape: pl.BlockSpec(shape, lambda i: (0, 0))
    return pl.pallas_call(
        _outproj_body,
        grid=(m // tm,),
        in_specs=[row(o_a.shape[1]), row(o_b.shape[1]), row(2 * D_MODEL), row(D_MODEL),
                  _mod_spec(gt, tm, rows_per_mod),
                  fixed(w_up_a.shape), fixed(w_up_b.shape), fixed(w_out.shape),
                  fixed((1, D_MODEL)), fixed((1, D_MODEL))],
        out_specs=row(D_MODEL),
        out_shape=jax.ShapeDtypeStruct((m, D_MODEL), F32),
        compiler_params=_cparams(("parallel",)),
        name="mixer_outproj",
    )(o_a, o_b, g_m, x2d, gt, w_up_a, w_up_b, w_out, ln_g, ln_b)


_FF_TILE = 256


def _ffn_body(x_ref, sc_ref, sh_ref, gt_ref, wg_ref, wu_ref, wd_ref, lg_ref, lb_ref, o_ref, h_sc, acc_sc):
    f = pl.program_id(1)

    @pl.when(f == 0)
    def _():
        h = _ln_rows(x_ref[...]) * (1.0 + sc_ref[...]) + sh_ref[...]
        h_sc[...] = h.astype(BF16)
        acc_sc[...] = jnp.zeros(acc_sc.shape, F32)

    hb = h_sc[...]
    hid = _silu(_dot(hb, wg_ref[...])) * _dot(hb, wu_ref[...])
    acc_sc[...] += _dot(hid.astype(BF16), wd_ref[...])

    @pl.when(f == pl.num_programs(1) - 1)
    def _():
        xr = ALPHA * x_ref[...] + gt_ref[...] * acc_sc[...]
        o_ref[...] = _ln_rows(xr) * lg_ref[...] + lb_ref[...]


def _ffn(x2d, sc, sh, gt, w_gu, w_down, ln_g, ln_b, tm, rows_per_mod):
    m = x2d.shape[0]
    tf = _FF_TILE
    n_f = D_FF // tf
    row = pl.BlockSpec((tm, D_MODEL), lambda i, f: (i, 0))
    fixed = lambda shape: pl.BlockSpec(shape, lambda i, f: (0, 0))
    mod = lambda a: _mod_spec(a, tm, rows_per_mod)
    return pl.pallas_call(
        _ffn_body,
        grid=(m // tm, n_f),
        in_specs=[row, mod(sc), mod(sh), mod(gt),
                  pl.BlockSpec((D_MODEL, tf), lambda i, f: (0, f)),
                  pl.BlockSpec((D_MODEL, tf), lambda i, f: (0, f + n_f)),
                  pl.BlockSpec((tf, D_MODEL), lambda i, f: (f, 0)),
                  fixed((1, D_MODEL)), fixed((1, D_MODEL))],
        out_specs=row,
        out_shape=jax.ShapeDtypeStruct((m, D_MODEL), F32),
        scratch_shapes=[pltpu.VMEM((tm, D_MODEL), BF16), pltpu.VMEM((tm, D_MODEL), F32)],
        compiler_params=_cparams(("parallel", "arbitrary")),
        name="ffn_dense",
    )(x2d, sc, sh, gt, w_gu, w_gu, w_down, ln_g, ln_b)


_MOE_CHUNK = 256
_MOE_FF_TILE = 1408


def _moe_body(x_ref, sc_ref, sh_ref, gt_ref, wr_ref, wg_ref, wu_ref, wd_ref, lg_ref, lb_ref, o_ref,
              h_sc, w_sc, sel_sc, rank_sc, selt_sc, rankt_sc, xg_sc, yg_sc, y_sc, *, chunk):
    e = pl.program_id(1)
    f = pl.program_id(2)
    tb = x_ref.shape[0]
    lane = lax.broadcasted_iota(jnp.int32, (1, LANES), 1)

    @pl.when((e == 0) & (f == 0))
    def _():
        h = _ln_rows(x_ref[...]) * (1.0 + sc_ref[...]) + sh_ref[...]
        h_sc[...] = h.astype(BF16)
        logits = jnp.where(lane < N_EXP, _dot(h, wr_ref[...], HIGHEST), NEG)
        ex = jnp.exp(logits - jnp.max(logits, axis=-1, keepdims=True))
        probs = ex / jnp.sum(ex, axis=-1, keepdims=True)
        lane_f = lane.astype(F32)
        p1 = jnp.max(probs, axis=-1, keepdims=True)
        i1 = jnp.min(jnp.where(probs == p1, lane_f, float(LANES)), axis=-1, keepdims=True)
        rest = jnp.where(lane_f == i1, -1.0, probs)
        p2 = jnp.max(rest, axis=-1, keepdims=True)
        i2 = jnp.min(jnp.where(rest == p2, lane_f, float(LANES)), axis=-1, keepdims=True)
        w_sc[...] = (jnp.where(lane_f == i1, p1, 0.0) + jnp.where(lane_f == i2, p2, 0.0)) / (p1 + p2)
        sel = ((lane_f == i1) | (lane_f == i2)).astype(F32)
        earlier = (lax.broadcasted_iota(jnp.int32, (tb, tb), 1)
                   < lax.broadcasted_iota(jnp.int32, (tb, tb), 0)).astype(BF16)
        rank = _dot(earlier, sel.astype(BF16))
        sel_sc[...] = sel
        rank_sc[...] = rank
        selt_sc[...] = sel.T
        rankt_sc[...] = rank.T
        y_sc[...] = jnp.zeros(y_sc.shape, F32)

    sel_row = selt_sc[pl.ds(e, 1), :]
    rank_row = rankt_sc[pl.ds(e, 1), :]
    count = jnp.sum(sel_row).astype(jnp.int32)
    n_chunks = (count + chunk - 1) // chunk

    @pl.when(f == 0)
    def _():
        def gather(c, carry):
            slot = (c * chunk + lax.broadcasted_iota(jnp.int32, (chunk, 1), 0)).astype(F32)
            pick = ((rank_row == slot) & (sel_row > 0.5)).astype(BF16)
            xg_sc[pl.ds(pl.multiple_of(c * chunk, chunk), chunk), :] = _dot(pick, h_sc[...]).astype(BF16)
            return carry

        lax.fori_loop(0, n_chunks, gather, 0)

    def expert(c, carry):
        rows = pl.ds(pl.multiple_of(c * chunk, chunk), chunk)
        xg = xg_sc[rows, :]
        hid = _silu(_dot(xg, wg_ref[...])) * _dot(xg, wu_ref[...])
        part = _dot(hid.astype(BF16), wd_ref[...])

        @pl.when(f == 0)
        def _():
            yg_sc[rows, :] = part

        @pl.when(f > 0)
        def _():
            yg_sc[rows, :] += part

        return carry

    lax.fori_loop(0, n_chunks, expert, 0)

    @pl.when(f == pl.num_programs(2) - 1)
    def _():
        pick_e = lane == e
        sel_col = jnp.sum(jnp.where(pick_e, sel_sc[...], 0.0), axis=-1, keepdims=True)
        rank_col = jnp.sum(jnp.where(pick_e, rank_sc[...], 0.0), axis=-1, keepdims=True)
        w_col = jnp.sum(jnp.where(pick_e, w_sc[...], 0.0), axis=-1, keepdims=True)

        def scatter(c, carry):
            slot = (c * chunk + lax.broadcasted_iota(jnp.int32, (1, chunk), 1)).astype(F32)
            place = ((rank_col == slot) & (sel_col > 0.5)).astype(BF16)
            rows = pl.ds(pl.multiple_of(c * chunk, chunk), chunk)
            y_sc[...] += w_col * _dot(place, yg_sc[rows, :].astype(BF16))
            return carry

        lax.fori_loop(0, n_chunks, scatter, 0)

    @pl.when((e == pl.num_programs(1) - 1) & (f == pl.num_programs(2) - 1))
    def _():
        xr = ALPHA * x_ref[...] + gt_ref[...] * y_sc[...]
        o_ref[...] = _ln_rows(xr) * lg_ref[...] + lb_ref[...]


def _moe(x2d, sc, sh, gt, w_router, w_gu, w_down, ln_g, ln_b, tm, rows_per_mod):
    m = x2d.shape[0]
    n_e = w_gu.shape[0]
    tf = _MOE_FF_TILE
    n_f = D_FF // tf
    chunk = min(_MOE_CHUNK, tm)
    row = pl.BlockSpec((tm, D_MODEL), lambda i, e, f: (i, 0), pipeline_mode=pl.Buffered(1))
    fixed = lambda shape: pl.BlockSpec(shape, lambda i, e, f: (0, 0))
    mod = lambda a: _mod_spec(a, tm, rows_per_mod)
    per_tok = lambda: pltpu.VMEM((tm, LANES), F32)
    per_exp = lambda: pltpu.VMEM((LANES, tm), F32)
    return pl.pallas_call(
        functools.partial(_moe_body, chunk=chunk),
        grid=(m // tm, n_e, n_f),
        in_specs=[row, mod(sc), mod(sh), mod(gt), fixed(w_router.shape),
                  pl.BlockSpec((None, D_MODEL, tf), lambda i, e, f: (e, 0, f)),
                  pl.BlockSpec((None, D_MODEL, tf), lambda i, e, f: (e, 0, f + n_f)),
                  pl.BlockSpec((None, tf, D_MODEL), lambda i, e, f: (e, f, 0)),
                  fixed((1, D_MODEL)), fixed((1, D_MODEL))],
        out_specs=row,
        out_shape=jax.ShapeDtypeStruct((m, D_MODEL), F32),
        scratch_shapes=[pltpu.VMEM((tm, D_MODEL), BF16), per_tok(), per_tok(), per_tok(), per_exp(), per_exp(),
                        pltpu.VMEM((tm, D_MODEL), BF16), pltpu.VMEM((tm, D_MODEL), F32),
                        pltpu.VMEM((tm, D_MODEL), F32)],
        compiler_params=_cparams(("parallel", "arbitrary", "arbitrary")),
        name="moe_routed",
    )(x2d, sc, sh, gt, w_router, w_gu, w_gu, w_down, ln_g, ln_b)


def _permute_w_in(w):
    sizes = [C_QKV, GDN_HEADS * GDN_DV, GDN_HEADS, GDN_HEADS, NSA_HEADS * NSA_DH, KV_ROW, KV_ROW, KV_ROW,
             3 * NSA_HEADS, 2 * D_MODEL]
    offs = np.cumsum([0] + sizes)
    qkv, z, b_a, a_a, q_b, kv_c, kv_s, kv_w, g_b, g_m = (w[:, offs[i]:offs[i + 1]] for i in range(len(sizes)))
    pad = jnp.zeros((w.shape[0], _SMALL_W - 2 * GDN_HEADS - 3 * NSA_HEADS), w.dtype)
    return jnp.concatenate([qkv, z, g_m, q_b, kv_c, kv_s, kv_w, b_a, a_a, g_b, pad], axis=1).astype(BF16)


def _layer_weights(l, w_in, conv_w, a_log, dt_bias, norm_w_a, cmp_pos, cmp_w1, cmp_w2, w_up_a, w_up_b, w_out,
                   ln_g, ln_b):
    lane_pad = lambda v: jnp.zeros((1, _SMALL_W), F32).at[0, GDN_HEADS:2 * GDN_HEADS].set(v)
    pe, w1bd, w2bd = _compress_weights(cmp_pos[l], cmp_w1[l], cmp_w2[l])
    return dict(
        w_in=_permute_w_in(w_in[l]), conv_w=conv_w[l], alog=lane_pad(a_log[l]), dt=lane_pad(dt_bias[l]),
        norm_w=norm_w_a[l].reshape(1, GDN_DV), pe=pe, w1bd=w1bd, w2bd=w2bd,
        w_up_a=w_up_a[l].astype(BF16), w_up_b=w_up_b[l].astype(BF16), w_out=w_out[l].astype(BF16),
        ln_g0=ln_g[l, 0].reshape(1, D_MODEL), ln_b0=ln_b[l, 0].reshape(1, D_MODEL),
        ln_g1=ln_g[l, 1].reshape(1, D_MODEL), ln_b1=ln_b[l, 1].reshape(1, D_MODEL))


def _mixer_common(x2d, mods, wts, b, t, t_pad, conv_buf, s0, tm, rows_per_mod):
    sh_m, sc_m = mods[0], mods[1]
    qkv, z, g_m, q_b, kv_c, kv_s, kv_w, small = _inproj(x2d, sc_m, sh_m, wts['w_in'], tm, rows_per_mod)
    seq = lambda a: a.reshape(b, t, a.shape[-1])
    padt = lambda a: jnp.pad(seq(a), ((0, 0), (0, t_pad - t), (0, 0)))
    buf8 = jnp.pad(conv_buf, ((0, 0), (SUBLANES - (CONV_TAPS - 1), 0), (0, 0)))
    qa, ka, va, bg = _gdn_prep(padt(qkv), buf8, wts['conv_w'], padt(small), wts['alog'], wts['dt'], t)
    o_a, s_new = _gdn(qa, ka, va, bg, padt(z), s0, wts['norm_w'])
    o_a = o_a[:, :t].reshape(b * t, GDN_HEADS * GDN_DV)
    conv_new = jnp.concatenate([conv_buf, seq(qkv)], axis=1)[:, -(CONV_TAPS - 1):]
    return (g_m, q_b, kv_c, kv_s, kv_w, small), o_a, s_new, conv_new


def _rows5(a, b, t):
    return a.reshape(b, t, 2, NSA_GROUPS, NSA_DH)


def _feature_major(rows):
    nd = rows.ndim
    return jnp.transpose(rows, tuple(range(nd - 4)) + (nd - 3, nd - 2, nd - 1, nd - 4))


def _prompt_layer(x, mods, wts, ffn_args, routed):
    b, t, _ = x.shape
    x2d = x.reshape(b * t, D_MODEL)
    tm = 256
    zeros_buf = jnp.zeros((b, CONV_TAPS - 1, C_QKV), F32)
    zeros_s = jnp.zeros((b, GDN_HEADS, GDN_DK, GDN_DV), F32)
    (g_m, q_b, kv_c, kv_s, kv_w, small), o_a, s_new, conv_new = _mixer_common(
        x2d, mods, wts, b, t, t, zeros_buf, zeros_s, tm, t)
    pages = (b * t) // PAGE_ROWS
    cmp_rows = _compress(kv_c.reshape(pages, PAGE_ROWS, KV_ROW), jnp.arange(pages, dtype=jnp.int32), b,
                         wts['pe'], wts['w1bd'], wts['w2bd'], False)
    seq = lambda a: a.reshape(b, t, a.shape[-1])
    o_b = _nsa_prompt(seq(q_b), seq(small), cmp_rows, seq(kv_s), seq(kv_w))
    x1 = _outproj(o_a, o_b.reshape(b * t, -1), g_m, x2d, mods[2], wts['w_up_a'], wts['w_up_b'], wts['w_out'],
                  wts['ln_g0'], wts['ln_b0'], 512, t)
    x2 = (_moe if routed else _ffn)(x1, mods[4], mods[3], mods[5], *ffn_args, wts['ln_g1'], wts['ln_b1'],
                                    min(1024, t), t)
    win_new = _rows5(kv_w, b, t)[:, t - min(WINDOW, t):]
    return x2.reshape(b, t, D_MODEL), (s_new, conv_new, _rows5(kv_c, b, t), _rows5(kv_s, b, t), win_new)


def _sample_layer(x, mods, wts, ffn_args, routed, conv_buf, s0, pool_c, pool_s, win_buf, table):
    b = x.shape[0]
    x2d = x.reshape(b, D_MODEL)
    (g_m, q_b, kv_c, kv_s, kv_w, small), o_a, s_new, conv_new = _mixer_common(
        x2d, mods, wts, b, 1, GDN_CHUNK, conv_buf, s0, b, b)
    cmp_rows = _compress(pool_c, table, b, wts['pe'], wts['w1bd'], wts['w2bd'], True)
    win_new = jnp.concatenate([win_buf.reshape(b, -1, KV_ROW)[:, 1:], kv_w[:, None, :]], axis=1)
    win_t = _feature_major(win_buf).reshape(b, KV_ROW, -1)
    q4 = q_b.reshape(b, NSA_GROUPS, NSA_REP, 1, NSA_DH)
    eye = jnp.eye(NSA_GROUPS, dtype=F32).reshape(NSA_GROUPS, 1, NSA_GROUPS, 1)
    qbd = (q4 * eye).reshape(b, NSA_HEADS, NSA_GROUPS * NSA_DH)
    o_b = _nsa_decode(qbd, small.reshape(b, 1, _SMALL_W), cmp_rows, win_t, kv_w.reshape(b, 1, KV_ROW),
                      kv_s.reshape(b, 1, KV_ROW), pool_s, table)
    x1 = _outproj(o_a, o_b, g_m, x2d, mods[2], wts['w_up_a'], wts['w_up_b'], wts['w_out'],
                  wts['ln_g0'], wts['ln_b0'], b, b)
    x2 = (_moe if routed else _ffn)(x1, mods[4], mods[3], mods[5], *ffn_args, wts['ln_g1'], wts['ln_b1'], b, b)
    return x2.reshape(b, 1, D_MODEL), (s_new, conv_new, _rows5(kv_c, b, 1), _rows5(kv_s, b, 1),
                                       _rows5(win_new, b, win_new.shape[1]))


def kernel(x_prompt, x_sample, state_delta, state_conv, cache_cmp_kv, cache_slc_kv, state_win_kv, page_table,
           c_prompt, c_sample, w_ada, b_ada, w_in, conv_w, a_log, dt_bias, norm_w_a, cmp_pos, cmp_w1, cmp_w2,
           w_up_a, w_up_b, w_out, ln_g, ln_b, ffn_w_gu, ffn_w_down, moe_router, moe_w_gu, moe_w_down):
    bp = x_prompt.shape[0]
    db = x_sample.shape[0]
    n_layers = w_in.shape[0]
    n_pool = cache_cmp_kv.shape[1]
    mod_all = _adaln(jnp.concatenate([c_prompt, c_sample], axis=0), w_ada, b_ada)
    table = page_table.reshape(-1)
    pool_c = _feature_major(cache_cmp_kv).reshape(n_layers * n_pool, KV_ROW, PAGE_ROWS)
    pool_s = _feature_major(cache_slc_kv).reshape(n_layers * n_pool, KV_ROW, PAGE_ROWS)
    xp, xs = x_prompt, x_sample
    st_p, st_s = [], []
    for l in range(n_layers):
        wts = _layer_weights(l, w_in, conv_w, a_log, dt_bias, norm_w_a, cmp_pos, cmp_w1, cmp_w2, w_up_a, w_up_b,
                             w_out, ln_g, ln_b)
        routed = l % 2 == 1
        if routed:
            router = jnp.pad(moe_router[l // 2], ((0, 0), (0, LANES - N_EXP)))
            ffn_args = (router, moe_w_gu[l // 2].astype(BF16), moe_w_down[l // 2].astype(BF16))
        else:
            ffn_args = (ffn_w_gu[l // 2].astype(BF16), ffn_w_down[l // 2].astype(BF16))
        mod6 = mod_all[l].reshape(bp + db, 6, D_MODEL)
        mods_p = [mod6[:bp, i].reshape(bp, 1, D_MODEL) for i in range(6)]
        mods_s = [mod6[bp:, i].reshape(1, db, D_MODEL) for i in range(6)]
        xp, sp = _prompt_layer(xp, mods_p, wts, ffn_args, routed)
        xs, ss = _sample_layer(xs, mods_s, wts, ffn_args, routed, state_conv[l], state_delta[l],
                               pool_c, pool_s, state_win_kv[l], table + l * n_pool)
        st_p.append(sp)
        st_s.append(ss)
    stack = lambda sts, i: jnp.stack([s[i] for s in sts])
    return (xp, xs) + tuple(stack(st_p, i) for i in range(5)) + tuple(stack(st_s, i) for i in range(5))
```

```python
import functools
import math

import jax
import jax.numpy as jnp
import numpy as np
from jax import lax
from jax.experimental import pallas as pl
from jax.experimental.pallas import tpu as pltpu

F32 = jnp.float32
BF16 = jnp.bfloat16
HIGHEST = lax.Precision.HIGHEST

D_MODEL = 1024
N_LAYERS = 2
PAGE_ROWS = 128
GDN_HEADS = 4
GDN_DK = 128
GDN_DV = 128
CONV_TAPS = 4
GDN_CHUNK = 64
C_QKV = 2 * GDN_HEADS * GDN_DK + GDN_HEADS * GDN_DV
NSA_HEADS = 8
NSA_GROUPS = 2
NSA_REP = NSA_HEADS // NSA_GROUPS
NSA_DH = 64
KV_ROW = 2 * NSA_GROUPS * NSA_DH
CMP_LEN = 32
CMP_STRIDE = 16
CMP_HID = 256
SLC_BLK = 64
N_SEL = 16
WINDOW = 512
Q_BLK = 128
D_FF = 2816
N_EXP = 8
ALPHA = (2 * N_LAYERS) ** 0.25
EPS = 1e-5
NEG = -1e30

LANES = 128
SUBLANES = 8
VMEM_LIMIT = 56 * 1024 * 1024

_SMALL_W = LANES
IN_OUT_WIDTHS = (C_QKV, GDN_HEADS * GDN_DV, 2 * D_MODEL, NSA_HEADS * NSA_DH, KV_ROW, KV_ROW, KV_ROW, _SMALL_W)


def _cparams(sem):
    return pltpu.CompilerParams(dimension_semantics=sem, vmem_limit_bytes=VMEM_LIMIT)


def _silu(x):
    return x * jax.nn.sigmoid(x)


def _ln_rows(x):
    mu = jnp.mean(x, axis=-1, keepdims=True)
    xc = x - mu
    var = jnp.mean(xc * xc, axis=-1, keepdims=True)
    return xc * lax.rsqrt(var + EPS)


def _dot(a, b, precision=None):
    return jnp.dot(a, b, preferred_element_type=F32, precision=precision)


def _dot_nt(a, b, precision=None):
    return lax.dot_general(a, b, (((1,), (1,)), ((), ())), preferred_element_type=F32, precision=precision)


def _dot_tn(a, b, precision=None):
    return lax.dot_general(a, b, (((0,), (0,)), ((), ())), preferred_element_type=F32, precision=precision)


def _masked_softmax(s, mask, axis=-1, exp=jnp.exp):
    s = jnp.where(mask, s, NEG)
    m = jnp.max(s, axis=axis, keepdims=True)
    e = jnp.where(mask, exp(s - m), 0.0)
    return e * (1.0 / jnp.maximum(jnp.sum(e, axis=axis, keepdims=True), 1e-30))


def _select_top(imp, lane_idx, n_lanes, axis=-1):
    sel = jnp.zeros(imp.shape, F32)
    lane_idx = lane_idx.astype(F32)
    for _ in range(N_SEL):
        mx = jnp.max(imp, axis=axis, keepdims=True)
        idx = jnp.min(jnp.where(imp == mx, lane_idx, float(n_lanes)), axis=axis, keepdims=True)
        hit = lane_idx == idx
        sel = jnp.where(hit, 1.0, sel)
        imp = jnp.where(hit, -3e38, imp)
    return sel


def _adaln_body(c_ref, w_ref, b_ref, o_ref):
    s = _silu(c_ref[...]).astype(BF16)
    o_ref[...] = _dot(s, w_ref[...].astype(BF16)) + b_ref[...]


def _adaln(c_all, w_ada, b_ada):
    n_l, d, n = w_ada.shape
    r = c_all.shape[0]
    tn = 1536
    return pl.pallas_call(
        _adaln_body,
        grid=(n_l, n // tn),
        in_specs=[pl.BlockSpec((r, d), lambda l, j: (0, 0)),
                  pl.BlockSpec((None, d, tn), lambda l, j: (l, 0, j)),
                  pl.BlockSpec((None, 1, tn), lambda l, j: (l, 0, j))],
        out_specs=pl.BlockSpec((None, r, tn), lambda l, j: (l, 0, j)),
        out_shape=jax.ShapeDtypeStruct((n_l, r, n), F32),
        compiler_params=_cparams(("parallel", "parallel")),
        name="adaln_mod",
    )(c_all, w_ada, b_ada.reshape(n_l, 1, n))


def _inproj_body(x_ref, sc_ref, sh_ref, w_ref, *o_refs):
    h = (_ln_rows(x_ref[...]) * (1.0 + sc_ref[...]) + sh_ref[...]).astype(BF16)
    off = 0
    for o_ref, n in zip(o_refs, IN_OUT_WIDTHS):
        o_ref[...] = _dot(h, w_ref[:, off:off + n])
        off += n


def _mod_spec(mod, tm, rows_per_mod):
    if mod.shape[1] == 1:
        return pl.BlockSpec((None, 1, D_MODEL), lambda i, *_: ((i * tm) // rows_per_mod, 0, 0))
    return pl.BlockSpec((None, tm, D_MODEL), lambda i, *_: (0, i, 0))


def _inproj(x2d, sc, sh, w_perm, tm, rows_per_mod):
    m = x2d.shape[0]
    n_tot = w_perm.shape[1]
    return pl.pallas_call(
        _inproj_body,
        grid=(m // tm,),
        in_specs=[pl.BlockSpec((tm, D_MODEL), lambda i: (i, 0)),
                  _mod_spec(sc, tm, rows_per_mod), _mod_spec(sh, tm, rows_per_mod),
                  pl.BlockSpec((D_MODEL, n_tot), lambda i: (0, 0))],
        out_specs=[pl.BlockSpec((tm, n), lambda i: (i, 0)) for n in IN_OUT_WIDTHS],
        out_shape=[jax.ShapeDtypeStruct((m, n), F32) for n in IN_OUT_WIDTHS],
        compiler_params=_cparams(("parallel",)),
        name="ln_inproj",
    )(x2d, sc, sh, w_perm)


def _prep_body(qkv_ref, buf_ref, cw_ref, sm_ref, alog_ref, dt_ref, q_ref, k_ref, v_ref, bg_ref, xp_sc,
               *, tb, t_valid):
    i = pl.program_id(1)

    @pl.when(i == 0)
    def _():
        xp_sc[0:SUBLANES, :] = buf_ref[...]

    x = qkv_ref[...]
    xp_sc[SUBLANES:SUBLANES + tb, :] = x
    y = cw_ref[3:4, :] * x
    for j in range(CONV_TAPS - 1):
        lo = SUBLANES - (CONV_TAPS - 1) + j
        y = y + cw_ref[j:j + 1, :] * xp_sc[lo:lo + tb, :]
    xp_sc[0:SUBLANES, :] = xp_sc[tb:tb + SUBLANES, :]
    a = _silu(y)
    rows = i * tb + lax.broadcasted_iota(jnp.int32, (tb, 1), 0)
    live = rows < t_valid
    hk = GDN_HEADS * GDN_DK
    for h in range(GDN_HEADS):
        qh = a[:, h * GDN_DK:(h + 1) * GDN_DK]
        kh = a[:, hk + h * GDN_DK:hk + (h + 1) * GDN_DK]
        qn = qh * lax.rsqrt(jnp.sum(qh * qh, axis=-1, keepdims=True) + 1e-6) * (GDN_DK ** -0.5)
        kn = kh * lax.rsqrt(jnp.sum(kh * kh, axis=-1, keepdims=True) + 1e-6)
        q_ref[:, h * GDN_DK:(h + 1) * GDN_DK] = jnp.where(live, qn, 0.0)
        k_ref[:, h * GDN_DK:(h + 1) * GDN_DK] = jnp.where(live, kn, 0.0)
    v_ref[...] = jnp.where(live, a[:, 2 * hk:], 0.0)
    sm = sm_ref[...]
    beta = jax.nn.sigmoid(sm)
    z = sm + dt_ref[...]
    softplus = jnp.maximum(z, 0.0) + jnp.log(1.0 + jnp.exp(-jnp.abs(z)))
    g = -jnp.exp(alog_ref[...]) * softplus
    lane = lax.broadcasted_iota(jnp.int32, sm.shape, 1)
    bg_ref[...] = jnp.where(live, jnp.where(lane < GDN_HEADS, beta, g), 0.0)


def _gdn_prep(qkv, buf8, conv_w, small, alog_vec, dt_vec, t_valid):
    b, t, _ = qkv.shape
    tb = min(t, 512)
    hd = GDN_HEADS * GDN_DK
    row = lambda bi, i: (bi, i, 0)
    fixed = lambda bi, i: (0, 0)
    return pl.pallas_call(
        functools.partial(_prep_body, tb=tb, t_valid=t_valid),
        grid=(b, t // tb),
        in_specs=[pl.BlockSpec((None, tb, C_QKV), row),
                  pl.BlockSpec((None, SUBLANES, C_QKV), lambda bi, i: (bi, 0, 0)),
                  pl.BlockSpec((CONV_TAPS, C_QKV), fixed),
                  pl.BlockSpec((None, tb, _SMALL_W), row),
                  pl.BlockSpec((1, _SMALL_W), fixed), pl.BlockSpec((1, _SMALL_W), fixed)],
        out_specs=[pl.BlockSpec((None, tb, hd), row), pl.BlockSpec((None, tb, hd), row),
                   pl.BlockSpec((None, tb, hd), row), pl.BlockSpec((None, tb, _SMALL_W), row)],
        out_shape=[jax.ShapeDtypeStruct((b, t, hd), F32)] * 3 + [jax.ShapeDtypeStruct((b, t, _SMALL_W), F32)],
        scratch_shapes=[pltpu.VMEM((tb + SUBLANES, C_QKV), F32)],
        compiler_params=_cparams(("parallel", "arbitrary")),
        name="gdn_prep",
    )(qkv, buf8, conv_w, small, alog_vec, dt_vec)


def _split_bf16(x):
    hi = x.astype(BF16)
    return hi, (x - hi.astype(F32)).astype(BF16)


def _dot3(a, b, dims):
    ah, al = _split_bf16(a)
    bh, bl = _split_bf16(b)
    d = lambda x, y: lax.dot_general(x, y, dims, preferred_element_type=F32)
    return d(ah, bh) + (d(ah, bl) + d(al, bh))


def _bmm(a, b):
    return _dot3(a, b, (((2,), (1,)), ((0,), (0,))))


def _bmm_nt(a, b):
    return _dot3(a, b, (((2,), (2,)), ((0,), (0,))))


def _gdn_ut_body(q_ref, k_ref, v_ref, bg_ref, u_ref, w_ref, qd_ref, kd_ref, qk_ref, gl_ref, *, n_chunks, c_len):
    nc = n_chunks
    tb = nc * c_len
    row = lax.broadcasted_iota(jnp.int32, (nc, c_len, c_len), 1)
    col = lax.broadcasted_iota(jnp.int32, (nc, c_len, c_len), 2)
    incl = row >= col
    strict = row > col
    incl_f = incl.astype(F32)
    strict_f = strict.astype(F32)
    eye = (row == col).astype(F32)
    bg = bg_ref[...]
    lane = lax.broadcasted_iota(jnp.int32, (tb, _SMALL_W), 1)
    gl_all = jnp.zeros((tb, _SMALL_W), F32)
    split = lambda x: x.reshape(nc, c_len, x.shape[-1])
    for h in range(GDN_HEADS):
        sl = slice(h * GDN_DK, (h + 1) * GDN_DK)
        q = split(q_ref[:, sl])
        k = split(k_ref[:, sl])
        v = split(v_ref[:, sl])
        beta = split(bg[:, h:h + 1])
        g = split(bg[:, GDN_HEADS + h:GDN_HEADS + h + 1])
        dlog = lax.dot_general(incl_f, g * strict_f, (((2,), (1,)), ((0,), (0,))), preferred_element_type=F32,
                               precision=HIGHEST)
        gc = dlog[:, :, 0:1] + g[:, 0:1, :]
        decay = jnp.where(incl, jnp.exp(dlog), 0.0)
        egc = jnp.exp(gc)
        g_last = gc[:, c_len - 1:c_len, :]
        kb = k * beta
        a = jnp.where(strict, _bmm_nt(kb, k) * decay, 0.0)
        p = -a
        tinv = eye + p
        for _ in range(int(math.log2(c_len)) - 1):
            p = _bmm(p, p)
            tinv = tinv + _bmm(tinv, p)
        u_ref[:, sl] = _bmm(tinv, v * beta).reshape(tb, GDN_DV)
        w_ref[:, sl] = _bmm(tinv, kb * egc).reshape(tb, GDN_DK)
        qk = jnp.where(incl, _bmm_nt(q, k) * decay, 0.0)
        qk_ref[:, h * c_len:(h + 1) * c_len] = qk.reshape(tb, c_len)
        qd_ref[:, sl] = (q * egc).reshape(tb, GDN_DK)
        kd_ref[:, sl] = (k * jnp.exp(g_last - gc)).reshape(tb, GDN_DK)
        gl = jnp.broadcast_to(jnp.exp(g_last), (nc, c_len, 1)).reshape(tb, 1)
        gl_all = jnp.where(lane == h, gl, gl_all)
    gl_ref[...] = gl_all


def _gdn_scan_body(u_ref, w_ref, qd_ref, kd_ref, qk_ref, gl_ref, z_ref, s0_ref, nw_ref, o_ref, sfin_ref, s_sc,
                   *, n_chunks, n_seq, c_len):
    i = pl.program_id(1)

    @pl.when(i == 0)
    def _():
        s_sc[...] = s0_ref[...]

    nw = nw_ref[...]

    def chunk(c, carry):
        r0 = pl.multiple_of(c * c_len, c_len)
        rows = pl.ds(r0, c_len)
        gl_row = gl_ref[:, pl.ds(r0, 1), :]
        for h in range(GDN_HEADS):
            sl = slice(h * GDN_DK, (h + 1) * GDN_DK)
            s = s_sc[:, h]
            v_new = u_ref[:, rows, sl] - _bmm(w_ref[:, rows, sl], s)
            o = _bmm(qd_ref[:, rows, sl], s) + _bmm(qk_ref[:, rows, h * c_len:(h + 1) * c_len], v_new)
            kd = kd_ref[:, rows, sl]
            upd = jnp.stack([_dot3(kd[b], v_new[b], (((0,), (0,)), ((), ()))) for b in range(n_seq)])
            s_sc[:, h] = s * gl_row[:, :, h:h + 1] + upd
            z = z_ref[:, rows, sl]
            on = o * lax.rsqrt(jnp.mean(o * o, axis=-1, keepdims=True) + EPS) * nw
            o_ref[:, rows, sl] = on * _silu(z)
        return carry

    lax.fori_loop(0, n_chunks, chunk, 0)

    @pl.when(i == pl.num_programs(1) - 1)
    def _():
        sfin_ref[...] = s_sc[...]


_GDN_SEQ_PER_STEP = 8
_GDN_SHORT_CHUNK = 16


def _gdn(q, k, v, bg, z, s0, norm_w):
    b, t, hd = q.shape
    tb = min(t, 512)
    c_len = min(GDN_CHUNK, t)
    nc = tb // c_len
    row = lambda bi, i: (bi, i, 0)
    wide = lambda n: pl.BlockSpec((None, tb, n), row)
    qkw = GDN_HEADS * c_len
    u, w, qd, kd, qk, gl = pl.pallas_call(
        functools.partial(_gdn_ut_body, n_chunks=nc, c_len=c_len),
        grid=(b, t // tb),
        in_specs=[wide(hd)] * 3 + [wide(_SMALL_W)],
        out_specs=[wide(hd)] * 4 + [wide(qkw), wide(_SMALL_W)],
        out_shape=[jax.ShapeDtypeStruct((b, t, hd), F32)] * 4
        + [jax.ShapeDtypeStruct((b, t, qkw), F32), jax.ShapeDtypeStruct((b, t, _SMALL_W), F32)],
        compiler_params=_cparams(("parallel", "parallel")),
        name="gdn_ut",
    )(q, k, v, bg)
    nb = min(b, _GDN_SEQ_PER_STEP)
    grp = lambda gi, i: (gi, i, 0)
    st = lambda gi, i: (gi, 0, 0, 0)
    seqs = lambda n: pl.BlockSpec((nb, tb, n), grp)
    return pl.pallas_call(
        functools.partial(_gdn_scan_body, n_chunks=nc, n_seq=nb, c_len=c_len),
        grid=(b // nb, t // tb),
        in_specs=[seqs(hd)] * 4 + [seqs(qkw), seqs(_SMALL_W), seqs(hd),
                                   pl.BlockSpec((nb, GDN_HEADS, GDN_DK, GDN_DV), st),
                                   pl.BlockSpec((1, GDN_DV), lambda gi, i: (0, 0))],
        out_specs=[seqs(hd), pl.BlockSpec((nb, GDN_HEADS, GDN_DK, GDN_DV), st)],
        out_shape=[jax.ShapeDtypeStruct((b, t, hd), F32),
                   jax.ShapeDtypeStruct((b, GDN_HEADS, GDN_DK, GDN_DV), F32)],
        scratch_shapes=[pltpu.VMEM((nb, GDN_HEADS, GDN_DK, GDN_DV), F32)],
        compiler_params=_cparams(("parallel", "arbitrary")),
        name="gdn_scan",
    )(u, w, qd, kd, qk, gl, z, s0, norm_w)


_SEG_PER_PAGE = PAGE_ROWS // CMP_STRIDE
_CMP_PAGES_PER_STEP = 32


def _compress_body(pt_ref, *refs, pages_per_seq, npg, feature_major):
    page_refs = refs[:npg]
    pe_ref, w1_ref, w2_ref, o_ref, carry_sc, rows_sc = refs[npg:]
    step = pl.program_id(0)
    first = (step % (pages_per_seq // npg)) == 0
    for p, r in enumerate(page_refs):
        for kv in range(2):
            half = slice(kv * LANES, (kv + 1) * LANES)
            rows_sc[kv, p * PAGE_ROWS:(p + 1) * PAGE_ROWS, :] = r[half, :].T if feature_major else r[:, half]
    n_rows = npg * _SEG_PER_PAGE
    n_m = CMP_LEN // CMP_STRIDE
    gh = NSA_GROUPS * CMP_HID
    acc = [[jnp.zeros((n_rows, gh), F32) for _ in range(2)] for _ in range(n_m)]
    for sp in range(CMP_STRIDE // 2):
        for kv in range(2):
            x = jnp.concatenate([rows_sc[kv, pl.ds(2 * sp + i, n_rows, stride=CMP_STRIDE), :] for i in range(2)],
                                axis=1)
            for m in range(n_m):
                xm = (x + pe_ref[m, sp, kv:kv + 1, :]).astype(BF16)
                acc[m][kv] = acc[m][kv] + _dot(xm, w1_ref[m, sp, kv])
    part = [jnp.concatenate(a, axis=1) for a in acc]
    prev = jnp.where(first, 0.0, carry_sc[...])
    rows = lax.broadcasted_iota(jnp.int32, (n_rows, 1), 0)
    shifted = jnp.where(rows == 0, prev, pltpu.roll(part[0], 1, 0))
    carry_sc[...] = part[0][n_rows - 1:n_rows, :]
    hid = _silu(shifted + part[1]).astype(BF16)
    for kv in range(2):
        o_ref[:, kv * LANES:(kv + 1) * LANES] = _dot(hid[:, kv * gh:(kv + 1) * gh], w2_ref[kv]).astype(BF16)


def _compress(pool, table, n_seq, pe, w1bd, w2bd, feature_major):
    pages_per_seq = table.shape[0] // n_seq
    npg = min(_CMP_PAGES_PER_STEP, pages_per_seq)
    n_steps = table.shape[0] // npg
    page_spec = lambda p: pl.BlockSpec((None,) + pool.shape[1:], lambda i, pt: (pt[i * npg + p], 0, 0))
    fixed = lambda nd: (lambda i, pt: (0,) * nd)
    out = pl.pallas_call(
        functools.partial(_compress_body, pages_per_seq=pages_per_seq, npg=npg, feature_major=feature_major),
        grid_spec=pltpu.PrefetchScalarGridSpec(
            num_scalar_prefetch=1,
            grid=(n_steps,),
            in_specs=[page_spec(p) for p in range(npg)]
            + [pl.BlockSpec(pe.shape, fixed(4)), pl.BlockSpec(w1bd.shape, fixed(5)),
               pl.BlockSpec(w2bd.shape, fixed(3))],
            out_specs=pl.BlockSpec((npg * _SEG_PER_PAGE, KV_ROW), lambda i, pt: (i, 0)),
            scratch_shapes=[pltpu.VMEM((1, 2 * NSA_GROUPS * CMP_HID), F32),
                            pltpu.VMEM((2, npg * PAGE_ROWS, LANES), F32)],
        ),
        out_shape=jax.ShapeDtypeStruct((n_steps * npg * _SEG_PER_PAGE, KV_ROW), BF16),
        compiler_params=_cparams(("arbitrary",)),
        name="nsa_compress",
    )(table, *([pool] * npg), pe, w1bd, w2bd)
    return out.reshape(n_seq, pages_per_seq * _SEG_PER_PAGE, KV_ROW)


def _compress_weights(cmp_pos, cmp_w1, cmp_w2):
    n_m = CMP_LEN // CMP_STRIDE
    pe = cmp_pos.reshape(2, n_m, CMP_STRIDE, 1, NSA_DH)
    pe = jnp.broadcast_to(pe, (2, n_m, CMP_STRIDE, NSA_GROUPS, NSA_DH))
    gd = NSA_GROUPS * NSA_DH
    pe = jnp.transpose(pe.reshape(2, n_m, CMP_STRIDE // 2, 2, gd), (1, 2, 0, 3, 4)).reshape(
        n_m, CMP_STRIDE // 2, 2, 2 * gd)
    w1 = cmp_w1.reshape(2, n_m, CMP_STRIDE // 2, 2, NSA_DH, CMP_HID)
    w1 = jnp.transpose(w1, (1, 2, 0, 3, 4, 5))
    eye = jnp.eye(NSA_GROUPS, dtype=F32)
    w1bd = jnp.einsum('mskidh,gG->mskigdGh', w1, eye).reshape(
        n_m, CMP_STRIDE // 2, 2, 2 * gd, NSA_GROUPS * CMP_HID).astype(BF16)
    w2bd = jnp.einsum('khd,gG->kghGd', cmp_w2, eye).reshape(
        2, NSA_GROUPS * CMP_HID, NSA_GROUPS * NSA_DH).astype(BF16)
    return pe, w1bd, w2bd


def _overlap_matrix(n_rows, n_blk_lanes):
    ci = (np.arange(n_rows)[:, None] - 1) * CMP_STRIDE
    sj = np.arange(n_blk_lanes)[None, :] * SLC_BLK
    ov = (ci < sj + SLC_BLK) & (ci + CMP_LEN > sj) & (ci >= 0)
    return jnp.asarray(ov.astype(np.float32))


_SEL_KEY_BLOCK = 1024


def _nsa_prompt_body(qt_ref, smt_ref, ck_ref, cvt_ref, sk_ref, svt_ref, wk_ref, wvt_ref, ovt_ref, o_ref,
                     m_sc, l_sc, acc_sc):
    iq = pl.program_id(1)
    qb = Q_BLK
    t0 = iq * qb
    n_cmp_rows = ck_ref.shape[0]
    n_blk = ovt_ref.shape[0]
    w_rows = WINDOW + qb
    gd = NSA_GROUPS * NSA_DH
    t_row = t0 + lax.broadcasted_iota(jnp.int32, (1, qb), 1)
    gates = jax.nn.sigmoid(smt_ref[...])
    ci = lax.broadcasted_iota(jnp.int32, (n_cmp_rows, 1), 0)
    cmp_mask = (ci >= 1) & (CMP_STRIDE * ci + (CMP_LEN - CMP_STRIDE - 1) <= t_row)
    blk = lax.broadcasted_iota(jnp.int32, (n_blk, 1), 0)
    forced = (blk == t_row // SLC_BLK) | (blk == 0)
    valid = blk * SLC_BLK <= t_row
    start = pl.multiple_of(jnp.maximum(t0 - WINDOW, 0), qb)
    wpos = start + lax.broadcasted_iota(jnp.int32, (w_rows, 1), 0)
    dlt = t_row - wpos
    win_mask = (dlt >= 0) & (dlt < WINDOW)
    kblk = min(_SEL_KEY_BLOCK, sk_ref.shape[0])
    e_key = lax.broadcasted_iota(jnp.int32, (kblk, n_blk), 0) // SLC_BLK
    e_blk = lax.broadcasted_iota(jnp.int32, (kblk, n_blk), 1)
    key_col = lax.broadcasted_iota(jnp.int32, (kblk, 1), 0)
    feat = lax.broadcasted_iota(jnp.int32, (gd, 1), 0)
    scale = NSA_DH ** -0.5 * math.log2(math.e)
    heads = lambda x: [x[:, r * qb:(r + 1) * qb] for r in range(NSA_REP)]

    for g in range(NSA_GROUPS):
        vrows = slice(g * NSA_DH, (g + 1) * NSA_DH)
        qt = jnp.concatenate([qt_ref[(g * NSA_REP + r) * NSA_DH:(g * NSA_REP + r + 1) * NSA_DH, :]
                              for r in range(NSA_REP)], axis=1) * scale
        qt = jnp.concatenate([qt, jnp.zeros_like(qt)] if g == 0 else [jnp.zeros_like(qt), qt], axis=0)
        qt = qt.astype(BF16)
        s_c = _dot(ck_ref[...], qt)
        p_c = [_masked_softmax(s, cmp_mask, axis=0, exp=jnp.exp2) for s in heads(s_c)]
        o_c = _dot(cvt_ref[vrows, :], jnp.concatenate(p_c, axis=1).astype(BF16))
        imp = _dot(ovt_ref[...], p_c[0] + p_c[1] + p_c[2] + p_c[3], HIGHEST)
        imp = jnp.where(forced, 1e9, jnp.where(valid, imp, -1.0))
        sel = _select_top(imp, blk, n_blk, axis=0).astype(BF16)

        m_sc[...] = jnp.full(m_sc.shape, NEG, F32)
        l_sc[...] = jnp.zeros(l_sc.shape, F32)
        acc_sc[...] = jnp.zeros(acc_sc.shape, F32)

        def kv_step(kb, carry):
            r0 = pl.multiple_of(kb * kblk, kblk)
            s_s = _dot(sk_ref[pl.ds(r0, kblk), :], qt)
            expand = (e_blk == (kblk // SLC_BLK) * kb + e_key).astype(BF16)
            msk = (_dot(expand, sel) > 0.5) & (r0 + key_col <= t_row)
            m_old = m_sc[...]
            l_old = l_sc[...]
            m_new, l_new, es = [], [], []
            for r, s in enumerate(heads(s_s)):
                s = jnp.where(msk, s, NEG)
                m_r = jnp.maximum(m_old[:, r * qb:(r + 1) * qb], jnp.max(s, axis=0, keepdims=True))
                e = jnp.exp2(s - m_r)
                m_new.append(m_r)
                es.append(e.astype(BF16))
                l_new.append(jnp.sum(e, axis=0, keepdims=True))
            m_new = jnp.concatenate(m_new, axis=1)
            corr = jnp.exp2(m_old - m_new)
            m_sc[...] = m_new
            l_sc[...] = corr * l_old + jnp.concatenate(l_new, axis=1)
            pv = _dot(svt_ref[vrows, pl.ds(r0, kblk)], jnp.concatenate(es, axis=1))
            acc_sc[...] = corr * acc_sc[...] + pv
            return carry

        lax.fori_loop(0, (t0 + qb + kblk - 1) // kblk, kv_step, 0)
        o_s = acc_sc[...] * (1.0 / jnp.maximum(l_sc[...], 1e-30))
        s_w = _dot(wk_ref[pl.ds(start, w_rows), :], qt)
        p_w = jnp.concatenate([_masked_softmax(s, win_mask, axis=0, exp=jnp.exp2).astype(BF16) for s in heads(s_w)],
                              axis=1)
        o_w = _dot(wvt_ref[vrows, pl.ds(start, w_rows)], p_w)
        for r in range(NSA_REP):
            h = g * NSA_REP + r
            row0 = 2 * GDN_HEADS + 3 * h
            cs = slice(r * qb, (r + 1) * qb)
            o_ref[h * NSA_DH:(h + 1) * NSA_DH, :] = (
                gates[row0:row0 + 1, :] * o_c[:, cs] + gates[row0 + 1:row0 + 2, :] * o_s[:, cs]
                + gates[row0 + 2:row0 + 3, :] * o_w[:, cs])


def _nsa_prompt(q_b, small, cmp_rows, kv_s, kv_w):
    b, t, _ = q_b.shape
    gd = NSA_GROUPS * NSA_DH
    n_blk = max(t // SLC_BLK, LANES)
    ovt = _overlap_matrix(cmp_rows.shape[1], n_blk).T
    tr = lambda a: jnp.swapaxes(a, 1, 2)
    keys = lambda kv: kv[:, :, :gd].astype(BF16)
    vals_t = lambda kv: tr(kv[:, :, gd:]).astype(BF16)
    whole = lambda bi, i: (bi, 0, 0)
    tok = lambda bi, i: (bi, 0, i)
    rq = NSA_REP * Q_BLK
    out_t = pl.pallas_call(
        _nsa_prompt_body,
        grid=(b, t // Q_BLK),
        in_specs=[pl.BlockSpec((None, NSA_HEADS * NSA_DH, Q_BLK), tok),
                  pl.BlockSpec((None, _SMALL_W, Q_BLK), tok),
                  pl.BlockSpec((None, cmp_rows.shape[1], gd), whole),
                  pl.BlockSpec((None, gd, cmp_rows.shape[1]), whole),
                  pl.BlockSpec((None, t, gd), whole), pl.BlockSpec((None, gd, t), whole),
                  pl.BlockSpec((None, t, gd), whole), pl.BlockSpec((None, gd, t), whole),
                  pl.BlockSpec(ovt.shape, lambda bi, i: (0, 0))],
        out_specs=pl.BlockSpec((None, NSA_HEADS * NSA_DH, Q_BLK), tok),
        out_shape=jax.ShapeDtypeStruct((b, NSA_HEADS * NSA_DH, t), F32),
        scratch_shapes=[pltpu.VMEM((1, rq), F32), pltpu.VMEM((1, rq), F32), pltpu.VMEM((NSA_DH, rq), F32)],
        compiler_params=_cparams(("parallel", "arbitrary")),
        name="nsa_prompt",
    )(tr(q_b), tr(small), cmp_rows[:, :, :gd], tr(cmp_rows[:, :, gd:]), keys(kv_s), vals_t(kv_s),
      keys(kv_w), vals_t(kv_w), ovt)
    return tr(out_t)


_DEC_PAGES_PER_STEP = 16


def _nsa_decode_body(pt_ref, qbd_ref, sm_ref, cmp_ref, wint_ref, neww_ref, news_ref, ov_ref, *refs, n_past_blk,
                     npg):
    page_refs = refs[:npg]
    o_ref, sel_sc, m_sc, l_sc, acc_sc, oc_sc, ow_sc = refs[npg:]
    j = pl.program_id(1)
    n_blk = ov_ref.shape[1]
    gd = NSA_GROUPS * NSA_DH
    qf = qbd_ref[...] * (NSA_DH ** -0.5)
    q = qf.astype(BF16)
    head = lax.broadcasted_iota(jnp.int32, (NSA_HEADS, gd), 0)
    lane = lax.broadcasted_iota(jnp.int32, (NSA_HEADS, gd), 1)
    own = (lane // NSA_DH == head // NSA_REP).astype(F32)

    @pl.when(j == 0)
    def _():
        n_rows = cmp_ref.shape[0]
        ci = lax.broadcasted_iota(jnp.int32, (1, n_rows), 1)
        p_c = _masked_softmax(_dot_nt(q, cmp_ref[:, 0:gd]), ci >= 1)
        oc_sc[...] = _dot(p_c.astype(BF16), cmp_ref[:, gd:2 * gd]) * own
        hr = lax.broadcasted_iota(jnp.int32, (NSA_HEADS, NSA_HEADS), 0) // NSA_REP
        hc = lax.broadcasted_iota(jnp.int32, (NSA_HEADS, NSA_HEADS), 1) // NSA_REP
        p_grp = _dot((hr == hc).astype(F32), p_c, HIGHEST)
        imp = _dot(p_grp, ov_ref[...], HIGHEST)
        blk = lax.broadcasted_iota(jnp.int32, (1, n_blk), 1)
        imp = jnp.where((blk == n_past_blk) | (blk == 0), 1e9, jnp.where(blk < n_past_blk, imp, -2e38))
        sel_sc[...] = _select_top(imp, blk, n_blk)
        n_win = wint_ref.shape[1]
        wi = lax.broadcasted_iota(jnp.int32, (1, n_win), 1)
        neww = neww_ref[...]
        s_w = jnp.where(wi >= 1, _dot(q, wint_ref[0:gd, :].astype(BF16)), NEG)
        s_cur = jnp.sum(qf * neww[:, 0:gd], axis=-1, keepdims=True)
        m_w = jnp.maximum(jnp.max(s_w, axis=-1, keepdims=True), s_cur)
        e_w = jnp.where(wi >= 1, jnp.exp(s_w - m_w), 0.0)
        e_cur = jnp.exp(s_cur - m_w)
        den = jnp.maximum(jnp.sum(e_w, axis=-1, keepdims=True) + e_cur, 1e-30)
        pv_w = _dot_nt(e_w.astype(BF16), wint_ref[gd:2 * gd, :].astype(BF16)) + e_cur * neww[:, gd:2 * gd]
        ow_sc[...] = pv_w * (1.0 / den) * own
        m_sc[...] = jnp.full(m_sc.shape, NEG, F32)
        l_sc[...] = jnp.zeros(l_sc.shape, F32)
        acc_sc[...] = jnp.zeros(acc_sc.shape, F32)

    n_keys = npg * PAGE_ROWS
    kt = jnp.concatenate([r[0:gd, :] for r in page_refs], axis=1).astype(BF16)
    vt = jnp.concatenate([r[gd:2 * gd, :] for r in page_refs], axis=1).astype(BF16)
    e_row = lax.broadcasted_iota(jnp.int32, (n_blk, n_keys), 0)
    e_col = lax.broadcasted_iota(jnp.int32, (n_blk, n_keys), 1) // SLC_BLK
    first_blk = (PAGE_ROWS // SLC_BLK) * (j * npg)
    msk = _dot(sel_sc[...].astype(BF16), (e_row == first_blk + e_col).astype(BF16)) > 0.5
    sc = jnp.where(msk, _dot(q, kt), NEG)
    m_old = m_sc[...]
    m_new = jnp.maximum(m_old, jnp.max(sc, axis=-1, keepdims=True))
    corr = jnp.exp(m_old - m_new)
    e = jnp.where(msk, jnp.exp(sc - m_new), 0.0)
    l_sc[...] = corr * l_sc[...] + jnp.sum(e, axis=-1, keepdims=True)
    acc_sc[...] = corr * acc_sc[...] + _dot_nt(e.astype(BF16), vt)
    m_sc[...] = m_new

    @pl.when(j == pl.num_programs(1) - 1)
    def _():
        new = news_ref[...]
        s_new = jnp.sum(qf * new[:, 0:gd], axis=-1, keepdims=True)
        m_old = m_sc[...]
        m_new = jnp.maximum(m_old, s_new)
        corr = jnp.exp(m_old - m_new)
        e = jnp.exp(s_new - m_new)
        l = corr * l_sc[...] + e
        acc = corr * acc_sc[...] + e * new[:, gd:2 * gd]
        o_s = acc / jnp.maximum(l, 1e-30) * own
        gates = jax.nn.sigmoid(sm_ref[...])
        glane = lax.broadcasted_iota(jnp.int32, (NSA_HEADS, _SMALL_W), 1)
        ghead = lax.broadcasted_iota(jnp.int32, (NSA_HEADS, _SMALL_W), 0)
        gate = [jnp.sum(jnp.where(glane == 2 * GDN_HEADS + 3 * ghead + br, gates, 0.0), axis=-1, keepdims=True)
                for br in range(3)]
        o = gate[0] * oc_sc[...] + gate[1] * o_s + gate[2] * ow_sc[...]
        o_ref[...] = o[:, 0:NSA_DH] + o[:, NSA_DH:gd]


def _nsa_decode(qbd, small, cmp_rows, win_t, new_w, new_s, pool, table):
    b = qbd.shape[0]
    pages_per_seq = table.shape[0] // b
    npg = min(_DEC_PAGES_PER_STEP, pages_per_seq)
    n_past_blk = pages_per_seq * (PAGE_ROWS // SLC_BLK)
    n_blk = -(-(n_past_blk + 1) // LANES) * LANES
    ov = _overlap_matrix(cmp_rows.shape[1], n_blk)
    per_seq = lambda bi, j, pt: (bi, 0, 0)
    page_spec = lambda p: pl.BlockSpec(
        (None, KV_ROW, PAGE_ROWS), lambda bi, j, pt: (pt[bi * pages_per_seq + j * npg + p], 0, 0))
    gd = NSA_GROUPS * NSA_DH
    out = pl.pallas_call(
        functools.partial(_nsa_decode_body, n_past_blk=n_past_blk, npg=npg),
        grid_spec=pltpu.PrefetchScalarGridSpec(
            num_scalar_prefetch=1,
            grid=(b, pages_per_seq // npg),
            in_specs=[pl.BlockSpec((None, NSA_HEADS, gd), per_seq),
                      pl.BlockSpec((None, 1, _SMALL_W), per_seq),
                      pl.BlockSpec((None,) + cmp_rows.shape[1:], per_seq),
                      pl.BlockSpec((None,) + win_t.shape[1:], per_seq),
                      pl.BlockSpec((None, 1, KV_ROW), per_seq), pl.BlockSpec((None, 1, KV_ROW), per_seq),
                      pl.BlockSpec(ov.shape, lambda bi, j, pt: (0, 0))]
            + [page_spec(p) for p in range(npg)],
            out_specs=pl.BlockSpec((None, NSA_HEADS, NSA_DH), per_seq),
            scratch_shapes=[pltpu.VMEM((NSA_HEADS, n_blk), F32), pltpu.VMEM((NSA_HEADS, 1), F32),
                            pltpu.VMEM((NSA_HEADS, 1), F32), pltpu.VMEM((NSA_HEADS, gd), F32),
                            pltpu.VMEM((NSA_HEADS, gd), F32), pltpu.VMEM((NSA_HEADS, gd), F32)],
        ),
        out_shape=jax.ShapeDtypeStruct((b, NSA_HEADS, NSA_DH), F32),
        compiler_params=_cparams(("parallel", "arbitrary")),
        name="nsa_decode",
    )(table, qbd, small, cmp_rows, win_t, new_w, new_s, ov, *([pool] * npg))
    return out.reshape(b, NSA_HEADS * NSA_DH)


def _outproj_body(oa_ref, ob_ref, gm_ref, x_ref, gt_ref, wa_ref, wb_ref, wo_ref, lg_ref, lb_ref, o_ref):
    ya = _dot(oa_ref[...].astype(BF16), wa_ref[...])
    yb = _dot(ob_ref[...].astype(BF16), wb_ref[...])
    gm = jax.nn.sigmoid(gm_ref[...])
    u = (gm[:, :D_MODEL] * ya + gm[:, D_MODEL:] * yb).astype(BF16)
    y = _dot(u, wo_ref[...])
    xr = ALPHA * x_ref[...] + gt_ref[...] * y
    o_ref[...] = _ln_rows(xr) * lg_ref[...] + lb_ref[...]


def _outproj(o_a, o_b, g_m, x2d, gt, w_up_a, w_up_b, w_out, ln_g, ln_b, tm, rows_per_mod):
    m = x2d.shape[0]
    row = lambda n: pl.BlockSpec((tm, n), lambda i: (i, 0))
    fixed = lambda shape: pl.BlockSpec(shape, lambda i: (0, 0))
    return pl.pallas_call(
        _outproj_body,
        grid=(m // tm,),
        in_specs=[row(o_a.shape[1]), row(o_b.shape[1]), row(2 * D_MODEL), row(D_MODEL),
                  _mod_spec(gt, tm, rows_per_mod),
                  fixed(w_up_a.shape), fixed(w_up_b.shape), fixed(w_out.shape),
                  fixed((1, D_MODEL)), fixed((1, D_MODEL))],
        out_specs=row(D_MODEL),
        out_shape=jax.ShapeDtypeStruct((m, D_MODEL), F32),
        compiler_params=_cparams(("parallel",)),
        name="mixer_outproj",
    )(o_a, o_b, g_m, x2d, gt, w_up_a, w_up_b, w_out, ln_g, ln_b)


_FF_TILE = 256


def _ffn_body(x_ref, sc_ref, sh_ref, gt_ref, wg_ref, wu_ref, wd_ref, lg_ref, lb_ref, o_ref, h_sc, acc_sc):
    f = pl.program_id(1)

    @pl.when(f == 0)
    def _():
        h = _ln_rows(x_ref[...]) * (1.0 + sc_ref[...]) + sh_ref[...]
        h_sc[...] = h.astype(BF16)
        acc_sc[...] = jnp.zeros(acc_sc.shape, F32)

    hb = h_sc[...]
    hid = _silu(_dot(hb, wg_ref[...])) * _dot(hb, wu_ref[...])
    acc_sc[...] += _dot(hid.astype(BF16), wd_ref[...])

    @pl.when(f == pl.num_programs(1) - 1)
    def _():
        xr = ALPHA * x_ref[...] + gt_ref[...] * acc_sc[...]
        o_ref[...] = _ln_rows(xr) * lg_ref[...] + lb_ref[...]


def _ffn(x2d, sc, sh, gt, w_gu, w_down, ln_g, ln_b, tm, rows_per_mod):
    m = x2d.shape[0]
    tf = _FF_TILE
    n_f = D_FF // tf
    row = pl.BlockSpec((tm, D_MODEL), lambda i, f: (i, 0))
    fixed = lambda shape: pl.BlockSpec(shape, lambda i, f: (0, 0))
    mod = lambda a: _mod_spec(a, tm, rows_per_mod)
    return pl.pallas_call(
        _ffn_body,
        grid=(m // tm, n_f),
        in_specs=[row, mod(sc), mod(sh), mod(gt),
                  pl.BlockSpec((D_MODEL, tf), lambda i, f: (0, f)),
                  pl.BlockSpec((D_MODEL, tf), lambda i, f: (0, f + n_f)),
                  pl.BlockSpec((tf, D_MODEL), lambda i, f: (f, 0)),
                  fixed((1, D_MODEL)), fixed((1, D_MODEL))],
        out_specs=row,
        out_shape=jax.ShapeDtypeStruct((m, D_MODEL), F32),
        scratch_shapes=[pltpu.VMEM((tm, D_MODEL), BF16), pltpu.VMEM((tm, D_MODEL), F32)],
        compiler_params=_cparams(("parallel", "arbitrary")),
        name="ffn_dense",
    )(x2d, sc, sh, gt, w_gu, w_gu, w_down, ln_g, ln_b)


_MOE_CHUNK = 256
_MOE_FF_TILE = 1408


def _moe_body(x_ref, sc_ref, sh_ref, gt_ref, wr_ref, wg_ref, wu_ref, wd_ref, lg_ref, lb_ref, o_ref,
              h_sc, w_sc, sel_sc, rank_sc, selt_sc, rankt_sc, xg_sc, yg_sc, y_sc, *, chunk):
    e = pl.program_id(1)
    f = pl.program_id(2)
    tb = x_ref.shape[0]
    lane = lax.broadcasted_iota(jnp.int32, (1, LANES), 1)

    @pl.when((e == 0) & (f == 0))
    def _():
        h = _ln_rows(x_ref[...]) * (1.0 + sc_ref[...]) + sh_ref[...]
        h_sc[...] = h.astype(BF16)
        logits = jnp.where(lane < N_EXP, _dot(h, wr_ref[...], HIGHEST), NEG)
        ex = jnp.exp(logits - jnp.max(logits, axis=-1, keepdims=True))
        probs = ex / jnp.sum(ex, axis=-1, keepdims=True)
        lane_f = lane.astype(F32)
        p1 = jnp.max(probs, axis=-1, keepdims=True)
        i1 = jnp.min(jnp.where(probs == p1, lane_f, float(LANES)), axis=-1, keepdims=True)
        rest = jnp.where(lane_f == i1, -1.0, probs)
        p2 = jnp.max(rest, axis=-1, keepdims=True)
        i2 = jnp.min(jnp.where(rest == p2, lane_f, float(LANES)), axis=-1, keepdims=True)
        w_sc[...] = (jnp.where(lane_f == i1, p1, 0.0) + jnp.where(lane_f == i2, p2, 0.0)) / (p1 + p2)
        sel = ((lane_f == i1) | (lane_f == i2)).astype(F32)
        earlier = (lax.broadcasted_iota(jnp.int32, (tb, tb), 1)
                   < lax.broadcasted_iota(jnp.int32, (tb, tb), 0)).astype(BF16)
        rank = _dot(earlier, sel.astype(BF16))
        sel_sc[...] = sel
        rank_sc[...] = rank
        selt_sc[...] = sel.T
        rankt_sc[...] = rank.T
        y_sc[...] = jnp.zeros(y_sc.shape, F32)

    sel_row = selt_sc[pl.ds(e, 1), :]
    rank_row = rankt_sc[pl.ds(e, 1), :]
    count = jnp.sum(sel_row).astype(jnp.int32)
    n_chunks = (count + chunk - 1) // chunk

    @pl.when(f == 0)
    def _():
        def gather(c, carry):
            slot = (c * chunk + lax.broadcasted_iota(jnp.int32, (chunk, 1), 0)).astype(F32)
            pick = ((rank_row == slot) & (sel_row > 0.5)).astype(BF16)
            xg_sc[pl.ds(pl.multiple_of(c * chunk, chunk), chunk), :] = _dot(pick, h_sc[...]).astype(BF16)
            return carry

        lax.fori_loop(0, n_chunks, gather, 0)

    def expert(c, carry):
        rows = pl.ds(pl.multiple_of(c * chunk, chunk), chunk)
        xg = xg_sc[rows, :]
        hid = _silu(_dot(xg, wg_ref[...])) * _dot(xg, wu_ref[...])
        part = _dot(hid.astype(BF16), wd_ref[...])

        @pl.when(f == 0)
        def _():
            yg_sc[rows, :] = part

        @pl.when(f > 0)
        def _():
            yg_sc[rows, :] += part

        return carry

    lax.fori_loop(0, n_chunks, expert, 0)

    @pl.when(f == pl.num_programs(2) - 1)
    def _():
        pick_e = lane == e
        sel_col = jnp.sum(jnp.where(pick_e, sel_sc[...], 0.0), axis=-1, keepdims=True)
        rank_col = jnp.sum(jnp.where(pick_e, rank_sc[...], 0.0), axis=-1, keepdims=True)
        w_col = jnp.sum(jnp.where(pick_e, w_sc[...], 0.0), axis=-1, keepdims=True)

        def scatter(c, carry):
            slot = (c * chunk + lax.broadcasted_iota(jnp.int32, (1, chunk), 1)).astype(F32)
            place = ((rank_col == slot) & (sel_col > 0.5)).astype(BF16)
            rows = pl.ds(pl.multiple_of(c * chunk, chunk), chunk)
            y_sc[...] += w_col * _dot(place, yg_sc[rows, :].astype(BF16))
            return carry

        lax.fori_loop(0, n_chunks, scatter, 0)

    @pl.when((e == pl.num_programs(1) - 1) & (f == pl.num_programs(2) - 1))
    def _():
        xr = ALPHA * x_ref[...] + gt_ref[...] * y_sc[...]
        o_ref[...] = _ln_rows(xr) * lg_ref[...] + lb_ref[...]


def _moe(x2d, sc, sh, gt, w_router, w_gu, w_down, ln_g, ln_b, tm, rows_per_mod):
    m = x2d.shape[0]
    n_e = w_gu.shape[0]
    tf = _MOE_FF_TILE
    n_f = D_FF // tf
    chunk = min(_MOE_CHUNK, tm)
    row = pl.BlockSpec((tm, D_MODEL), lambda i, e, f: (i, 0), pipeline_mode=pl.Buffered(1))
    fixed = lambda shape: pl.BlockSpec(shape, lambda i, e, f: (0, 0))
    mod = lambda a: _mod_spec(a, tm, rows_per_mod)
    per_tok = lambda: pltpu.VMEM((tm, LANES), F32)
    per_exp = lambda: pltpu.VMEM((LANES, tm), F32)
    return pl.pallas_call(
        functools.partial(_moe_body, chunk=chunk),
        grid=(m // tm, n_e, n_f),
        in_specs=[row, mod(sc), mod(sh), mod(gt), fixed(w_router.shape),
                  pl.BlockSpec((None, D_MODEL, tf), lambda i, e, f: (e, 0, f)),
                  pl.BlockSpec((None, D_MODEL, tf), lambda i, e, f: (e, 0, f + n_f)),
                  pl.BlockSpec((None, tf, D_MODEL), lambda i, e, f: (e, f, 0)),
                  fixed((1, D_MODEL)), fixed((1, D_MODEL))],
        out_specs=row,
        out_shape=jax.ShapeDtypeStruct((m, D_MODEL), F32),
        scratch_shapes=[pltpu.VMEM((tm, D_MODEL), BF16), per_tok(), per_tok(), per_tok(), per_exp(), per_exp(),
                        pltpu.VMEM((tm, D_MODEL), BF16), pltpu.VMEM((tm, D_MODEL), F32),
                        pltpu.VMEM((tm, D_MODEL), F32)],
        compiler_params=_cparams(("parallel", "arbitrary", "arbitrary")),
        name="moe_routed",
    )(x2d, sc, sh, gt, w_router, w_gu, w_gu, w_down, ln_g, ln_b)


def _permute_w_in(w):
    sizes = [C_QKV, GDN_HEADS * GDN_DV, GDN_HEADS, GDN_HEADS, NSA_HEADS * NSA_DH, KV_ROW, KV_ROW, KV_ROW,
             3 * NSA_HEADS, 2 * D_MODEL]
    offs = np.cumsum([0] + sizes)
    qkv, z, b_a, a_a, q_b, kv_c, kv_s, kv_w, g_b, g_m = (w[:, offs[i]:offs[i + 1]] for i in range(len(sizes)))
    pad = jnp.zeros((w.shape[0], _SMALL_W - 2 * GDN_HEADS - 3 * NSA_HEADS), w.dtype)
    return jnp.concatenate([qkv, z, g_m, q_b, kv_c, kv_s, kv_w, b_a, a_a, g_b, pad], axis=1).astype(BF16)


def _layer_weights(l, w_in, conv_w, a_log, dt_bias, norm_w_a, cmp_pos, cmp_w1, cmp_w2, w_up_a, w_up_b, w_out,
                   ln_g, ln_b):
    lane_pad = lambda v: jnp.zeros((1, _SMALL_W), F32).at[0, GDN_HEADS:2 * GDN_HEADS].set(v)
    pe, w1bd, w2bd = _compress_weights(cmp_pos[l], cmp_w1[l], cmp_w2[l])
    return dict(
        w_in=_permute_w_in(w_in[l]), conv_w=conv_w[l], alog=lane_pad(a_log[l]), dt=lane_pad(dt_bias[l]),
        norm_w=norm_w_a[l].reshape(1, GDN_DV), pe=pe, w1bd=w1bd, w2bd=w2bd,
        w_up_a=w_up_a[l].astype(BF16), w_up_b=w_up_b[l].astype(BF16), w_out=w_out[l].astype(BF16),
        ln_g0=ln_g[l, 0].reshape(1, D_MODEL), ln_b0=ln_b[l, 0].reshape(1, D_MODEL),
        ln_g1=ln_g[l, 1].reshape(1, D_MODEL), ln_b1=ln_b[l, 1].reshape(1, D_MODEL))


def _mixer_common(x2d, mods, wts, b, t, t_pad, conv_buf, s0, tm, rows_per_mod):
    sh_m, sc_m = mods[0], mods[1]
    qkv, z, g_m, q_b, kv_c, kv_s, kv_w, small = _inproj(x2d, sc_m, sh_m, wts['w_in'], tm, rows_per_mod)
    seq = lambda a: a.reshape(b, t, a.shape[-1])
    padt = lambda a: jnp.pad(seq(a), ((0, 0), (0, t_pad - t), (0, 0)))
    buf8 = jnp.pad(conv_buf, ((0, 0), (SUBLANES - (CONV_TAPS - 1), 0), (0, 0)))
    qa, ka, va, bg = _gdn_prep(padt(qkv), buf8, wts['conv_w'], padt(small), wts['alog'], wts['dt'], t)
    o_a, s_new = _gdn(qa, ka, va, bg, padt(z), s0, wts['norm_w'])
    o_a = o_a[:, :t].reshape(b * t, GDN_HEADS * GDN_DV)
    conv_new = jnp.concatenate([conv_buf, seq(qkv)], axis=1)[:, -(CONV_TAPS - 1):]
    return (g_m, q_b, kv_c, kv_s, kv_w, small), o_a, s_new, conv_new


def _rows5(a, b, t):
    return a.reshape(b, t, 2, NSA_GROUPS, NSA_DH)


def _feature_major(rows):
    nd = rows.ndim
    return jnp.transpose(rows, tuple(range(nd - 4)) + (nd - 3, nd - 2, nd - 1, nd - 4))


def _prompt_layer(x, mods, wts, ffn_args, routed):
    b, t, _ = x.shape
    x2d = x.reshape(b * t, D_MODEL)
    tm = 256
    zeros_buf = jnp.zeros((b, CONV_TAPS - 1, C_QKV), F32)
    zeros_s = jnp.zeros((b, GDN_HEADS, GDN_DK, GDN_DV), F32)
    (g_m, q_b, kv_c, kv_s, kv_w, small), o_a, s_new, conv_new = _mixer_common(
        x2d, mods, wts, b, t, t, zeros_buf, zeros_s, tm, t)
    pages = (b * t) // PAGE_ROWS
    cmp_rows = _compress(kv_c.reshape(pages, PAGE_ROWS, KV_ROW), jnp.arange(pages, dtype=jnp.int32), b,
                         wts['pe'], wts['w1bd'], wts['w2bd'], False)
    seq = lambda a: a.reshape(b, t, a.shape[-1])
    o_b = _nsa_prompt(seq(q_b), seq(small), cmp_rows, seq(kv_s), seq(kv_w))
    x1 = _outproj(o_a, o_b.reshape(b * t, -1), g_m, x2d, mods[2], wts['w_up_a'], wts['w_up_b'], wts['w_out'],
                  wts['ln_g0'], wts['ln_b0'], 512, t)
    x2 = (_moe if routed else _ffn)(x1, mods[4], mods[3], mods[5], *ffn_args, wts['ln_g1'], wts['ln_b1'],
                                    min(1024, t), t)
    win_new = _rows5(kv_w, b, t)[:, t - min(WINDOW, t):]
    return x2.reshape(b, t, D_MODEL), (s_new, conv_new, _rows5(kv_c, b, t), _rows5(kv_s, b, t), win_new)


def _sample_layer(x, mods, wts, ffn_args, routed, conv_buf, s0, pool_c, pool_s, win_buf, table):
    b = x.shape[0]
    x2d = x.reshape(b, D_MODEL)
    (g_m, q_b, kv_c, kv_s, kv_w, small), o_a, s_new, conv_new = _mixer_common(
        x2d, mods, wts, b, 1, _GDN_SHORT_CHUNK, conv_buf, s0, b, b)
    cmp_rows = _compress(pool_c, table, b, wts['pe'], wts['w1bd'], wts['w2bd'], True)
    win_new = jnp.concatenate([win_buf.reshape(b, -1, KV_ROW)[:, 1:], kv_w[:, None, :]], axis=1)
    win_t = _feature_major(win_buf).reshape(b, KV_ROW, -1)
    q4 = q_b.reshape(b, NSA_GROUPS, NSA_REP, 1, NSA_DH)
    eye = jnp.eye(NSA_GROUPS, dtype=F32).reshape(NSA_GROUPS, 1, NSA_GROUPS, 1)
    qbd = (q4 * eye).reshape(b, NSA_HEADS, NSA_GROUPS * NSA_DH)
    o_b = _nsa_decode(qbd, small.reshape(b, 1, _SMALL_W), cmp_rows, win_t, kv_w.reshape(b, 1, KV_ROW),
                      kv_s.reshape(b, 1, KV_ROW), pool_s, table)
    x1 = _outproj(o_a, o_b, g_m, x2d, mods[2], wts['w_up_a'], wts['w_up_b'], wts['w_out'],
                  wts['ln_g0'], wts['ln_b0'], b, b)
    x2 = (_moe if routed else _ffn)(x1, mods[4], mods[3], mods[5], *ffn_args, wts['ln_g1'], wts['ln_b1'], b, b)
    return x2.reshape(b, 1, D_MODEL), (s_new, conv_new, _rows5(kv_c, b, 1), _rows5(kv_s, b, 1),
                                       _rows5(win_new, b, win_new.shape[1]))


def kernel(x_prompt, x_sample, state_delta, state_conv, cache_cmp_kv, cache_slc_kv, state_win_kv, page_table,
           c_prompt, c_sample, w_ada, b_ada, w_in, conv_w, a_log, dt_bias, norm_w_a, cmp_pos, cmp_w1, cmp_w2,
           w_up_a, w_up_b, w_out, ln_g, ln_b, ffn_w_gu, ffn_w_down, moe_router, moe_w_gu, moe_w_down):
    bp = x_prompt.shape[0]
    db = x_sample.shape[0]
    n_layers = w_in.shape[0]
    n_pool = cache_cmp_kv.shape[1]
    mod_all = _adaln(jnp.concatenate([c_prompt, c_sample], axis=0), w_ada, b_ada)
    table = page_table.reshape(-1)
    pool_c = _feature_major(cache_cmp_kv).reshape(n_layers * n_pool, KV_ROW, PAGE_ROWS)
    pool_s = _feature_major(cache_slc_kv).reshape(n_layers * n_pool, KV_ROW, PAGE_ROWS)
    xp, xs = x_prompt, x_sample
    st_p, st_s = [], []
    for l in range(n_layers):
        wts = _layer_weights(l, w_in, conv_w, a_log, dt_bias, norm_w_a, cmp_pos, cmp_w1, cmp_w2, w_up_a, w_up_b,
                             w_out, ln_g, ln_b)
        routed = l % 2 == 1
        if routed:
            router = jnp.pad(moe_router[l // 2], ((0, 0), (0, LANES - N_EXP)))
            ffn_args = (router, moe_w_gu[l // 2].astype(BF16), moe_w_down[l // 2].astype(BF16))
        else:
            ffn_args = (ffn_w_gu[l // 2].astype(BF16), ffn_w_down[l // 2].astype(BF16))
        mod6 = mod_all[l].reshape(bp + db, 6, D_MODEL)
        mods_p = [mod6[:bp, i].reshape(bp, 1, D_MODEL) for i in range(6)]
        mods_s = [mod6[bp:, i].reshape(1, db, D_MODEL) for i in range(6)]
        xp, sp = _prompt_layer(xp, mods_p, wts, ffn_args, routed)
        xs, ss = _sample_layer(xs, mods_s, wts, ffn_args, routed, state_conv[l], state_delta[l],
                               pool_c, pool_s, state_win_kv[l], table + l * n_pool)
        st_p.append(sp)
        st_s.append(ss)
    stack = lambda sts, i: jnp.stack([s[i] for s in sts])
    return (xp, xs) + tuple(stack(st_p, i) for i in range(5)) + tuple(stack(st_s, i) for i in range(5))
```

```python
import functools
import math

import jax
import jax.numpy as jnp
import numpy as np
from jax import lax
from jax.experimental import pallas as pl
from jax.experimental.pallas import tpu as pltpu

F32 = jnp.float32
BF16 = jnp.bfloat16
HIGHEST = lax.Precision.HIGHEST

D_MODEL = 1024
N_LAYERS = 2
PAGE_ROWS = 128
GDN_HEADS = 4
GDN_DK = 128
GDN_DV = 128
CONV_TAPS = 4
GDN_CHUNK = 64
C_QKV = 2 * GDN_HEADS * GDN_DK + GDN_HEADS * GDN_DV
NSA_HEADS = 8
NSA_GROUPS = 2
NSA_REP = NSA_HEADS // NSA_GROUPS
NSA_DH = 64
KV_ROW = 2 * NSA_GROUPS * NSA_DH
CMP_LEN = 32
CMP_STRIDE = 16
CMP_HID = 256
SLC_BLK = 64
N_SEL = 16
WINDOW = 512
Q_BLK = 128
D_FF = 2816
N_EXP = 8
ALPHA = (2 * N_LAYERS) ** 0.25
EPS = 1e-5
NEG = -1e30

LANES = 128
SUBLANES = 8
VMEM_LIMIT = 56 * 1024 * 1024

_SMALL_W = LANES
IN_OUT_WIDTHS = (C_QKV, GDN_HEADS * GDN_DV, 2 * D_MODEL, NSA_HEADS * NSA_DH, KV_ROW, KV_ROW, KV_ROW, _SMALL_W)


def _cparams(sem):
    return pltpu.CompilerParams(dimension_semantics=sem, vmem_limit_bytes=VMEM_LIMIT)


def _silu(x):
    return x * jax.nn.sigmoid(x)


def _ln_rows(x):
    mu = jnp.mean(x, axis=-1, keepdims=True)
    xc = x - mu
    var = jnp.mean(xc * xc, axis=-1, keepdims=True)
    return xc * lax.rsqrt(var + EPS)


def _dot(a, b, precision=None):
    return jnp.dot(a, b, preferred_element_type=F32, precision=precision)


def _dot_nt(a, b, precision=None):
    return lax.dot_general(a, b, (((1,), (1,)), ((), ())), preferred_element_type=F32, precision=precision)


def _dot_tn(a, b, precision=None):
    return lax.dot_general(a, b, (((0,), (0,)), ((), ())), preferred_element_type=F32, precision=precision)


def _masked_softmax(s, mask, axis=-1, exp=jnp.exp):
    s = jnp.where(mask, s, NEG)
    m = jnp.max(s, axis=axis, keepdims=True)
    e = jnp.where(mask, exp(s - m), 0.0)
    return e * (1.0 / jnp.maximum(jnp.sum(e, axis=axis, keepdims=True), 1e-30))


def _select_top(imp, lane_idx, n_lanes, axis=-1):
    sel = jnp.zeros(imp.shape, F32)
    lane_idx = lane_idx.astype(F32)
    for _ in range(N_SEL):
        mx = jnp.max(imp, axis=axis, keepdims=True)
        idx = jnp.min(jnp.where(imp == mx, lane_idx, float(n_lanes)), axis=axis, keepdims=True)
        hit = lane_idx == idx
        sel = jnp.where(hit, 1.0, sel)
        imp = jnp.where(hit, -3e38, imp)
    return sel


def _adaln_body(c_ref, w_ref, b_ref, o_ref):
    s = _silu(c_ref[...]).astype(BF16)
    o_ref[...] = _dot(s, w_ref[...].astype(BF16)) + b_ref[...]


def _adaln(c_all, w_ada, b_ada):
    n_l, d, n = w_ada.shape
    r = c_all.shape[0]
    tn = 1536
    return pl.pallas_call(
        _adaln_body,
        grid=(n_l, n // tn),
        in_specs=[pl.BlockSpec((r, d), lambda l, j: (0, 0)),
                  pl.BlockSpec((None, d, tn), lambda l, j: (l, 0, j)),
                  pl.BlockSpec((None, 1, tn), lambda l, j: (l, 0, j))],
        out_specs=pl.BlockSpec((None, r, tn), lambda l, j: (l, 0, j)),
        out_shape=jax.ShapeDtypeStruct((n_l, r, n), F32),
        compiler_params=_cparams(("parallel", "parallel")),
        name="adaln_mod",
    )(c_all, w_ada, b_ada.reshape(n_l, 1, n))


def _inproj_body(x_ref, sc_ref, sh_ref, w_ref, *o_refs):
    h = (_ln_rows(x_ref[...]) * (1.0 + sc_ref[...]) + sh_ref[...]).astype(BF16)
    off = 0
    for o_ref, n in zip(o_refs, IN_OUT_WIDTHS):
        o_ref[...] = _dot(h, w_ref[:, off:off + n])
        off += n


def _mod_spec(mod, tm, rows_per_mod):
    if mod.shape[1] == 1:
        return pl.BlockSpec((None, 1, D_MODEL), lambda i, *_: ((i * tm) // rows_per_mod, 0, 0))
    return pl.BlockSpec((None, tm, D_MODEL), lambda i, *_: (0, i, 0))


def _inproj(x2d, sc, sh, w_perm, tm, rows_per_mod):
    m = x2d.shape[0]
    n_tot = w_perm.shape[1]
    return pl.pallas_call(
        _inproj_body,
        grid=(m // tm,),
        in_specs=[pl.BlockSpec((tm, D_MODEL), lambda i: (i, 0)),
                  _mod_spec(sc, tm, rows_per_mod), _mod_spec(sh, tm, rows_per_mod),
                  pl.BlockSpec((D_MODEL, n_tot), lambda i: (0, 0))],
        out_specs=[pl.BlockSpec((tm, n), lambda i: (i, 0)) for n in IN_OUT_WIDTHS],
        out_shape=[jax.ShapeDtypeStruct((m, n), F32) for n in IN_OUT_WIDTHS],
        compiler_params=_cparams(("parallel",)),
        name="ln_inproj",
    )(x2d, sc, sh, w_perm)


def _prep_body(qkv_ref, buf_ref, cw_ref, sm_ref, alog_ref, dt_ref, q_ref, k_ref, v_ref, bg_ref, xp_sc,
               *, tb, t_valid):
    i = pl.program_id(1)

    @pl.when(i == 0)
    def _():
        xp_sc[0:SUBLANES, :] = buf_ref[...]

    x = qkv_ref[...]
    xp_sc[SUBLANES:SUBLANES + tb, :] = x
    y = cw_ref[3:4, :] * x
    for j in range(CONV_TAPS - 1):
        lo = SUBLANES - (CONV_TAPS - 1) + j
        y = y + cw_ref[j:j + 1, :] * xp_sc[lo:lo + tb, :]
    xp_sc[0:SUBLANES, :] = xp_sc[tb:tb + SUBLANES, :]
    a = _silu(y)
    rows = i * tb + lax.broadcasted_iota(jnp.int32, (tb, 1), 0)
    live = rows < t_valid
    hk = GDN_HEADS * GDN_DK
    for h in range(GDN_HEADS):
        qh = a[:, h * GDN_DK:(h + 1) * GDN_DK]
        kh = a[:, hk + h * GDN_DK:hk + (h + 1) * GDN_DK]
        qn = qh * lax.rsqrt(jnp.sum(qh * qh, axis=-1, keepdims=True) + 1e-6) * (GDN_DK ** -0.5)
        kn = kh * lax.rsqrt(jnp.sum(kh * kh, axis=-1, keepdims=True) + 1e-6)
        q_ref[:, h * GDN_DK:(h + 1) * GDN_DK] = jnp.where(live, qn, 0.0)
        k_ref[:, h * GDN_DK:(h + 1) * GDN_DK] = jnp.where(live, kn, 0.0)
    v_ref[...] = jnp.where(live, a[:, 2 * hk:], 0.0)
    sm = sm_ref[...]
    beta = jax.nn.sigmoid(sm)
    z = sm + dt_ref[...]
    softplus = jnp.maximum(z, 0.0) + jnp.log(1.0 + jnp.exp(-jnp.abs(z)))
    g = -jnp.exp(alog_ref[...]) * softplus
    lane = lax.broadcasted_iota(jnp.int32, sm.shape, 1)
    bg_ref[...] = jnp.where(live, jnp.where(lane < GDN_HEADS, beta, g), 0.0)


def _gdn_prep(qkv, buf8, conv_w, small, alog_vec, dt_vec, t_valid):
    b, t, _ = qkv.shape
    tb = min(t, 512)
    hd = GDN_HEADS * GDN_DK
    row = lambda bi, i: (bi, i, 0)
    fixed = lambda bi, i: (0, 0)
    return pl.pallas_call(
        functools.partial(_prep_body, tb=tb, t_valid=t_valid),
        grid=(b, t // tb),
        in_specs=[pl.BlockSpec((None, tb, C_QKV), row),
                  pl.BlockSpec((None, SUBLANES, C_QKV), lambda bi, i: (bi, 0, 0)),
                  pl.BlockSpec((CONV_TAPS, C_QKV), fixed),
                  pl.BlockSpec((None, tb, _SMALL_W), row),
                  pl.BlockSpec((1, _SMALL_W), fixed), pl.BlockSpec((1, _SMALL_W), fixed)],
        out_specs=[pl.BlockSpec((None, tb, hd), row), pl.BlockSpec((None, tb, hd), row),
                   pl.BlockSpec((None, tb, hd), row), pl.BlockSpec((None, tb, _SMALL_W), row)],
        out_shape=[jax.ShapeDtypeStruct((b, t, hd), F32)] * 3 + [jax.ShapeDtypeStruct((b, t, _SMALL_W), F32)],
        scratch_shapes=[pltpu.VMEM((tb + SUBLANES, C_QKV), F32)],
        compiler_params=_cparams(("parallel", "arbitrary")),
        name="gdn_prep",
    )(qkv, buf8, conv_w, small, alog_vec, dt_vec)


def _split_bf16(x):
    hi = x.astype(BF16)
    return hi, (x - hi.astype(F32)).astype(BF16)


def _dot3(a, b, dims):
    ah, al = _split_bf16(a)
    bh, bl = _split_bf16(b)
    d = lambda x, y: lax.dot_general(x, y, dims, preferred_element_type=F32)
    return d(ah, bh) + (d(ah, bl) + d(al, bh))


def _bmm(a, b):
    return _dot3(a, b, (((2,), (1,)), ((0,), (0,))))


def _bmm_nt(a, b):
    return _dot3(a, b, (((2,), (2,)), ((0,), (0,))))


def _gdn_ut_body(q_ref, k_ref, v_ref, bg_ref, u_ref, w_ref, qd_ref, kd_ref, qk_ref, gl_ref, *, n_chunks, c_len):
    nc = n_chunks
    tb = nc * c_len
    row = lax.broadcasted_iota(jnp.int32, (nc, c_len, c_len), 1)
    col = lax.broadcasted_iota(jnp.int32, (nc, c_len, c_len), 2)
    incl = row >= col
    strict = row > col
    incl_f = incl.astype(F32)
    strict_f = strict.astype(F32)
    eye = (row == col).astype(F32)
    bg = bg_ref[...]
    lane = lax.broadcasted_iota(jnp.int32, (tb, _SMALL_W), 1)
    gl_all = jnp.zeros((tb, _SMALL_W), F32)
    split = lambda x: x.reshape(nc, c_len, x.shape[-1])
    for h in range(GDN_HEADS):
        sl = slice(h * GDN_DK, (h + 1) * GDN_DK)
        q = split(q_ref[:, sl])
        k = split(k_ref[:, sl])
        v = split(v_ref[:, sl])
        beta = split(bg[:, h:h + 1])
        g = split(bg[:, GDN_HEADS + h:GDN_HEADS + h + 1])
        dlog = lax.dot_general(incl_f, g * strict_f, (((2,), (1,)), ((0,), (0,))), preferred_element_type=F32,
                               precision=HIGHEST)
        gc = dlog[:, :, 0:1] + g[:, 0:1, :]
        decay = jnp.where(incl, jnp.exp(dlog), 0.0)
        egc = jnp.exp(gc)
        g_last = gc[:, c_len - 1:c_len, :]
        kb = k * beta
        a = jnp.where(strict, _bmm_nt(kb, k) * decay, 0.0)
        p = -a
        tinv = eye + p
        for _ in range(int(math.log2(c_len)) - 1):
            p = _bmm(p, p)
            tinv = tinv + _bmm(tinv, p)
        u_ref[:, sl] = _bmm(tinv, v * beta).reshape(tb, GDN_DV)
        w_ref[:, sl] = _bmm(tinv, kb * egc).reshape(tb, GDN_DK)
        qk = jnp.where(incl, _bmm_nt(q, k) * decay, 0.0)
        qk_ref[:, h * c_len:(h + 1) * c_len] = qk.reshape(tb, c_len)
        qd_ref[:, sl] = (q * egc).reshape(tb, GDN_DK)
        kd_ref[:, sl] = (k * jnp.exp(g_last - gc)).reshape(tb, GDN_DK)
        gl = jnp.broadcast_to(jnp.exp(g_last), (nc, c_len, 1)).reshape(tb, 1)
        gl_all = jnp.where(lane == h, gl, gl_all)
    gl_ref[...] = gl_all


def _gdn_scan_body(u_ref, w_ref, qd_ref, kd_ref, qk_ref, gl_ref, z_ref, s0_ref, nw_ref, o_ref, sfin_ref, s_sc,
                   *, n_chunks, n_seq, c_len):
    i = pl.program_id(1)

    @pl.when(i == 0)
    def _():
        s_sc[...] = s0_ref[...]

    nw = nw_ref[...]

    def chunk(c, carry):
        r0 = pl.multiple_of(c * c_len, c_len)
        rows = pl.ds(r0, c_len)
        gl_row = gl_ref[:, pl.ds(r0, 1), :]
        for h in range(GDN_HEADS):
            sl = slice(h * GDN_DK, (h + 1) * GDN_DK)
            s = s_sc[:, h]
            v_new = u_ref[:, rows, sl] - _bmm(w_ref[:, rows, sl], s)
            o = _bmm(qd_ref[:, rows, sl], s) + _bmm(qk_ref[:, rows, h * c_len:(h + 1) * c_len], v_new)
            kd = kd_ref[:, rows, sl]
            upd = jnp.stack([_dot3(kd[b], v_new[b], (((0,), (0,)), ((), ()))) for b in range(n_seq)])
            s_sc[:, h] = s * gl_row[:, :, h:h + 1] + upd
            z = z_ref[:, rows, sl]
            on = o * lax.rsqrt(jnp.mean(o * o, axis=-1, keepdims=True) + EPS) * nw
            o_ref[:, rows, sl] = on * _silu(z)
        return carry

    lax.fori_loop(0, n_chunks, chunk, 0)

    @pl.when(i == pl.num_programs(1) - 1)
    def _():
        sfin_ref[...] = s_sc[...]


_GDN_SEQ_PER_STEP = 8
_GDN_SHORT_CHUNK = 16


def _gdn(q, k, v, bg, z, s0, norm_w):
    b, t, hd = q.shape
    tb = min(t, 512)
    c_len = min(GDN_CHUNK, t)
    nc = tb // c_len
    row = lambda bi, i: (bi, i, 0)
    wide = lambda n: pl.BlockSpec((None, tb, n), row)
    qkw = GDN_HEADS * c_len
    u, w, qd, kd, qk, gl = pl.pallas_call(
        functools.partial(_gdn_ut_body, n_chunks=nc, c_len=c_len),
        grid=(b, t // tb),
        in_specs=[wide(hd)] * 3 + [wide(_SMALL_W)],
        out_specs=[wide(hd)] * 4 + [wide(qkw), wide(_SMALL_W)],
        out_shape=[jax.ShapeDtypeStruct((b, t, hd), F32)] * 4
        + [jax.ShapeDtypeStruct((b, t, qkw), F32), jax.ShapeDtypeStruct((b, t, _SMALL_W), F32)],
        compiler_params=_cparams(("parallel", "parallel")),
        name="gdn_ut",
    )(q, k, v, bg)
    nb = min(b, _GDN_SEQ_PER_STEP)
    grp = lambda gi, i: (gi, i, 0)
    st = lambda gi, i: (gi, 0, 0, 0)
    seqs = lambda n: pl.BlockSpec((nb, tb, n), grp)
    return pl.pallas_call(
        functools.partial(_gdn_scan_body, n_chunks=nc, n_seq=nb, c_len=c_len),
        grid=(b // nb, t // tb),
        in_specs=[seqs(hd)] * 4 + [seqs(qkw), seqs(_SMALL_W), seqs(hd),
                                   pl.BlockSpec((nb, GDN_HEADS, GDN_DK, GDN_DV), st),
                                   pl.BlockSpec((1, GDN_DV), lambda gi, i: (0, 0))],
        out_specs=[seqs(hd), pl.BlockSpec((nb, GDN_HEADS, GDN_DK, GDN_DV), st)],
        out_shape=[jax.ShapeDtypeStruct((b, t, hd), F32),
                   jax.ShapeDtypeStruct((b, GDN_HEADS, GDN_DK, GDN_DV), F32)],
        scratch_shapes=[pltpu.VMEM((nb, GDN_HEADS, GDN_DK, GDN_DV), F32)],
        compiler_params=_cparams(("parallel", "arbitrary")),
        name="gdn_scan",
    )(u, w, qd, kd, qk, gl, z, s0, norm_w)


_SEG_PER_PAGE = PAGE_ROWS // CMP_STRIDE
_CMP_PAGES_PER_STEP = 32


def _compress_body(pt_ref, *refs, pages_per_seq, npg, feature_major):
    page_refs = refs[:npg]
    pe_ref, w1_ref, w2_ref, o_ref, carry_sc, rows_sc = refs[npg:]
    step = pl.program_id(0)
    first = (step % (pages_per_seq // npg)) == 0
    for p, r in enumerate(page_refs):
        for kv in range(2):
            half = slice(kv * LANES, (kv + 1) * LANES)
            rows_sc[kv, p * PAGE_ROWS:(p + 1) * PAGE_ROWS, :] = r[half, :].T if feature_major else r[:, half]
    n_rows = npg * _SEG_PER_PAGE
    n_m = CMP_LEN // CMP_STRIDE
    gh = NSA_GROUPS * CMP_HID
    acc = [[jnp.zeros((n_rows, gh), F32) for _ in range(2)] for _ in range(n_m)]
    for sp in range(CMP_STRIDE // 2):
        for kv in range(2):
            x = jnp.concatenate([rows_sc[kv, pl.ds(2 * sp + i, n_rows, stride=CMP_STRIDE), :] for i in range(2)],
                                axis=1)
            for m in range(n_m):
                xm = (x + pe_ref[m, sp, kv:kv + 1, :]).astype(BF16)
                acc[m][kv] = acc[m][kv] + _dot(xm, w1_ref[m, sp, kv])
    part = [jnp.concatenate(a, axis=1) for a in acc]
    prev = jnp.where(first, 0.0, carry_sc[...])
    rows = lax.broadcasted_iota(jnp.int32, (n_rows, 1), 0)
    shifted = jnp.where(rows == 0, prev, pltpu.roll(part[0], 1, 0))
    carry_sc[...] = part[0][n_rows - 1:n_rows, :]
    hid = _silu(shifted + part[1]).astype(BF16)
    for kv in range(2):
        o_ref[:, kv * LANES:(kv + 1) * LANES] = _dot(hid[:, kv * gh:(kv + 1) * gh], w2_ref[kv]).astype(BF16)


def _compress(pool, table, n_seq, pe, w1bd, w2bd, feature_major):
    pages_per_seq = table.shape[0] // n_seq
    npg = min(_CMP_PAGES_PER_STEP, pages_per_seq)
    n_steps = table.shape[0] // npg
    page_spec = lambda p: pl.BlockSpec((None,) + pool.shape[1:], lambda i, pt: (pt[i * npg + p], 0, 0))
    fixed = lambda nd: (lambda i, pt: (0,) * nd)
    out = pl.pallas_call(
        functools.partial(_compress_body, pages_per_seq=pages_per_seq, npg=npg, feature_major=feature_major),
        grid_spec=pltpu.PrefetchScalarGridSpec(
            num_scalar_prefetch=1,
            grid=(n_steps,),
            in_specs=[page_spec(p) for p in range(npg)]
            + [pl.BlockSpec(pe.shape, fixed(4)), pl.BlockSpec(w1bd.shape, fixed(5)),
               pl.BlockSpec(w2bd.shape, fixed(3))],
            out_specs=pl.BlockSpec((npg * _SEG_PER_PAGE, KV_ROW), lambda i, pt: (i, 0)),
            scratch_shapes=[pltpu.VMEM((1, 2 * NSA_GROUPS * CMP_HID), F32),
                            pltpu.VMEM((2, npg * PAGE_ROWS, LANES), F32)],
        ),
        out_shape=jax.ShapeDtypeStruct((n_steps * npg * _SEG_PER_PAGE, KV_ROW), BF16),
        compiler_params=_cparams(("arbitrary",)),
        name="nsa_compress",
    )(table, *([pool] * npg), pe, w1bd, w2bd)
    return out.reshape(n_seq, pages_per_seq * _SEG_PER_PAGE, KV_ROW)


def _compress_weights(cmp_pos, cmp_w1, cmp_w2):
    n_m = CMP_LEN // CMP_STRIDE
    pe = cmp_pos.reshape(2, n_m, CMP_STRIDE, 1, NSA_DH)
    pe = jnp.broadcast_to(pe, (2, n_m, CMP_STRIDE, NSA_GROUPS, NSA_DH))
    gd = NSA_GROUPS * NSA_DH
    pe = jnp.transpose(pe.reshape(2, n_m, CMP_STRIDE // 2, 2, gd), (1, 2, 0, 3, 4)).reshape(
        n_m, CMP_STRIDE // 2, 2, 2 * gd)
    w1 = cmp_w1.reshape(2, n_m, CMP_STRIDE // 2, 2, NSA_DH, CMP_HID)
    w1 = jnp.transpose(w1, (1, 2, 0, 3, 4, 5))
    eye = jnp.eye(NSA_GROUPS, dtype=F32)
    w1bd = jnp.einsum('mskidh,gG->mskigdGh', w1, eye).reshape(
        n_m, CMP_STRIDE // 2, 2, 2 * gd, NSA_GROUPS * CMP_HID).astype(BF16)
    w2bd = jnp.einsum('khd,gG->kghGd', cmp_w2, eye).reshape(
        2, NSA_GROUPS * CMP_HID, NSA_GROUPS * NSA_DH).astype(BF16)
    return pe, w1bd, w2bd


def _overlap_matrix(n_rows, n_blk_lanes):
    ci = (np.arange(n_rows)[:, None] - 1) * CMP_STRIDE
    sj = np.arange(n_blk_lanes)[None, :] * SLC_BLK
    ov = (ci < sj + SLC_BLK) & (ci + CMP_LEN > sj) & (ci >= 0)
    return jnp.asarray(ov.astype(np.float32))


_SEL_KEY_BLOCK = 1024


_MASK_BIG = 1e30
_VAL_ROWS = NSA_DH + 16


def _nsa_prompt_body(qt_ref, smt_ref, ck_ref, cvt_ref, ska_ref, sva_ref, wk_ref, wvt_ref, ovt_ref, o_ref,
                     m_sc, acc_sc, s_sc):
    iq = pl.program_id(1)
    qb = Q_BLK
    t0 = iq * qb
    n_cmp_rows = ck_ref.shape[0]
    n_blk = ovt_ref.shape[0]
    w_rows = WINDOW + qb
    gd = NSA_GROUPS * NSA_DH
    t_row = t0 + lax.broadcasted_iota(jnp.int32, (1, qb), 1)
    gates = jax.nn.sigmoid(smt_ref[...])
    ci = lax.broadcasted_iota(jnp.int32, (n_cmp_rows, 1), 0)
    cmp_mask = (ci >= 1) & (CMP_STRIDE * ci + (CMP_LEN - CMP_STRIDE - 1) <= t_row)
    blk = lax.broadcasted_iota(jnp.int32, (n_blk, 1), 0)
    forced = (blk == t_row // SLC_BLK) | (blk == 0)
    valid = blk * SLC_BLK <= t_row
    start = pl.multiple_of(jnp.maximum(t0 - WINDOW, 0), qb)
    wpos = start + lax.broadcasted_iota(jnp.int32, (w_rows, 1), 0)
    dlt = t_row - wpos
    win_mask = (dlt >= 0) & (dlt < WINDOW)
    kblk = min(_SEL_KEY_BLOCK, ska_ref.shape[0])
    key_col = lax.broadcasted_iota(jnp.int32, (kblk, 1), 0)
    scale = NSA_DH ** -0.5 * math.log2(math.e)
    heads = lambda x: [x[:, r * qb:(r + 1) * qb] for r in range(NSA_REP)]

    for g in range(NSA_GROUPS):
        vrows = slice(g * NSA_DH, (g + 1) * NSA_DH)
        qt = jnp.concatenate([qt_ref[(g * NSA_REP + r) * NSA_DH:(g * NSA_REP + r + 1) * NSA_DH, :]
                              for r in range(NSA_REP)], axis=1) * scale
        qt = jnp.concatenate([qt, jnp.zeros_like(qt)] if g == 0 else [jnp.zeros_like(qt), qt], axis=0)
        qt = qt.astype(BF16)
        s_c = _dot(ck_ref[...], qt)
        p_c = [_masked_softmax(s, cmp_mask, axis=0, exp=jnp.exp2) for s in heads(s_c)]
        o_c = _dot(cvt_ref[vrows, :], jnp.concatenate(p_c, axis=1).astype(BF16))
        imp = _dot(ovt_ref[...], p_c[0] + p_c[1] + p_c[2] + p_c[3], HIGHEST)
        imp = jnp.where(forced, 1e9, jnp.where(valid, imp, -1.0))
        sel = _select_top(imp, blk, n_blk, axis=0)

        bias = ((sel - 1.0) * _MASK_BIG).astype(BF16)
        rhs = jnp.concatenate([qt, jnp.concatenate([bias] * NSA_REP, axis=1)], axis=0)
        m_sc[...] = jnp.full(m_sc.shape, NEG, F32)
        acc_sc[...] = jnp.zeros(acc_sc.shape, F32)

        def scores(kb):
            return _dot(ska_ref[pl.ds(pl.multiple_of(kb * kblk, kblk), kblk), :], rhs)

        def absorb(kb, s_s, causal):
            r0 = pl.multiple_of(kb * kblk, kblk)
            m_old = m_sc[...]
            m_new, es = [], []
            for r, s in enumerate(heads(s_s)):
                if causal:
                    s = jnp.where(r0 + key_col <= t_row, s, NEG)
                m_r = jnp.maximum(m_old[:, r * qb:(r + 1) * qb], jnp.max(s, axis=0, keepdims=True))
                es.append(jnp.exp2(s - m_r).astype(BF16))
                m_new.append(m_r)
            m_new = jnp.concatenate(m_new, axis=1)
            corr = jnp.exp2(m_old - m_new)
            m_sc[...] = m_new
            acc_sc[...] = corr * acc_sc[...] + _dot(sva_ref[g, :, pl.ds(r0, kblk)], jnp.concatenate(es, axis=1))

        n_steps = (t0 + qb + kblk - 1) // kblk
        s_sc[0] = scores(0)

        def kv_step(kb, carry):
            s_next = scores(kb + 1)
            absorb(kb, s_sc[kb % 2], False)
            s_sc[(kb + 1) % 2] = s_next
            return carry

        lax.fori_loop(0, n_steps - 1, kv_step, 0)
        absorb(n_steps - 1, s_sc[(n_steps - 1) % 2], True)
        o_s = acc_sc[0:NSA_DH, :] * (1.0 / jnp.maximum(acc_sc[NSA_DH:NSA_DH + 1, :], 1e-30))
        s_w = _dot(wk_ref[pl.ds(start, w_rows), :], qt)
        p_w = jnp.concatenate([_masked_softmax(s, win_mask, axis=0, exp=jnp.exp2).astype(BF16) for s in heads(s_w)],
                              axis=1)
        o_w = _dot(wvt_ref[vrows, pl.ds(start, w_rows)], p_w)
        for r in range(NSA_REP):
            h = g * NSA_REP + r
            row0 = 2 * GDN_HEADS + 3 * h
            cs = slice(r * qb, (r + 1) * qb)
            o_ref[h * NSA_DH:(h + 1) * NSA_DH, :] = (
                gates[row0:row0 + 1, :] * o_c[:, cs] + gates[row0 + 1:row0 + 2, :] * o_s[:, cs]
                + gates[row0 + 2:row0 + 3, :] * o_w[:, cs])


def _nsa_prompt(q_b, small, cmp_rows, kv_s, kv_w):
    b, t, _ = q_b.shape
    gd = NSA_GROUPS * NSA_DH
    n_blk = max(t // SLC_BLK, LANES)
    ovt = _overlap_matrix(cmp_rows.shape[1], n_blk).T
    tr = lambda a: jnp.swapaxes(a, 1, 2)
    keys = lambda kv: kv[:, :, :gd].astype(BF16)
    vals_t = lambda kv: tr(kv[:, :, gd:]).astype(BF16)
    blk_onehot = jnp.asarray(np.arange(t)[:, None] // SLC_BLK == np.arange(n_blk)[None, :], BF16)
    keys_aug = jnp.concatenate([keys(kv_s), jnp.broadcast_to(blk_onehot, (b, t, n_blk))], axis=2)
    v_t = vals_t(kv_s).reshape(b, NSA_GROUPS, NSA_DH, t)
    extra = jnp.zeros((b, NSA_GROUPS, _VAL_ROWS - NSA_DH, t), BF16).at[:, :, 0].set(1.0)
    vals_aug = jnp.concatenate([v_t, extra], axis=2)
    whole = lambda bi, i: (bi, 0, 0)
    tok = lambda bi, i: (bi, 0, i)
    rq = NSA_REP * Q_BLK
    out_t = pl.pallas_call(
        _nsa_prompt_body,
        grid=(b, t // Q_BLK),
        in_specs=[pl.BlockSpec((None, NSA_HEADS * NSA_DH, Q_BLK), tok),
                  pl.BlockSpec((None, _SMALL_W, Q_BLK), tok),
                  pl.BlockSpec((None, cmp_rows.shape[1], gd), whole),
                  pl.BlockSpec((None, gd, cmp_rows.shape[1]), whole),
                  pl.BlockSpec((None, t, gd + n_blk), whole),
                  pl.BlockSpec((None, NSA_GROUPS, _VAL_ROWS, t), lambda bi, i: (bi, 0, 0, 0)),
                  pl.BlockSpec((None, t, gd), whole), pl.BlockSpec((None, gd, t), whole),
                  pl.BlockSpec(ovt.shape, lambda bi, i: (0, 0))],
        out_specs=pl.BlockSpec((None, NSA_HEADS * NSA_DH, Q_BLK), tok),
        out_shape=jax.ShapeDtypeStruct((b, NSA_HEADS * NSA_DH, t), F32),
        scratch_shapes=[pltpu.VMEM((1, rq), F32), pltpu.VMEM((_VAL_ROWS, rq), F32),
                        pltpu.VMEM((2, min(_SEL_KEY_BLOCK, t), rq), F32)],
        compiler_params=_cparams(("parallel", "arbitrary")),
        name="nsa_prompt",
    )(tr(q_b), tr(small), cmp_rows[:, :, :gd], tr(cmp_rows[:, :, gd:]), keys_aug, vals_aug,
      keys(kv_w), vals_t(kv_w), ovt)
    return tr(out_t)


_DEC_PAGES_PER_STEP = 16


def _nsa_decode_body(pt_ref, qbd_ref, sm_ref, cmp_ref, wint_ref, neww_ref, news_ref, ov_ref, *refs, n_past_blk,
                     npg):
    page_refs = refs[:npg]
    o_ref, sel_sc, m_sc, l_sc, acc_sc, oc_sc, ow_sc = refs[npg:]
    j = pl.program_id(1)
    n_blk = ov_ref.shape[1]
    gd = NSA_GROUPS * NSA_DH
    qf = qbd_ref[...] * (NSA_DH ** -0.5)
    q = qf.astype(BF16)
    head = lax.broadcasted_iota(jnp.int32, (NSA_HEADS, gd), 0)
    lane = lax.broadcasted_iota(jnp.int32, (NSA_HEADS, gd), 1)
    own = (lane // NSA_DH == head // NSA_REP).astype(F32)

    @pl.when(j == 0)
    def _():
        n_rows = cmp_ref.shape[0]
        ci = lax.broadcasted_iota(jnp.int32, (1, n_rows), 1)
        p_c = _masked_softmax(_dot_nt(q, cmp_ref[:, 0:gd]), ci >= 1)
        oc_sc[...] = _dot(p_c.astype(BF16), cmp_ref[:, gd:2 * gd]) * own
        hr = lax.broadcasted_iota(jnp.int32, (NSA_HEADS, NSA_HEADS), 0) // NSA_REP
        hc = lax.broadcasted_iota(jnp.int32, (NSA_HEADS, NSA_HEADS), 1) // NSA_REP
        p_grp = _dot((hr == hc).astype(F32), p_c, HIGHEST)
        imp = _dot(p_grp, ov_ref[...], HIGHEST)
        blk = lax.broadcasted_iota(jnp.int32, (1, n_blk), 1)
        imp = jnp.where((blk == n_past_blk) | (blk == 0), 1e9, jnp.where(blk < n_past_blk, imp, -2e38))
        sel_sc[...] = _select_top(imp, blk, n_blk)
        n_win = wint_ref.shape[1]
        wi = lax.broadcasted_iota(jnp.int32, (1, n_win), 1)
        neww = neww_ref[...]
        s_w = jnp.where(wi >= 1, _dot(q, wint_ref[0:gd, :].astype(BF16)), NEG)
        s_cur = jnp.sum(qf * neww[:, 0:gd], axis=-1, keepdims=True)
        m_w = jnp.maximum(jnp.max(s_w, axis=-1, keepdims=True), s_cur)
        e_w = jnp.where(wi >= 1, jnp.exp(s_w - m_w), 0.0)
        e_cur = jnp.exp(s_cur - m_w)
        den = jnp.maximum(jnp.sum(e_w, axis=-1, keepdims=True) + e_cur, 1e-30)
        pv_w = _dot_nt(e_w.astype(BF16), wint_ref[gd:2 * gd, :].astype(BF16)) + e_cur * neww[:, gd:2 * gd]
        ow_sc[...] = pv_w * (1.0 / den) * own
        m_sc[...] = jnp.full(m_sc.shape, NEG, F32)
        l_sc[...] = jnp.zeros(l_sc.shape, F32)
        acc_sc[...] = jnp.zeros(acc_sc.shape, F32)

    n_keys = npg * PAGE_ROWS
    kt = jnp.concatenate([r[0:gd, :] for r in page_refs], axis=1).astype(BF16)
    vt = jnp.concatenate([r[gd:2 * gd, :] for r in page_refs], axis=1).astype(BF16)
    e_row = lax.broadcasted_iota(jnp.int32, (n_blk, n_keys), 0)
    e_col = lax.broadcasted_iota(jnp.int32, (n_blk, n_keys), 1) // SLC_BLK
    first_blk = (PAGE_ROWS // SLC_BLK) * (j * npg)
    msk = _dot(sel_sc[...].astype(BF16), (e_row == first_blk + e_col).astype(BF16)) > 0.5
    sc = jnp.where(msk, _dot(q, kt), NEG)
    m_old = m_sc[...]
    m_new = jnp.maximum(m_old, jnp.max(sc, axis=-1, keepdims=True))
    corr = jnp.exp(m_old - m_new)
    e = jnp.where(msk, jnp.exp(sc - m_new), 0.0)
    l_sc[...] = corr * l_sc[...] + jnp.sum(e, axis=-1, keepdims=True)
    acc_sc[...] = corr * acc_sc[...] + _dot_nt(e.astype(BF16), vt)
    m_sc[...] = m_new

    @pl.when(j == pl.num_programs(1) - 1)
    def _():
        new = news_ref[...]
        s_new = jnp.sum(qf * new[:, 0:gd], axis=-1, keepdims=True)
        m_old = m_sc[...]
        m_new = jnp.maximum(m_old, s_new)
        corr = jnp.exp(m_old - m_new)
        e = jnp.exp(s_new - m_new)
        l = corr * l_sc[...] + e
        acc = corr * acc_sc[...] + e * new[:, gd:2 * gd]
        o_s = acc / jnp.maximum(l, 1e-30) * own
        gates = jax.nn.sigmoid(sm_ref[...])
        glane = lax.broadcasted_iota(jnp.int32, (NSA_HEADS, _SMALL_W), 1)
        ghead = lax.broadcasted_iota(jnp.int32, (NSA_HEADS, _SMALL_W), 0)
        gate = [jnp.sum(jnp.where(glane == 2 * GDN_HEADS + 3 * ghead + br, gates, 0.0), axis=-1, keepdims=True)
                for br in range(3)]
        o = gate[0] * oc_sc[...] + gate[1] * o_s + gate[2] * ow_sc[...]
        o_ref[...] = o[:, 0:NSA_DH] + o[:, NSA_DH:gd]


def _nsa_decode(qbd, small, cmp_rows, win_t, new_w, new_s, pool, table):
    b = qbd.shape[0]
    pages_per_seq = table.shape[0] // b
    npg = min(_DEC_PAGES_PER_STEP, pages_per_seq)
    n_past_blk = pages_per_seq * (PAGE_ROWS // SLC_BLK)
    n_blk = -(-(n_past_blk + 1) // LANES) * LANES
    ov = _overlap_matrix(cmp_rows.shape[1], n_blk)
    per_seq = lambda bi, j, pt: (bi, 0, 0)
    page_spec = lambda p: pl.BlockSpec(
        (None, KV_ROW, PAGE_ROWS), lambda bi, j, pt: (pt[bi * pages_per_seq + j * npg + p], 0, 0))
    gd = NSA_GROUPS * NSA_DH
    out = pl.pallas_call(
        functools.partial(_nsa_decode_body, n_past_blk=n_past_blk, npg=npg),
        grid_spec=pltpu.PrefetchScalarGridSpec(
            num_scalar_prefetch=1,
            grid=(b, pages_per_seq // npg),
            in_specs=[pl.BlockSpec((None, NSA_HEADS, gd), per_seq),
                      pl.BlockSpec((None, 1, _SMALL_W), per_seq),
                      pl.BlockSpec((None,) + cmp_rows.shape[1:], per_seq),
                      pl.BlockSpec((None,) + win_t.shape[1:], per_seq),
                      pl.BlockSpec((None, 1, KV_ROW), per_seq), pl.BlockSpec((None, 1, KV_ROW), per_seq),
                      pl.BlockSpec(ov.shape, lambda bi, j, pt: (0, 0))]
            + [page_spec(p) for p in range(npg)],
            out_specs=pl.BlockSpec((None, NSA_HEADS, NSA_DH), per_seq),
            scratch_shapes=[pltpu.VMEM((NSA_HEADS, n_blk), F32), pltpu.VMEM((NSA_HEADS, 1), F32),
                            pltpu.VMEM((NSA_HEADS, 1), F32), pltpu.VMEM((NSA_HEADS, gd), F32),
                            pltpu.VMEM((NSA_HEADS, gd), F32), pltpu.VMEM((NSA_HEADS, gd), F32)],
        ),
        out_shape=jax.ShapeDtypeStruct((b, NSA_HEADS, NSA_DH), F32),
        compiler_params=_cparams(("parallel", "arbitrary")),
        name="nsa_decode",
    )(table, qbd, small, cmp_rows, win_t, new_w, new_s, ov, *([pool] * npg))
    return out.reshape(b, NSA_HEADS * NSA_DH)


def _outproj_body(oa_ref, ob_ref, gm_ref, x_ref, gt_ref, wa_ref, wb_ref, wo_ref, lg_ref, lb_ref, o_ref):
    ya = _dot(oa_ref[...].astype(BF16), wa_ref[...])
    yb = _dot(ob_ref[...].astype(BF16), wb_ref[...])
    gm = jax.nn.sigmoid(gm_ref[...])
    u = (gm[:, :D_MODEL] * ya + gm[:, D_MODEL:] * yb).astype(BF16)
    y = _dot(u, wo_ref[...])
    xr = ALPHA * x_ref[...] + gt_ref[...] * y
    o_ref[...] = _ln_rows(xr) * lg_ref[...] + lb_ref[...]


def _outproj(o_a, o_b, g_m, x2d, gt, w_up_a, w_up_b, w_out, ln_g, ln_b, tm, rows_per_mod):
    m = x2d.shape[0]
    row = lambda n: pl.BlockSpec((tm, n), lambda i: (i, 0))
    fixed = lambda shape: pl.BlockSpec(shape, lambda i: (0, 0))
    return pl.pallas_call(
        _outproj_body,
        grid=(m // tm,),
        in_specs=[row(o_a.shape[1]), row(o_b.shape[1]), row(2 * D_MODEL), row(D_MODEL),
                  _mod_spec(gt, tm, rows_per_mod),
                  fixed(w_up_a.shape), fixed(w_up_b.shape), fixed(w_out.shape),
                  fixed((1, D_MODEL)), fixed((1, D_MODEL))],
        out_specs=row(D_MODEL),
        out_shape=jax.ShapeDtypeStruct((m, D_MODEL), F32),
        compiler_params=_cparams(("parallel",)),
        name="mixer_outproj",
    )(o_a, o_b, g_m, x2d, gt, w_up_a, w_up_b, w_out, ln_g, ln_b)


_FF_TILE = 256


def _ffn_body(x_ref, sc_ref, sh_ref, gt_ref, wg_ref, wu_ref, wd_ref, lg_ref, lb_ref, o_ref, h_sc, acc_sc):
    f = pl.program_id(1)

    @pl.when(f == 0)
    def _():
        h = _ln_rows(x_ref[...]) * (1.0 + sc_ref[...]) + sh_ref[...]
        h_sc[...] = h.astype(BF16)
        acc_sc[...] = jnp.zeros(acc_sc.shape, F32)

    hb = h_sc[...]
    hid = _silu(_dot(hb, wg_ref[...])) * _dot(hb, wu_ref[...])
    acc_sc[...] += _dot(hid.astype(BF16), wd_ref[...])

    @pl.when(f == pl.num_programs(1) - 1)
    def _():
        xr = ALPHA * x_ref[...] + gt_ref[...] * acc_sc[...]
        o_ref[...] = _ln_rows(xr) * lg_ref[...] + lb_ref[...]


def _ffn(x2d, sc, sh, gt, w_gu, w_down, ln_g, ln_b, tm, rows_per_mod):
    m = x2d.shape[0]
    tf = _FF_TILE
    n_f = D_FF // tf
    row = pl.BlockSpec((tm, D_MODEL), lambda i, f: (i, 0))
    fixed = lambda shape: pl.BlockSpec(shape, lambda i, f: (0, 0))
    mod = lambda a: _mod_spec(a, tm, rows_per_mod)
    return pl.pallas_call(
        _ffn_body,
        grid=(m // tm, n_f),
        in_specs=[row, mod(sc), mod(sh), mod(gt),
                  pl.BlockSpec((D_MODEL, tf), lambda i, f: (0, f)),
                  pl.BlockSpec((D_MODEL, tf), lambda i, f: (0, f + n_f)),
                  pl.BlockSpec((tf, D_MODEL), lambda i, f: (f, 0)),
                  fixed((1, D_MODEL)), fixed((1, D_MODEL))],
        out_specs=row,
        out_shape=jax.ShapeDtypeStruct((m, D_MODEL), F32),
        scratch_shapes=[pltpu.VMEM((tm, D_MODEL), BF16), pltpu.VMEM((tm, D_MODEL), F32)],
        compiler_params=_cparams(("parallel", "arbitrary")),
        name="ffn_dense",
    )(x2d, sc, sh, gt, w_gu, w_gu, w_down, ln_g, ln_b)


_MOE_CHUNK = 288
_MOE_FF_TILE = 1408


def _moe_body(x_ref, sc_ref, sh_ref, gt_ref, wr_ref, wg_ref, wu_ref, wd_ref, lg_ref, lb_ref, o_ref,
              h_sc, w_sc, sel_sc, rank_sc, selt_sc, rankt_sc, xg_sc, yg_sc, y_sc, *, chunk):
    e = pl.program_id(1)
    f = pl.program_id(2)
    tb = x_ref.shape[0]
    lane = lax.broadcasted_iota(jnp.int32, (1, LANES), 1)

    @pl.when((e == 0) & (f == 0))
    def _():
        h = _ln_rows(x_ref[...]) * (1.0 + sc_ref[...]) + sh_ref[...]
        h_sc[...] = h.astype(BF16)
        logits = jnp.where(lane < N_EXP, _dot(h, wr_ref[...], HIGHEST), NEG)
        ex = jnp.exp(logits - jnp.max(logits, axis=-1, keepdims=True))
        probs = ex / jnp.sum(ex, axis=-1, keepdims=True)
        lane_f = lane.astype(F32)
        p1 = jnp.max(probs, axis=-1, keepdims=True)
        i1 = jnp.min(jnp.where(probs == p1, lane_f, float(LANES)), axis=-1, keepdims=True)
        rest = jnp.where(lane_f == i1, -1.0, probs)
        p2 = jnp.max(rest, axis=-1, keepdims=True)
        i2 = jnp.min(jnp.where(rest == p2, lane_f, float(LANES)), axis=-1, keepdims=True)
        w_sc[...] = (jnp.where(lane_f == i1, p1, 0.0) + jnp.where(lane_f == i2, p2, 0.0)) / (p1 + p2)
        sel = ((lane_f == i1) | (lane_f == i2)).astype(F32)
        earlier = (lax.broadcasted_iota(jnp.int32, (tb, tb), 1)
                   < lax.broadcasted_iota(jnp.int32, (tb, tb), 0)).astype(BF16)
        rank = _dot(earlier, sel.astype(BF16))
        sel_sc[...] = sel
        rank_sc[...] = rank
        selt_sc[...] = sel.T
        rankt_sc[...] = rank.T
        y_sc[...] = jnp.zeros(y_sc.shape, F32)

    sel_row = selt_sc[pl.ds(e, 1), :]
    rank_row = rankt_sc[pl.ds(e, 1), :]
    count = jnp.sum(sel_row).astype(jnp.int32)
    n_chunks = (count + chunk - 1) // chunk

    @pl.when(f == 0)
    def _():
        def gather(c, carry):
            slot = (c * chunk + lax.broadcasted_iota(jnp.int32, (chunk, 1), 0)).astype(F32)
            pick = ((rank_row == slot) & (sel_row > 0.5)).astype(BF16)
            xg_sc[pl.ds(pl.multiple_of(c * chunk, chunk), chunk), :] = _dot(pick, h_sc[...]).astype(BF16)
            return carry

        lax.fori_loop(0, n_chunks, gather, 0)

    def expert(c, carry):
        rows = pl.ds(pl.multiple_of(c * chunk, chunk), chunk)
        xg = xg_sc[rows, :]
        hid = _silu(_dot(xg, wg_ref[...])) * _dot(xg, wu_ref[...])
        part = _dot(hid.astype(BF16), wd_ref[...])

        @pl.when(f == 0)
        def _():
            yg_sc[rows, :] = part

        @pl.when(f > 0)
        def _():
            yg_sc[rows, :] += part

        return carry

    lax.fori_loop(0, n_chunks, expert, 0)

    @pl.when(f == pl.num_programs(2) - 1)
    def _():
        pick_e = lane == e
        sel_col = jnp.sum(jnp.where(pick_e, sel_sc[...], 0.0), axis=-1, keepdims=True)
        rank_col = jnp.sum(jnp.where(pick_e, rank_sc[...], 0.0), axis=-1, keepdims=True)
        w_col = jnp.sum(jnp.where(pick_e, w_sc[...], 0.0), axis=-1, keepdims=True)

        def scatter(c, carry):
            slot = (c * chunk + lax.broadcasted_iota(jnp.int32, (1, chunk), 1)).astype(F32)
            place = ((rank_col == slot) & (sel_col > 0.5)).astype(BF16)
            rows = pl.ds(pl.multiple_of(c * chunk, chunk), chunk)
            y_sc[...] += w_col * _dot(place, yg_sc[rows, :].astype(BF16))
            return carry

        lax.fori_loop(0, n_chunks, scatter, 0)

    @pl.when((e == pl.num_programs(1) - 1) & (f == pl.num_programs(2) - 1))
    def _():
        xr = ALPHA * x_ref[...] + gt_ref[...] * y_sc[...]
        o_ref[...] = _ln_rows(xr) * lg_ref[...] + lb_ref[...]


def _moe(x2d, sc, sh, gt, w_router, w_gu, w_down, ln_g, ln_b, tm, rows_per_mod):
    m = x2d.shape[0]
    n_e = w_gu.shape[0]
    tf = _MOE_FF_TILE
    n_f = D_FF // tf
    chunk = min(_MOE_CHUNK, tm)
    slot_rows = -(-tm // chunk) * chunk
    row = pl.BlockSpec((tm, D_MODEL), lambda i, e, f: (i, 0), pipeline_mode=pl.Buffered(1))
    fixed = lambda shape: pl.BlockSpec(shape, lambda i, e, f: (0, 0))
    mod = lambda a: _mod_spec(a, tm, rows_per_mod)
    per_tok = lambda: pltpu.VMEM((tm, LANES), F32)
    per_exp = lambda: pltpu.VMEM((LANES, tm), F32)
    return pl.pallas_call(
        functools.partial(_moe_body, chunk=chunk),
        grid=(m // tm, n_e, n_f),
        in_specs=[row, mod(sc), mod(sh), mod(gt), fixed(w_router.shape),
                  pl.BlockSpec((None, D_MODEL, tf), lambda i, e, f: (e, 0, f)),
                  pl.BlockSpec((None, D_MODEL, tf), lambda i, e, f: (e, 0, f + n_f)),
                  pl.BlockSpec((None, tf, D_MODEL), lambda i, e, f: (e, f, 0)),
                  fixed((1, D_MODEL)), fixed((1, D_MODEL))],
        out_specs=row,
        out_shape=jax.ShapeDtypeStruct((m, D_MODEL), F32),
        scratch_shapes=[pltpu.VMEM((tm, D_MODEL), BF16), per_tok(), per_tok(), per_tok(), per_exp(), per_exp(),
                        pltpu.VMEM((slot_rows, D_MODEL), BF16), pltpu.VMEM((slot_rows, D_MODEL), F32),
                        pltpu.VMEM((tm, D_MODEL), F32)],
        compiler_params=_cparams(("parallel", "arbitrary", "arbitrary")),
        name="moe_routed",
    )(x2d, sc, sh, gt, w_router, w_gu, w_gu, w_down, ln_g, ln_b)


def _permute_w_in(w):
    sizes = [C_QKV, GDN_HEADS * GDN_DV, GDN_HEADS, GDN_HEADS, NSA_HEADS * NSA_DH, KV_ROW, KV_ROW, KV_ROW,
             3 * NSA_HEADS, 2 * D_MODEL]
    offs = np.cumsum([0] + sizes)
    qkv, z, b_a, a_a, q_b, kv_c, kv_s, kv_w, g_b, g_m = (w[:, offs[i]:offs[i + 1]] for i in range(len(sizes)))
    pad = jnp.zeros((w.shape[0], _SMALL_W - 2 * GDN_HEADS - 3 * NSA_HEADS), w.dtype)
    return jnp.concatenate([qkv, z, g_m, q_b, kv_c, kv_s, kv_w, b_a, a_a, g_b, pad], axis=1).astype(BF16)


def _layer_weights(l, w_in, conv_w, a_log, dt_bias, norm_w_a, cmp_pos, cmp_w1, cmp_w2, w_up_a, w_up_b, w_out,
                   ln_g, ln_b):
    lane_pad = lambda v: jnp.zeros((1, _SMALL_W), F32).at[0, GDN_HEADS:2 * GDN_HEADS].set(v)
    pe, w1bd, w2bd = _compress_weights(cmp_pos[l], cmp_w1[l], cmp_w2[l])
    return dict(
        w_in=_permute_w_in(w_in[l]), conv_w=conv_w[l], alog=lane_pad(a_log[l]), dt=lane_pad(dt_bias[l]),
        norm_w=norm_w_a[l].reshape(1, GDN_DV), pe=pe, w1bd=w1bd, w2bd=w2bd,
        w_up_a=w_up_a[l].astype(BF16), w_up_b=w_up_b[l].astype(BF16), w_out=w_out[l].astype(BF16),
        ln_g0=ln_g[l, 0].reshape(1, D_MODEL), ln_b0=ln_b[l, 0].reshape(1, D_MODEL),
        ln_g1=ln_g[l, 1].reshape(1, D_MODEL), ln_b1=ln_b[l, 1].reshape(1, D_MODEL))


def _mixer_common(x2d, mods, wts, b, t, t_pad, conv_buf, s0, tm, rows_per_mod):
    sh_m, sc_m = mods[0], mods[1]
    qkv, z, g_m, q_b, kv_c, kv_s, kv_w, small = _inproj(x2d, sc_m, sh_m, wts['w_in'], tm, rows_per_mod)
    seq = lambda a: a.reshape(b, t, a.shape[-1])
    padt = lambda a: jnp.pad(seq(a), ((0, 0), (0, t_pad - t), (0, 0)))
    buf8 = jnp.pad(conv_buf, ((0, 0), (SUBLANES - (CONV_TAPS - 1), 0), (0, 0)))
    qa, ka, va, bg = _gdn_prep(padt(qkv), buf8, wts['conv_w'], padt(small), wts['alog'], wts['dt'], t)
    o_a, s_new = _gdn(qa, ka, va, bg, padt(z), s0, wts['norm_w'])
    o_a = o_a[:, :t].reshape(b * t, GDN_HEADS * GDN_DV)
    conv_new = jnp.concatenate([conv_buf, seq(qkv)], axis=1)[:, -(CONV_TAPS - 1):]
    return (g_m, q_b, kv_c, kv_s, kv_w, small), o_a, s_new, conv_new


def _rows5(a, b, t):
    return a.reshape(b, t, 2, NSA_GROUPS, NSA_DH)


def _feature_major(rows):
    nd = rows.ndim
    return jnp.transpose(rows, tuple(range(nd - 4)) + (nd - 3, nd - 2, nd - 1, nd - 4))


def _prompt_layer(x, mods, wts, ffn_args, routed):
    b, t, _ = x.shape
    x2d = x.reshape(b * t, D_MODEL)
    tm = 256
    zeros_buf = jnp.zeros((b, CONV_TAPS - 1, C_QKV), F32)
    zeros_s = jnp.zeros((b, GDN_HEADS, GDN_DK, GDN_DV), F32)
    (g_m, q_b, kv_c, kv_s, kv_w, small), o_a, s_new, conv_new = _mixer_common(
        x2d, mods, wts, b, t, t, zeros_buf, zeros_s, tm, t)
    pages = (b * t) // PAGE_ROWS
    cmp_rows = _compress(kv_c.reshape(pages, PAGE_ROWS, KV_ROW), jnp.arange(pages, dtype=jnp.int32), b,
                         wts['pe'], wts['w1bd'], wts['w2bd'], False)
    seq = lambda a: a.reshape(b, t, a.shape[-1])
    o_b = _nsa_prompt(seq(q_b), seq(small), cmp_rows, seq(kv_s), seq(kv_w))
    x1 = _outproj(o_a, o_b.reshape(b * t, -1), g_m, x2d, mods[2], wts['w_up_a'], wts['w_up_b'], wts['w_out'],
                  wts['ln_g0'], wts['ln_b0'], 512, t)
    x2 = (_moe if routed else _ffn)(x1, mods[4], mods[3], mods[5], *ffn_args, wts['ln_g1'], wts['ln_b1'],
                                    min(1024, t), t)
    win_new = _rows5(kv_w, b, t)[:, t - min(WINDOW, t):]
    return x2.reshape(b, t, D_MODEL), (s_new, conv_new, _rows5(kv_c, b, t), _rows5(kv_s, b, t), win_new)


def _sample_layer(x, mods, wts, ffn_args, routed, conv_buf, s0, pool_c, pool_s, win_buf, table):
    b = x.shape[0]
    x2d = x.reshape(b, D_MODEL)
    (g_m, q_b, kv_c, kv_s, kv_w, small), o_a, s_new, conv_new = _mixer_common(
        x2d, mods, wts, b, 1, _GDN_SHORT_CHUNK, conv_buf, s0, b, b)
    cmp_rows = _compress(pool_c, table, b, wts['pe'], wts['w1bd'], wts['w2bd'], True)
    win_new = jnp.concatenate([win_buf.reshape(b, -1, KV_ROW)[:, 1:], kv_w[:, None, :]], axis=1)
    win_t = _feature_major(win_buf).reshape(b, KV_ROW, -1)
    q4 = q_b.reshape(b, NSA_GROUPS, NSA_REP, 1, NSA_DH)
    eye = jnp.eye(NSA_GROUPS, dtype=F32).reshape(NSA_GROUPS, 1, NSA_GROUPS, 1)
    qbd = (q4 * eye).reshape(b, NSA_HEADS, NSA_GROUPS * NSA_DH)
    o_b = _nsa_decode(qbd, small.reshape(b, 1, _SMALL_W), cmp_rows, win_t, kv_w.reshape(b, 1, KV_ROW),
                      kv_s.reshape(b, 1, KV_ROW), pool_s, table)
    x1 = _outproj(o_a, o_b, g_m, x2d, mods[2], wts['w_up_a'], wts['w_up_b'], wts['w_out'],
                  wts['ln_g0'], wts['ln_b0'], b, b)
    x2 = (_moe if routed else _ffn)(x1, mods[4], mods[3], mods[5], *ffn_args, wts['ln_g1'], wts['ln_b1'], b, b)
    return x2.reshape(b, 1, D_MODEL), (s_new, conv_new, _rows5(kv_c, b, 1), _rows5(kv_s, b, 1),
                                       _rows5(win_new, b, win_new.shape[1]))


def kernel(x_prompt, x_sample, state_delta, state_conv, cache_cmp_kv, cache_slc_kv, state_win_kv, page_table,
           c_prompt, c_sample, w_ada, b_ada, w_in, conv_w, a_log, dt_bias, norm_w_a, cmp_pos, cmp_w1, cmp_w2,
           w_up_a, w_up_b, w_out, ln_g, ln_b, ffn_w_gu, ffn_w_down, moe_router, moe_w_gu, moe_w_down):
    bp = x_prompt.shape[0]
    db = x_sample.shape[0]
    n_layers = w_in.shape[0]
    n_pool = cache_cmp_kv.shape[1]
    mod_all = _adaln(jnp.concatenate([c_prompt, c_sample], axis=0), w_ada, b_ada)
    table = page_table.reshape(-1)
    pool_c = _feature_major(cache_cmp_kv).reshape(n_layers * n_pool, KV_ROW, PAGE_ROWS)
    pool_s = _feature_major(cache_slc_kv).reshape(n_layers * n_pool, KV_ROW, PAGE_ROWS)
    xp, xs = x_prompt, x_sample
    st_p, st_s = [], []
    for l in range(n_layers):
        wts = _layer_weights(l, w_in, conv_w, a_log, dt_bias, norm_w_a, cmp_pos, cmp_w1, cmp_w2, w_up_a, w_up_b,
                             w_out, ln_g, ln_b)
        routed = l % 2 == 1
        if routed:
            router = jnp.pad(moe_router[l // 2], ((0, 0), (0, LANES - N_EXP)))
            ffn_args = (router, moe_w_gu[l // 2].astype(BF16), moe_w_down[l // 2].astype(BF16))
        else:
            ffn_args = (ffn_w_gu[l // 2].astype(BF16), ffn_w_down[l // 2].astype(BF16))
        mod6 = mod_all[l].reshape(bp + db, 6, D_MODEL)
        mods_p = [mod6[:bp, i].reshape(bp, 1, D_MODEL) for i in range(6)]
        mods_s = [mod6[bp:, i].reshape(1, db, D_MODEL) for i in range(6)]
        xp, sp = _prompt_layer(xp, mods_p, wts, ffn_args, routed)
        xs, ss = _sample_layer(xs, mods_s, wts, ffn_args, routed, state_conv[l], state_delta[l],
                               pool_c, pool_s, state_win_kv[l], table + l * n_pool)
        st_p.append(sp)
        st_s.append(ss)
    stack = lambda sts, i: jnp.stack([s[i] for s in sts])
    return (xp, xs) + tuple(stack(st_p, i) for i in range(5)) + tuple(stack(st_s, i) for i in range(5))
```

```python
import functools
import math

import jax
import jax.numpy as jnp
import numpy as np
from jax import lax
from jax.experimental import pallas as pl
from jax.experimental.pallas import tpu as pltpu

F32 = jnp.float32
BF16 = jnp.bfloat16
HIGHEST = lax.Precision.HIGHEST

D_MODEL = 1024
N_LAYERS = 2
PAGE_ROWS = 128
GDN_HEADS = 4
GDN_DK = 128
GDN_DV = 128
CONV_TAPS = 4
GDN_CHUNK = 64
C_QKV = 2 * GDN_HEADS * GDN_DK + GDN_HEADS * GDN_DV
NSA_HEADS = 8
NSA_GROUPS = 2
NSA_REP = NSA_HEADS // NSA_GROUPS
NSA_DH = 64
KV_ROW = 2 * NSA_GROUPS * NSA_DH
CMP_LEN = 32
CMP_STRIDE = 16
CMP_HID = 256
SLC_BLK = 64
N_SEL = 16
WINDOW = 512
Q_BLK = 128
D_FF = 2816
N_EXP = 8
ALPHA = (2 * N_LAYERS) ** 0.25
EPS = 1e-5
NEG = -1e30

LANES = 128
SUBLANES = 8
VMEM_LIMIT = 56 * 1024 * 1024

_SMALL_W = LANES
IN_OUT_WIDTHS = (C_QKV, GDN_HEADS * GDN_DV, 2 * D_MODEL, NSA_HEADS * NSA_DH, KV_ROW, KV_ROW, KV_ROW, _SMALL_W)


def _cparams(sem):
    return pltpu.CompilerParams(dimension_semantics=sem, vmem_limit_bytes=VMEM_LIMIT)


def _silu(x):
    return x * jax.nn.sigmoid(x)


def _ln_rows(x):
    mu = jnp.mean(x, axis=-1, keepdims=True)
    xc = x - mu
    var = jnp.mean(xc * xc, axis=-1, keepdims=True)
    return xc * lax.rsqrt(var + EPS)


def _dot(a, b, precision=None):
    return jnp.dot(a, b, preferred_element_type=F32, precision=precision)


def _dot_nt(a, b, precision=None):
    return lax.dot_general(a, b, (((1,), (1,)), ((), ())), preferred_element_type=F32, precision=precision)


def _dot_tn(a, b, precision=None):
    return lax.dot_general(a, b, (((0,), (0,)), ((), ())), preferred_element_type=F32, precision=precision)


def _masked_softmax(s, mask, axis=-1, exp=jnp.exp):
    s = jnp.where(mask, s, NEG)
    m = jnp.max(s, axis=axis, keepdims=True)
    e = jnp.where(mask, exp(s - m), 0.0)
    return e * (1.0 / jnp.maximum(jnp.sum(e, axis=axis, keepdims=True), 1e-30))


def _select_top(imp, lane_idx, n_lanes, axis=-1):
    sel = jnp.zeros(imp.shape, F32)
    lane_idx = lane_idx.astype(F32)
    for _ in range(N_SEL):
        mx = jnp.max(imp, axis=axis, keepdims=True)
        idx = jnp.min(jnp.where(imp == mx, lane_idx, float(n_lanes)), axis=axis, keepdims=True)
        hit = lane_idx == idx
        sel = jnp.where(hit, 1.0, sel)
        imp = jnp.where(hit, -3e38, imp)
    return sel


def _adaln_body(c_ref, w_ref, b_ref, o_ref):
    s = _silu(c_ref[...]).astype(BF16)
    o_ref[...] = _dot(s, w_ref[...].astype(BF16)) + b_ref[...]


def _adaln(c_all, w_ada, b_ada):
    n_l, d, n = w_ada.shape
    r = c_all.shape[0]
    tn = 1536
    return pl.pallas_call(
        _adaln_body,
        grid=(n_l, n // tn),
        in_specs=[pl.BlockSpec((r, d), lambda l, j: (0, 0)),
                  pl.BlockSpec((None, d, tn), lambda l, j: (l, 0, j)),
                  pl.BlockSpec((None, 1, tn), lambda l, j: (l, 0, j))],
        out_specs=pl.BlockSpec((None, r, tn), lambda l, j: (l, 0, j)),
        out_shape=jax.ShapeDtypeStruct((n_l, r, n), F32),
        compiler_params=_cparams(("parallel", "parallel")),
        name="adaln_mod",
    )(c_all, w_ada, b_ada.reshape(n_l, 1, n))


def _inproj_body(x_ref, sc_ref, sh_ref, w_ref, *o_refs):
    h = (_ln_rows(x_ref[...]) * (1.0 + sc_ref[...]) + sh_ref[...]).astype(BF16)
    off = 0
    for o_ref, n in zip(o_refs, IN_OUT_WIDTHS):
        o_ref[...] = _dot(h, w_ref[:, off:off + n])
        off += n


def _mod_spec(mod, tm, rows_per_mod):
    if mod.shape[1] == 1:
        return pl.BlockSpec((None, 1, D_MODEL), lambda i, *_: ((i * tm) // rows_per_mod, 0, 0))
    return pl.BlockSpec((None, tm, D_MODEL), lambda i, *_: (0, i, 0))


def _inproj(x2d, sc, sh, w_perm, tm, rows_per_mod):
    m = x2d.shape[0]
    n_tot = w_perm.shape[1]
    return pl.pallas_call(
        _inproj_body,
        grid=(m // tm,),
        in_specs=[pl.BlockSpec((tm, D_MODEL), lambda i: (i, 0)),
                  _mod_spec(sc, tm, rows_per_mod), _mod_spec(sh, tm, rows_per_mod),
                  pl.BlockSpec((D_MODEL, n_tot), lambda i: (0, 0))],
        out_specs=[pl.BlockSpec((tm, n), lambda i: (i, 0)) for n in IN_OUT_WIDTHS],
        out_shape=[jax.ShapeDtypeStruct((m, n), F32) for n in IN_OUT_WIDTHS],
        compiler_params=_cparams(("parallel",)),
        name="ln_inproj",
    )(x2d, sc, sh, w_perm)


def _prep_body(qkv_ref, buf_ref, cw_ref, sm_ref, alog_ref, dt_ref, q_ref, k_ref, v_ref, bg_ref, xp_sc,
               *, tb, t_valid):
    i = pl.program_id(1)

    @pl.when(i == 0)
    def _():
        xp_sc[0:SUBLANES, :] = buf_ref[...]

    x = qkv_ref[...]
    xp_sc[SUBLANES:SUBLANES + tb, :] = x
    y = cw_ref[3:4, :] * x
    for j in range(CONV_TAPS - 1):
        lo = SUBLANES - (CONV_TAPS - 1) + j
        y = y + cw_ref[j:j + 1, :] * xp_sc[lo:lo + tb, :]
    xp_sc[0:SUBLANES, :] = xp_sc[tb:tb + SUBLANES, :]
    a = _silu(y)
    rows = i * tb + lax.broadcasted_iota(jnp.int32, (tb, 1), 0)
    live = rows < t_valid
    hk = GDN_HEADS * GDN_DK
    for h in range(GDN_HEADS):
        qh = a[:, h * GDN_DK:(h + 1) * GDN_DK]
        kh = a[:, hk + h * GDN_DK:hk + (h + 1) * GDN_DK]
        qn = qh * lax.rsqrt(jnp.sum(qh * qh, axis=-1, keepdims=True) + 1e-6) * (GDN_DK ** -0.5)
        kn = kh * lax.rsqrt(jnp.sum(kh * kh, axis=-1, keepdims=True) + 1e-6)
        q_ref[:, h * GDN_DK:(h + 1) * GDN_DK] = jnp.where(live, qn, 0.0)
        k_ref[:, h * GDN_DK:(h + 1) * GDN_DK] = jnp.where(live, kn, 0.0)
    v_ref[...] = jnp.where(live, a[:, 2 * hk:], 0.0)
    sm = sm_ref[...]
    beta = jax.nn.sigmoid(sm)
    z = sm + dt_ref[...]
    softplus = jnp.maximum(z, 0.0) + jnp.log(1.0 + jnp.exp(-jnp.abs(z)))
    g = -jnp.exp(alog_ref[...]) * softplus
    lane = lax.broadcasted_iota(jnp.int32, sm.shape, 1)
    bg_ref[...] = jnp.where(live, jnp.where(lane < GDN_HEADS, beta, g), 0.0)


def _gdn_prep(qkv, buf8, conv_w, small, alog_vec, dt_vec, t_valid):
    b, t, _ = qkv.shape
    tb = min(t, 512)
    hd = GDN_HEADS * GDN_DK
    row = lambda bi, i: (bi, i, 0)
    fixed = lambda bi, i: (0, 0)
    return pl.pallas_call(
        functools.partial(_prep_body, tb=tb, t_valid=t_valid),
        grid=(b, t // tb),
        in_specs=[pl.BlockSpec((None, tb, C_QKV), row),
                  pl.BlockSpec((None, SUBLANES, C_QKV), lambda bi, i: (bi, 0, 0)),
                  pl.BlockSpec((CONV_TAPS, C_QKV), fixed),
                  pl.BlockSpec((None, tb, _SMALL_W), row),
                  pl.BlockSpec((1, _SMALL_W), fixed), pl.BlockSpec((1, _SMALL_W), fixed)],
        out_specs=[pl.BlockSpec((None, tb, hd), row), pl.BlockSpec((None, tb, hd), row),
                   pl.BlockSpec((None, tb, hd), row), pl.BlockSpec((None, tb, _SMALL_W), row)],
        out_shape=[jax.ShapeDtypeStruct((b, t, hd), F32)] * 3 + [jax.ShapeDtypeStruct((b, t, _SMALL_W), F32)],
        scratch_shapes=[pltpu.VMEM((tb + SUBLANES, C_QKV), F32)],
        compiler_params=_cparams(("parallel", "arbitrary")),
        name="gdn_prep",
    )(qkv, buf8, conv_w, small, alog_vec, dt_vec)


def _split_bf16(x):
    hi = x.astype(BF16)
    return hi, (x - hi.astype(F32)).astype(BF16)


def _dot3(a, b, dims):
    ah, al = _split_bf16(a)
    bh, bl = _split_bf16(b)
    d = lambda x, y: lax.dot_general(x, y, dims, preferred_element_type=F32)
    return d(ah, bh) + (d(ah, bl) + d(al, bh))


def _bmm(a, b):
    return _dot3(a, b, (((2,), (1,)), ((0,), (0,))))


def _bmm_nt(a, b):
    return _dot3(a, b, (((2,), (2,)), ((0,), (0,))))


def _gdn_ut_body(q_ref, k_ref, v_ref, bg_ref, u_ref, w_ref, qd_ref, kd_ref, qk_ref, gl_ref, *, n_chunks, c_len):
    nc = n_chunks
    nb = GDN_HEADS * nc
    tb = nc * c_len
    row = lax.broadcasted_iota(jnp.int32, (nb, c_len, c_len), 1)
    col = lax.broadcasted_iota(jnp.int32, (nb, c_len, c_len), 2)
    incl = row >= col
    strict = row > col
    incl_f = incl.astype(F32)
    strict_f = strict.astype(F32)
    eye = (row == col).astype(F32)
    bg = bg_ref[...]
    split = lambda x: x.reshape(nc, c_len, x.shape[-1])
    head_slices = [slice(h * GDN_DK, (h + 1) * GDN_DK) for h in range(GDN_HEADS)]
    stack = lambda ref: jnp.concatenate([split(ref[:, sl]) for sl in head_slices], axis=0)
    q = stack(q_ref)
    k = stack(k_ref)
    v = stack(v_ref)
    beta = jnp.concatenate([split(bg[:, h:h + 1]) for h in range(GDN_HEADS)], axis=0)
    g = jnp.concatenate([split(bg[:, GDN_HEADS + h:GDN_HEADS + h + 1]) for h in range(GDN_HEADS)], axis=0)
    dlog = lax.dot_general(incl_f, g * strict_f, (((2,), (1,)), ((0,), (0,))), preferred_element_type=F32,
                           precision=HIGHEST)
    gc = dlog[:, :, 0:1] + g[:, 0:1, :]
    decay = jnp.where(incl, jnp.exp(dlog), 0.0)
    egc = jnp.exp(gc)
    g_last = gc[:, c_len - 1:c_len, :]
    kb = k * beta
    a = jnp.where(strict, _bmm_nt(kb, k) * decay, 0.0)
    p = -a
    tinv = eye + p
    for _ in range(int(math.log2(c_len)) - 1):
        p = _bmm(p, p)
        tinv = tinv + _bmm(tinv, p)
    u = _bmm(tinv, v * beta)
    w = _bmm(tinv, kb * egc)
    qk = jnp.where(incl, _bmm_nt(q, k) * decay, 0.0)
    qd = q * egc
    kd = k * jnp.exp(g_last - gc)
    gl = jnp.broadcast_to(jnp.exp(g_last), (nb, c_len, 1))
    lane = lax.broadcasted_iota(jnp.int32, (tb, _SMALL_W), 1)
    gl_all = jnp.zeros((tb, _SMALL_W), F32)
    for h, sl in enumerate(head_slices):
        of_head = lambda x: x[h * nc:(h + 1) * nc].reshape(tb, x.shape[-1])
        u_ref[:, sl] = of_head(u)
        w_ref[:, sl] = of_head(w)
        qd_ref[:, sl] = of_head(qd)
        kd_ref[:, sl] = of_head(kd)
        qk_ref[:, h * c_len:(h + 1) * c_len] = of_head(qk)
        gl_all = jnp.where(lane == h, of_head(gl), gl_all)
    gl_ref[...] = gl_all


def _gdn_scan_body(u_ref, w_ref, qd_ref, kd_ref, qk_ref, gl_ref, z_ref, s0_ref, nw_ref, o_ref, sfin_ref, s_sc,
                   *, n_chunks, n_seq, c_len):
    i = pl.program_id(1)

    @pl.when(i == 0)
    def _():
        s_sc[...] = s0_ref[...]

    nw = nw_ref[...]

    def chunk(c, carry):
        r0 = pl.multiple_of(c * c_len, c_len)
        rows = pl.ds(r0, c_len)
        gl_row = gl_ref[:, pl.ds(r0, 1), :]
        for h in range(GDN_HEADS):
            sl = slice(h * GDN_DK, (h + 1) * GDN_DK)
            s = s_sc[:, h]
            v_new = u_ref[:, rows, sl] - _bmm(w_ref[:, rows, sl], s)
            o = _bmm(qd_ref[:, rows, sl], s) + _bmm(qk_ref[:, rows, h * c_len:(h + 1) * c_len], v_new)
            kd = kd_ref[:, rows, sl]
            upd = jnp.stack([_dot3(kd[b], v_new[b], (((0,), (0,)), ((), ()))) for b in range(n_seq)])
            s_sc[:, h] = s * gl_row[:, :, h:h + 1] + upd
            z = z_ref[:, rows, sl]
            on = o * lax.rsqrt(jnp.mean(o * o, axis=-1, keepdims=True) + EPS) * nw
            o_ref[:, rows, sl] = on * _silu(z)
        return carry

    lax.fori_loop(0, n_chunks, chunk, 0)

    @pl.when(i == pl.num_programs(1) - 1)
    def _():
        sfin_ref[...] = s_sc[...]


_GDN_SEQ_PER_STEP = 8
_GDN_SHORT_CHUNK = 16


def _gdn(q, k, v, bg, z, s0, norm_w):
    b, t, hd = q.shape
    tb = min(t, 512)
    c_len = min(GDN_CHUNK, t)
    nc = tb // c_len
    row = lambda bi, i: (bi, i, 0)
    wide = lambda n: pl.BlockSpec((None, tb, n), row)
    qkw = GDN_HEADS * c_len
    u, w, qd, kd, qk, gl = pl.pallas_call(
        functools.partial(_gdn_ut_body, n_chunks=nc, c_len=c_len),
        grid=(b, t // tb),
        in_specs=[wide(hd)] * 3 + [wide(_SMALL_W)],
        out_specs=[wide(hd)] * 4 + [wide(qkw), wide(_SMALL_W)],
        out_shape=[jax.ShapeDtypeStruct((b, t, hd), F32)] * 4
        + [jax.ShapeDtypeStruct((b, t, qkw), F32), jax.ShapeDtypeStruct((b, t, _SMALL_W), F32)],
        compiler_params=_cparams(("parallel", "parallel")),
        name="gdn_ut",
    )(q, k, v, bg)
    nb = min(b, _GDN_SEQ_PER_STEP)
    grp = lambda gi, i: (gi, i, 0)
    st = lambda gi, i: (gi, 0, 0, 0)
    seqs = lambda n: pl.BlockSpec((nb, tb, n), grp)
    return pl.pallas_call(
        functools.partial(_gdn_scan_body, n_chunks=nc, n_seq=nb, c_len=c_len),
        grid=(b // nb, t // tb),
        in_specs=[seqs(hd)] * 4 + [seqs(qkw), seqs(_SMALL_W), seqs(hd),
                                   pl.BlockSpec((nb, GDN_HEADS, GDN_DK, GDN_DV), st),
                                   pl.BlockSpec((1, GDN_DV), lambda gi, i: (0, 0))],
        out_specs=[seqs(hd), pl.BlockSpec((nb, GDN_HEADS, GDN_DK, GDN_DV), st)],
        out_shape=[jax.ShapeDtypeStruct((b, t, hd), F32),
                   jax.ShapeDtypeStruct((b, GDN_HEADS, GDN_DK, GDN_DV), F32)],
        scratch_shapes=[pltpu.VMEM((nb, GDN_HEADS, GDN_DK, GDN_DV), F32)],
        compiler_params=_cparams(("parallel", "arbitrary")),
        name="gdn_scan",
    )(u, w, qd, kd, qk, gl, z, s0, norm_w)


_SEG_PER_PAGE = PAGE_ROWS // CMP_STRIDE
_CMP_PAGES_PER_STEP = 32


def _compress_body(pt_ref, *refs, pages_per_seq, npg, feature_major):
    page_refs = refs[:npg]
    pe_ref, w1_ref, w2_ref, o_ref, carry_sc, rows_sc = refs[npg:]
    step = pl.program_id(0)
    first = (step % (pages_per_seq // npg)) == 0
    for p, r in enumerate(page_refs):
        for kv in range(2):
            half = slice(kv * LANES, (kv + 1) * LANES)
            rows_sc[kv, p * PAGE_ROWS:(p + 1) * PAGE_ROWS, :] = r[half, :].T if feature_major else r[:, half]
    n_rows = npg * _SEG_PER_PAGE
    n_m = CMP_LEN // CMP_STRIDE
    gh = NSA_GROUPS * CMP_HID
    acc = [[jnp.zeros((n_rows, gh), F32) for _ in range(2)] for _ in range(n_m)]
    for sp in range(CMP_STRIDE // 2):
        for kv in range(2):
            x = jnp.concatenate([rows_sc[kv, pl.ds(2 * sp + i, n_rows, stride=CMP_STRIDE), :] for i in range(2)],
                                axis=1)
            for m in range(n_m):
                xm = (x + pe_ref[m, sp, kv:kv + 1, :]).astype(BF16)
                acc[m][kv] = acc[m][kv] + _dot(xm, w1_ref[m, sp, kv])
    part = [jnp.concatenate(a, axis=1) for a in acc]
    prev = jnp.where(first, 0.0, carry_sc[...])
    rows = lax.broadcasted_iota(jnp.int32, (n_rows, 1), 0)
    shifted = jnp.where(rows == 0, prev, pltpu.roll(part[0], 1, 0))
    carry_sc[...] = part[0][n_rows - 1:n_rows, :]
    hid = _silu(shifted + part[1]).astype(BF16)
    for kv in range(2):
        o_ref[:, kv * LANES:(kv + 1) * LANES] = _dot(hid[:, kv * gh:(kv + 1) * gh], w2_ref[kv]).astype(BF16)


def _compress(pool, table, n_seq, pe, w1bd, w2bd, feature_major):
    pages_per_seq = table.shape[0] // n_seq
    npg = min(_CMP_PAGES_PER_STEP, pages_per_seq)
    n_steps = table.shape[0] // npg
    page_spec = lambda p: pl.BlockSpec((None,) + pool.shape[1:], lambda i, pt: (pt[i * npg + p], 0, 0))
    fixed = lambda nd: (lambda i, pt: (0,) * nd)
    out = pl.pallas_call(
        functools.partial(_compress_body, pages_per_seq=pages_per_seq, npg=npg, feature_major=feature_major),
        grid_spec=pltpu.PrefetchScalarGridSpec(
            num_scalar_prefetch=1,
            grid=(n_steps,),
            in_specs=[page_spec(p) for p in range(npg)]
            + [pl.BlockSpec(pe.shape, fixed(4)), pl.BlockSpec(w1bd.shape, fixed(5)),
               pl.BlockSpec(w2bd.shape, fixed(3))],
            out_specs=pl.BlockSpec((npg * _SEG_PER_PAGE, KV_ROW), lambda i, pt: (i, 0)),
            scratch_shapes=[pltpu.VMEM((1, 2 * NSA_GROUPS * CMP_HID), F32),
                            pltpu.VMEM((2, npg * PAGE_ROWS, LANES), F32)],
        ),
        out_shape=jax.ShapeDtypeStruct((n_steps * npg * _SEG_PER_PAGE, KV_ROW), BF16),
        compiler_params=_cparams(("arbitrary",)),
        name="nsa_compress",
    )(table, *([pool] * npg), pe, w1bd, w2bd)
    return out.reshape(n_seq, pages_per_seq * _SEG_PER_PAGE, KV_ROW)


def _compress_weights(cmp_pos, cmp_w1, cmp_w2):
    n_m = CMP_LEN // CMP_STRIDE
    pe = cmp_pos.reshape(2, n_m, CMP_STRIDE, 1, NSA_DH)
    pe = jnp.broadcast_to(pe, (2, n_m, CMP_STRIDE, NSA_GROUPS, NSA_DH))
    gd = NSA_GROUPS * NSA_DH
    pe = jnp.transpose(pe.reshape(2, n_m, CMP_STRIDE // 2, 2, gd), (1, 2, 0, 3, 4)).reshape(
        n_m, CMP_STRIDE // 2, 2, 2 * gd)
    w1 = cmp_w1.reshape(2, n_m, CMP_STRIDE // 2, 2, NSA_DH, CMP_HID)
    w1 = jnp.transpose(w1, (1, 2, 0, 3, 4, 5))
    eye = jnp.eye(NSA_GROUPS, dtype=F32)
    w1bd = jnp.einsum('mskidh,gG->mskigdGh', w1, eye).reshape(
        n_m, CMP_STRIDE // 2, 2, 2 * gd, NSA_GROUPS * CMP_HID).astype(BF16)
    w2bd = jnp.einsum('khd,gG->kghGd', cmp_w2, eye).reshape(
        2, NSA_GROUPS * CMP_HID, NSA_GROUPS * NSA_DH).astype(BF16)
    return pe, w1bd, w2bd


def _overlap_matrix(n_rows, n_blk_lanes):
    ci = (np.arange(n_rows)[:, None] - 1) * CMP_STRIDE
    sj = np.arange(n_blk_lanes)[None, :] * SLC_BLK
    ov = (ci < sj + SLC_BLK) & (ci + CMP_LEN > sj) & (ci >= 0)
    return jnp.asarray(ov.astype(np.float32))


_SEL_KEY_BLOCK = 1024


_MASK_BIG = 1e30
_VAL_ROWS = NSA_DH + 16


def _nsa_prompt_body(qt_ref, smt_ref, ck_ref, cvt_ref, ska_ref, sva_ref, wk_ref, wvt_ref, ovt_ref, o_ref,
                     m_sc, acc_sc, s_sc):
    iq = pl.program_id(1)
    qb = Q_BLK
    t0 = iq * qb
    n_cmp_rows = ck_ref.shape[0]
    n_blk = ovt_ref.shape[0]
    w_rows = WINDOW + qb
    gd = NSA_GROUPS * NSA_DH
    t_row = t0 + lax.broadcasted_iota(jnp.int32, (1, qb), 1)
    gates = jax.nn.sigmoid(smt_ref[...])
    ci = lax.broadcasted_iota(jnp.int32, (n_cmp_rows, 1), 0)
    cmp_mask = (ci >= 1) & (CMP_STRIDE * ci + (CMP_LEN - CMP_STRIDE - 1) <= t_row)
    blk = lax.broadcasted_iota(jnp.int32, (n_blk, 1), 0)
    forced = (blk == t_row // SLC_BLK) | (blk == 0)
    valid = blk * SLC_BLK <= t_row
    start = pl.multiple_of(jnp.maximum(t0 - WINDOW, 0), qb)
    wpos = start + lax.broadcasted_iota(jnp.int32, (w_rows, 1), 0)
    dlt = t_row - wpos
    win_mask = (dlt >= 0) & (dlt < WINDOW)
    kblk = min(_SEL_KEY_BLOCK, ska_ref.shape[0])
    key_col = lax.broadcasted_iota(jnp.int32, (kblk, 1), 0)
    scale = NSA_DH ** -0.5 * math.log2(math.e)
    heads = lambda x: [x[:, r * qb:(r + 1) * qb] for r in range(NSA_REP)]

    qts, o_cs, imps = [], [], []
    for g in range(NSA_GROUPS):
        vrows = slice(g * NSA_DH, (g + 1) * NSA_DH)
        qt = jnp.concatenate([qt_ref[(g * NSA_REP + r) * NSA_DH:(g * NSA_REP + r + 1) * NSA_DH, :]
                              for r in range(NSA_REP)], axis=1) * scale
        qt = jnp.concatenate([qt, jnp.zeros_like(qt)] if g == 0 else [jnp.zeros_like(qt), qt], axis=0)
        qt = qt.astype(BF16)
        s_c = _dot(ck_ref[...], qt)
        p_c = [_masked_softmax(s, cmp_mask, axis=0, exp=jnp.exp2) for s in heads(s_c)]
        o_cs.append(_dot(cvt_ref[vrows, :], jnp.concatenate(p_c, axis=1).astype(BF16)))
        imp = _dot(ovt_ref[...], p_c[0] + p_c[1] + p_c[2] + p_c[3], HIGHEST)
        imps.append(jnp.where(forced, 1e9, jnp.where(valid, imp, -1.0)))
        qts.append(qt)
    sel_all = _select_top(jnp.concatenate(imps, axis=1), blk, n_blk, axis=0)

    for g in range(NSA_GROUPS):
        vrows = slice(g * NSA_DH, (g + 1) * NSA_DH)
        qt = qts[g]
        o_c = o_cs[g]
        sel = sel_all[:, g * qb:(g + 1) * qb]

        bias = ((sel - 1.0) * _MASK_BIG).astype(BF16)
        rhs = jnp.concatenate([qt, jnp.concatenate([bias] * NSA_REP, axis=1)], axis=0)
        m_sc[...] = jnp.full(m_sc.shape, NEG, F32)
        acc_sc[...] = jnp.zeros(acc_sc.shape, F32)

        def scores(kb):
            return _dot(ska_ref[pl.ds(pl.multiple_of(kb * kblk, kblk), kblk), :], rhs)

        def absorb(kb, s_s, causal):
            r0 = pl.multiple_of(kb * kblk, kblk)
            m_old = m_sc[...]
            m_new, es = [], []
            for r, s in enumerate(heads(s_s)):
                if causal:
                    s = jnp.where(r0 + key_col <= t_row, s, NEG)
                m_r = jnp.maximum(m_old[:, r * qb:(r + 1) * qb], jnp.max(s, axis=0, keepdims=True))
                es.append(jnp.exp2(s - m_r).astype(BF16))
                m_new.append(m_r)
            m_new = jnp.concatenate(m_new, axis=1)
            corr = jnp.exp2(m_old - m_new)
            m_sc[...] = m_new
            acc_sc[...] = corr * acc_sc[...] + _dot(sva_ref[g, :, pl.ds(r0, kblk)], jnp.concatenate(es, axis=1))

        n_steps = (t0 + qb + kblk - 1) // kblk
        s_sc[0] = scores(0)

        def kv_step(kb, carry):
            s_next = scores(kb + 1)
            absorb(kb, s_sc[kb % 2], False)
            s_sc[(kb + 1) % 2] = s_next
            return carry

        lax.fori_loop(0, n_steps - 1, kv_step, 0)
        absorb(n_steps - 1, s_sc[(n_steps - 1) % 2], True)
        o_s = acc_sc[0:NSA_DH, :] * (1.0 / jnp.maximum(acc_sc[NSA_DH:NSA_DH + 1, :], 1e-30))
        s_w = _dot(wk_ref[pl.ds(start, w_rows), :], qt)
        p_w = jnp.concatenate([_masked_softmax(s, win_mask, axis=0, exp=jnp.exp2).astype(BF16) for s in heads(s_w)],
                              axis=1)
        o_w = _dot(wvt_ref[vrows, pl.ds(start, w_rows)], p_w)
        for r in range(NSA_REP):
            h = g * NSA_REP + r
            row0 = 2 * GDN_HEADS + 3 * h
            cs = slice(r * qb, (r + 1) * qb)
            o_ref[h * NSA_DH:(h + 1) * NSA_DH, :] = (
                gates[row0:row0 + 1, :] * o_c[:, cs] + gates[row0 + 1:row0 + 2, :] * o_s[:, cs]
                + gates[row0 + 2:row0 + 3, :] * o_w[:, cs])


def _nsa_prompt(q_b, small, cmp_rows, kv_s, kv_w):
    b, t, _ = q_b.shape
    gd = NSA_GROUPS * NSA_DH
    n_blk = max(t // SLC_BLK, LANES)
    ovt = _overlap_matrix(cmp_rows.shape[1], n_blk).T
    tr = lambda a: jnp.swapaxes(a, 1, 2)
    keys = lambda kv: kv[:, :, :gd].astype(BF16)
    vals_t = lambda kv: tr(kv[:, :, gd:]).astype(BF16)
    blk_onehot = jnp.asarray(np.arange(t)[:, None] // SLC_BLK == np.arange(n_blk)[None, :], BF16)
    keys_aug = jnp.concatenate([keys(kv_s), jnp.broadcast_to(blk_onehot, (b, t, n_blk))], axis=2)
    v_t = vals_t(kv_s).reshape(b, NSA_GROUPS, NSA_DH, t)
    extra = jnp.zeros((b, NSA_GROUPS, _VAL_ROWS - NSA_DH, t), BF16).at[:, :, 0].set(1.0)
    vals_aug = jnp.concatenate([v_t, extra], axis=2)
    whole = lambda bi, i: (bi, 0, 0)
    tok = lambda bi, i: (bi, 0, i)
    rq = NSA_REP * Q_BLK
    out_t = pl.pallas_call(
        _nsa_prompt_body,
        grid=(b, t // Q_BLK),
        in_specs=[pl.BlockSpec((None, NSA_HEADS * NSA_DH, Q_BLK), tok),
                  pl.BlockSpec((None, _SMALL_W, Q_BLK), tok),
                  pl.BlockSpec((None, cmp_rows.shape[1], gd), whole),
                  pl.BlockSpec((None, gd, cmp_rows.shape[1]), whole),
                  pl.BlockSpec((None, t, gd + n_blk), whole),
                  pl.BlockSpec((None, NSA_GROUPS, _VAL_ROWS, t), lambda bi, i: (bi, 0, 0, 0)),
                  pl.BlockSpec((None, t, gd), whole), pl.BlockSpec((None, gd, t), whole),
                  pl.BlockSpec(ovt.shape, lambda bi, i: (0, 0))],
        out_specs=pl.BlockSpec((None, NSA_HEADS * NSA_DH, Q_BLK), tok),
        out_shape=jax.ShapeDtypeStruct((b, NSA_HEADS * NSA_DH, t), F32),
        scratch_shapes=[pltpu.VMEM((1, rq), F32), pltpu.VMEM((_VAL_ROWS, rq), F32),
                        pltpu.VMEM((2, min(_SEL_KEY_BLOCK, t), rq), F32)],
        compiler_params=_cparams(("parallel", "arbitrary")),
        name="nsa_prompt",
    )(tr(q_b), tr(small), cmp_rows[:, :, :gd], tr(cmp_rows[:, :, gd:]), keys_aug, vals_aug,
      keys(kv_w), vals_t(kv_w), ovt)
    return tr(out_t)


_DEC_PAGES_PER_STEP = 16


def _nsa_decode_body(pt_ref, qbd_ref, sm_ref, cmp_ref, wint_ref, neww_ref, news_ref, ov_ref, *refs, n_past_blk,
                     npg):
    page_refs = refs[:npg]
    o_ref, sel_sc, m_sc, l_sc, acc_sc, oc_sc, ow_sc = refs[npg:]
    j = pl.program_id(1)
    n_blk = ov_ref.shape[1]
    gd = NSA_GROUPS * NSA_DH
    qf = qbd_ref[...] * (NSA_DH ** -0.5)
    q = qf.astype(BF16)
    head = lax.broadcasted_iota(jnp.int32, (NSA_HEADS, gd), 0)
    lane = lax.broadcasted_iota(jnp.int32, (NSA_HEADS, gd), 1)
    own = (lane // NSA_DH == head // NSA_REP).astype(F32)

    @pl.when(j == 0)
    def _():
        n_rows = cmp_ref.shape[0]
        ci = lax.broadcasted_iota(jnp.int32, (1, n_rows), 1)
        p_c = _masked_softmax(_dot_nt(q, cmp_ref[:, 0:gd]), ci >= 1)
        oc_sc[...] = _dot(p_c.astype(BF16), cmp_ref[:, gd:2 * gd]) * own
        hr = lax.broadcasted_iota(jnp.int32, (NSA_HEADS, NSA_HEADS), 0) // NSA_REP
        hc = lax.broadcasted_iota(jnp.int32, (NSA_HEADS, NSA_HEADS), 1) // NSA_REP
        p_grp = _dot((hr == hc).astype(F32), p_c, HIGHEST)
        imp = _dot(p_grp, ov_ref[...], HIGHEST)
        blk = lax.broadcasted_iota(jnp.int32, (1, n_blk), 1)
        imp = jnp.where((blk == n_past_blk) | (blk == 0), 1e9, jnp.where(blk < n_past_blk, imp, -2e38))
        sel_sc[...] = _select_top(imp, blk, n_blk)
        n_win = wint_ref.shape[1]
        wi = lax.broadcasted_iota(jnp.int32, (1, n_win), 1)
        neww = neww_ref[...]
        s_w = jnp.where(wi >= 1, _dot(q, wint_ref[0:gd, :].astype(BF16)), NEG)
        s_cur = jnp.sum(qf * neww[:, 0:gd], axis=-1, keepdims=True)
        m_w = jnp.maximum(jnp.max(s_w, axis=-1, keepdims=True), s_cur)
        e_w = jnp.where(wi >= 1, jnp.exp(s_w - m_w), 0.0)
        e_cur = jnp.exp(s_cur - m_w)
        den = jnp.maximum(jnp.sum(e_w, axis=-1, keepdims=True) + e_cur, 1e-30)
        pv_w = _dot_nt(e_w.astype(BF16), wint_ref[gd:2 * gd, :].astype(BF16)) + e_cur * neww[:, gd:2 * gd]
        ow_sc[...] = pv_w * (1.0 / den) * own
        m_sc[...] = jnp.full(m_sc.shape, NEG, F32)
        l_sc[...] = jnp.zeros(l_sc.shape, F32)
        acc_sc[...] = jnp.zeros(acc_sc.shape, F32)

    n_keys = npg * PAGE_ROWS
    kt = jnp.concatenate([r[0:gd, :] for r in page_refs], axis=1).astype(BF16)
    vt = jnp.concatenate([r[gd:2 * gd, :] for r in page_refs], axis=1).astype(BF16)
    e_row = lax.broadcasted_iota(jnp.int32, (n_blk, n_keys), 0)
    e_col = lax.broadcasted_iota(jnp.int32, (n_blk, n_keys), 1) // SLC_BLK
    first_blk = (PAGE_ROWS // SLC_BLK) * (j * npg)
    msk = _dot(sel_sc[...].astype(BF16), (e_row == first_blk + e_col).astype(BF16)) > 0.5
    sc = jnp.where(msk, _dot(q, kt), NEG)
    m_old = m_sc[...]
    m_new = jnp.maximum(m_old, jnp.max(sc, axis=-1, keepdims=True))
    corr = jnp.exp(m_old - m_new)
    e = jnp.where(msk, jnp.exp(sc - m_new), 0.0)
    l_sc[...] = corr * l_sc[...] + jnp.sum(e, axis=-1, keepdims=True)
    acc_sc[...] = corr * acc_sc[...] + _dot_nt(e.astype(BF16), vt)
    m_sc[...] = m_new

    @pl.when(j == pl.num_programs(1) - 1)
    def _():
        new = news_ref[...]
        s_new = jnp.sum(qf * new[:, 0:gd], axis=-1, keepdims=True)
        m_old = m_sc[...]
        m_new = jnp.maximum(m_old, s_new)
        corr = jnp.exp(m_old - m_new)
        e = jnp.exp(s_new - m_new)
        l = corr * l_sc[...] + e
        acc = corr * acc_sc[...] + e * new[:, gd:2 * gd]
        o_s = acc / jnp.maximum(l, 1e-30) * own
        gates = jax.nn.sigmoid(sm_ref[...])
        glane = lax.broadcasted_iota(jnp.int32, (NSA_HEADS, _SMALL_W), 1)
        ghead = lax.broadcasted_iota(jnp.int32, (NSA_HEADS, _SMALL_W), 0)
        gate = [jnp.sum(jnp.where(glane == 2 * GDN_HEADS + 3 * ghead + br, gates, 0.0), axis=-1, keepdims=True)
                for br in range(3)]
        o = gate[0] * oc_sc[...] + gate[1] * o_s + gate[2] * ow_sc[...]
        o_ref[...] = o[:, 0:NSA_DH] + o[:, NSA_DH:gd]


def _nsa_decode(qbd, small, cmp_rows, win_t, new_w, new_s, pool, table):
    b = qbd.shape[0]
    pages_per_seq = table.shape[0] // b
    npg = min(_DEC_PAGES_PER_STEP, pages_per_seq)
    n_past_blk = pages_per_seq * (PAGE_ROWS // SLC_BLK)
    n_blk = -(-(n_past_blk + 1) // LANES) * LANES
    ov = _overlap_matrix(cmp_rows.shape[1], n_blk)
    per_seq = lambda bi, j, pt: (bi, 0, 0)
    page_spec = lambda p: pl.BlockSpec(
        (None, KV_ROW, PAGE_ROWS), lambda bi, j, pt: (pt[bi * pages_per_seq + j * npg + p], 0, 0))
    gd = NSA_GROUPS * NSA_DH
    out = pl.pallas_call(
        functools.partial(_nsa_decode_body, n_past_blk=n_past_blk, npg=npg),
        grid_spec=pltpu.PrefetchScalarGridSpec(
            num_scalar_prefetch=1,
            grid=(b, pages_per_seq // npg),
            in_specs=[pl.BlockSpec((None, NSA_HEADS, gd), per_seq),
                      pl.BlockSpec((None, 1, _SMALL_W), per_seq),
                      pl.BlockSpec((None,) + cmp_rows.shape[1:], per_seq),
                      pl.BlockSpec((None,) + win_t.shape[1:], per_seq),
                      pl.BlockSpec((None, 1, KV_ROW), per_seq), pl.BlockSpec((None, 1, KV_ROW), per_seq),
                      pl.BlockSpec(ov.shape, lambda bi, j, pt: (0, 0))]
            + [page_spec(p) for p in range(npg)],
            out_specs=pl.BlockSpec((None, NSA_HEADS, NSA_DH), per_seq),
            scratch_shapes=[pltpu.VMEM((NSA_HEADS, n_blk), F32), pltpu.VMEM((NSA_HEADS, 1), F32),
                            pltpu.VMEM((NSA_HEADS, 1), F32), pltpu.VMEM((NSA_HEADS, gd), F32),
                            pltpu.VMEM((NSA_HEADS, gd), F32), pltpu.VMEM((NSA_HEADS, gd), F32)],
        ),
        out_shape=jax.ShapeDtypeStruct((b, NSA_HEADS, NSA_DH), F32),
        compiler_params=_cparams(("parallel", "arbitrary")),
        name="nsa_decode",
    )(table, qbd, small, cmp_rows, win_t, new_w, new_s, ov, *([pool] * npg))
    return out.reshape(b, NSA_HEADS * NSA_DH)


def _outproj_body(oa_ref, ob_ref, gm_ref, x_ref, gt_ref, wa_ref, wb_ref, wo_ref, lg_ref, lb_ref, o_ref):
    ya = _dot(oa_ref[...].astype(BF16), wa_ref[...])
    yb = _dot(ob_ref[...].astype(BF16), wb_ref[...])
    gm = jax.nn.sigmoid(gm_ref[...])
    u = (gm[:, :D_MODEL] * ya + gm[:, D_MODEL:] * yb).astype(BF16)
    y = _dot(u, wo_ref[...])
    xr = ALPHA * x_ref[...] + gt_ref[...] * y
    o_ref[...] = _ln_rows(xr) * lg_ref[...] + lb_ref[...]


def _outproj(o_a, o_b, g_m, x2d, gt, w_up_a, w_up_b, w_out, ln_g, ln_b, tm, rows_per_mod):
    m = x2d.shape[0]
    row = lambda n: pl.BlockSpec((tm, n), lambda i: (i, 0))
    fixed = lambda shape: pl.BlockSpec(shape, lambda i: (0, 0))
    return pl.pallas_call(
        _outproj_body,
        grid=(m // tm,),
        in_specs=[row(o_a.shape[1]), row(o_b.shape[1]), row(2 * D_MODEL), row(D_MODEL),
                  _mod_spec(gt, tm, rows_per_mod),
                  fixed(w_up_a.shape), fixed(w_up_b.shape), fixed(w_out.shape),
                  fixed((1, D_MODEL)), fixed((1, D_MODEL))],
        out_specs=row(D_MODEL),
        out_shape=jax.ShapeDtypeStruct((m, D_MODEL), F32),
        compiler_params=_cparams(("parallel",)),
        name="mixer_outproj",
    )(o_a, o_b, g_m, x2d, gt, w_up_a, w_up_b, w_out, ln_g, ln_b)


_FF_TILE = 256


def _ffn_body(x_ref, sc_ref, sh_ref, gt_ref, wg_ref, wu_ref, wd_ref, lg_ref, lb_ref, o_ref, h_sc, acc_sc):
    f = pl.program_id(1)

    @pl.when(f == 0)
    def _():
        h = _ln_rows(x_ref[...]) * (1.0 + sc_ref[...]) + sh_ref[...]
        h_sc[...] = h.astype(BF16)
        acc_sc[...] = jnp.zeros(acc_sc.shape, F32)

    hb = h_sc[...]
    hid = _silu(_dot(hb, wg_ref[...])) * _dot(hb, wu_ref[...])
    acc_sc[...] += _dot(hid.astype(BF16), wd_ref[...])

    @pl.when(f == pl.num_programs(1) - 1)
    def _():
        xr = ALPHA * x_ref[...] + gt_ref[...] * acc_sc[...]
        o_ref[...] = _ln_rows(xr) * lg_ref[...] + lb_ref[...]


def _ffn(x2d, sc, sh, gt, w_gu, w_down, ln_g, ln_b, tm, rows_per_mod):
    m = x2d.shape[0]
    tf = _FF_TILE
    n_f = D_FF // tf
    row = pl.BlockSpec((tm, D_MODEL), lambda i, f: (i, 0))
    fixed = lambda shape: pl.BlockSpec(shape, lambda i, f: (0, 0))
    mod = lambda a: _mod_spec(a, tm, rows_per_mod)
    return pl.pallas_call(
        _ffn_body,
        grid=(m // tm, n_f),
        in_specs=[row, mod(sc), mod(sh), mod(gt),
                  pl.BlockSpec((D_MODEL, tf), lambda i, f: (0, f)),
                  pl.BlockSpec((D_MODEL, tf), lambda i, f: (0, f + n_f)),
                  pl.BlockSpec((tf, D_MODEL), lambda i, f: (f, 0)),
                  fixed((1, D_MODEL)), fixed((1, D_MODEL))],
        out_specs=row,
        out_shape=jax.ShapeDtypeStruct((m, D_MODEL), F32),
        scratch_shapes=[pltpu.VMEM((tm, D_MODEL), BF16), pltpu.VMEM((tm, D_MODEL), F32)],
        compiler_params=_cparams(("parallel", "arbitrary")),
        name="ffn_dense",
    )(x2d, sc, sh, gt, w_gu, w_gu, w_down, ln_g, ln_b)


_MOE_CHUNK = 288
_MOE_FF_TILE = 1408


def _moe_body(x_ref, sc_ref, sh_ref, gt_ref, wr_ref, wg_ref, wu_ref, wd_ref, lg_ref, lb_ref, o_ref,
              h_sc, w_sc, sel_sc, rank_sc, selt_sc, rankt_sc, xg_sc, yg_sc, y_sc, *, chunk):
    e = pl.program_id(1)
    f = pl.program_id(2)
    tb = x_ref.shape[0]
    lane = lax.broadcasted_iota(jnp.int32, (1, LANES), 1)

    @pl.when((e == 0) & (f == 0))
    def _():
        h = _ln_rows(x_ref[...]) * (1.0 + sc_ref[...]) + sh_ref[...]
        h_sc[...] = h.astype(BF16)
        logits = jnp.where(lane < N_EXP, _dot(h, wr_ref[...], HIGHEST), NEG)
        ex = jnp.exp(logits - jnp.max(logits, axis=-1, keepdims=True))
        probs = ex / jnp.sum(ex, axis=-1, keepdims=True)
        lane_f = lane.astype(F32)
        p1 = jnp.max(probs, axis=-1, keepdims=True)
        i1 = jnp.min(jnp.where(probs == p1, lane_f, float(LANES)), axis=-1, keepdims=True)
        rest = jnp.where(lane_f == i1, -1.0, probs)
        p2 = jnp.max(rest, axis=-1, keepdims=True)
        i2 = jnp.min(jnp.where(rest == p2, lane_f, float(LANES)), axis=-1, keepdims=True)
        w_sc[...] = (jnp.where(lane_f == i1, p1, 0.0) + jnp.where(lane_f == i2, p2, 0.0)) / (p1 + p2)
        sel = ((lane_f == i1) | (lane_f == i2)).astype(F32)
        earlier = (lax.broadcasted_iota(jnp.int32, (tb, tb), 1)
                   < lax.broadcasted_iota(jnp.int32, (tb, tb), 0)).astype(BF16)
        rank = _dot(earlier, sel.astype(BF16))
        sel_sc[...] = sel
        rank_sc[...] = rank
        selt_sc[...] = sel.T
        rankt_sc[...] = rank.T
        y_sc[...] = jnp.zeros(y_sc.shape, F32)

    sel_row = selt_sc[pl.ds(e, 1), :]
    rank_row = rankt_sc[pl.ds(e, 1), :]
    count = jnp.sum(sel_row).astype(jnp.int32)
    n_chunks = (count + chunk - 1) // chunk

    @pl.when(f == 0)
    def _():
        def gather(c, carry):
            slot = (c * chunk + lax.broadcasted_iota(jnp.int32, (chunk, 1), 0)).astype(F32)
            pick = ((rank_row == slot) & (sel_row > 0.5)).astype(BF16)
            xg_sc[pl.ds(pl.multiple_of(c * chunk, chunk), chunk), :] = _dot(pick, h_sc[...]).astype(BF16)
            return carry

        lax.fori_loop(0, n_chunks, gather, 0)

    def expert(c, carry):
        rows = pl.ds(pl.multiple_of(c * chunk, chunk), chunk)
        xg = xg_sc[rows, :]
        hid = _silu(_dot(xg, wg_ref[...])) * _dot(xg, wu_ref[...])
        part = _dot(hid.astype(BF16), wd_ref[...])

        @pl.when(f == 0)
        def _():
            yg_sc[rows, :] = part

        @pl.when(f > 0)
        def _():
            yg_sc[rows, :] += part

        return carry

    lax.fori_loop(0, n_chunks, expert, 0)

    @pl.when(f == pl.num_programs(2) - 1)
    def _():
        pick_e = lane == e
        sel_col = jnp.sum(jnp.where(pick_e, sel_sc[...], 0.0), axis=-1, keepdims=True)
        rank_col = jnp.sum(jnp.where(pick_e, rank_sc[...], 0.0), axis=-1, keepdims=True)
        w_col = jnp.sum(jnp.where(pick_e, w_sc[...], 0.0), axis=-1, keepdims=True)

        def scatter(c, carry):
            slot = (c * chunk + lax.broadcasted_iota(jnp.int32, (1, chunk), 1)).astype(F32)
            place = ((rank_col == slot) & (sel_col > 0.5)).astype(BF16)
            rows = pl.ds(pl.multiple_of(c * chunk, chunk), chunk)
            y_sc[...] += w_col * _dot(place, yg_sc[rows, :].astype(BF16))
            return carry

        lax.fori_loop(0, n_chunks, scatter, 0)

    @pl.when((e == pl.num_programs(1) - 1) & (f == pl.num_programs(2) - 1))
    def _():
        xr = ALPHA * x_ref[...] + gt_ref[...] * y_sc[...]
        o_ref[...] = _ln_rows(xr) * lg_ref[...] + lb_ref[...]


def _moe(x2d, sc, sh, gt, w_router, w_gu, w_down, ln_g, ln_b, tm, rows_per_mod):
    m = x2d.shape[0]
    n_e = w_gu.shape[0]
    tf = _MOE_FF_TILE
    n_f = D_FF // tf
    chunk = min(_MOE_CHUNK, tm)
    slot_rows = -(-tm // chunk) * chunk
    row = pl.BlockSpec((tm, D_MODEL), lambda i, e, f: (i, 0), pipeline_mode=pl.Buffered(1))
    fixed = lambda shape: pl.BlockSpec(shape, lambda i, e, f: (0, 0))
    mod = lambda a: _mod_spec(a, tm, rows_per_mod)
    per_tok = lambda: pltpu.VMEM((tm, LANES), F32)
    per_exp = lambda: pltpu.VMEM((LANES, tm), F32)
    return pl.pallas_call(
        functools.partial(_moe_body, chunk=chunk),
        grid=(m // tm, n_e, n_f),
        in_specs=[row, mod(sc), mod(sh), mod(gt), fixed(w_router.shape),
                  pl.BlockSpec((None, D_MODEL, tf), lambda i, e, f: (e, 0, f)),
                  pl.BlockSpec((None, D_MODEL, tf), lambda i, e, f: (e, 0, f + n_f)),
                  pl.BlockSpec((None, tf, D_MODEL), lambda i, e, f: (e, f, 0)),
                  fixed((1, D_MODEL)), fixed((1, D_MODEL))],
        out_specs=row,
        out_shape=jax.ShapeDtypeStruct((m, D_MODEL), F32),
        scratch_shapes=[pltpu.VMEM((tm, D_MODEL), BF16), per_tok(), per_tok(), per_tok(), per_exp(), per_exp(),
                        pltpu.VMEM((slot_rows, D_MODEL), BF16), pltpu.VMEM((slot_rows, D_MODEL), F32),
                        pltpu.VMEM((tm, D_MODEL), F32)],
        compiler_params=_cparams(("parallel", "arbitrary", "arbitrary")),
        name="moe_routed",
    )(x2d, sc, sh, gt, w_router, w_gu, w_gu, w_down, ln_g, ln_b)


def _permute_w_in(w):
    sizes = [C_QKV, GDN_HEADS * GDN_DV, GDN_HEADS, GDN_HEADS, NSA_HEADS * NSA_DH, KV_ROW, KV_ROW, KV_ROW,
             3 * NSA_HEADS, 2 * D_MODEL]
    offs = np.cumsum([0] + sizes)
    qkv, z, b_a, a_a, q_b, kv_c, kv_s, kv_w, g_b, g_m = (w[:, offs[i]:offs[i + 1]] for i in range(len(sizes)))
    pad = jnp.zeros((w.shape[0], _SMALL_W - 2 * GDN_HEADS - 3 * NSA_HEADS), w.dtype)
    return jnp.concatenate([qkv, z, g_m, q_b, kv_c, kv_s, kv_w, b_a, a_a, g_b, pad], axis=1).astype(BF16)


def _layer_weights(l, w_in, conv_w, a_log, dt_bias, norm_w_a, cmp_pos, cmp_w1, cmp_w2, w_up_a, w_up_b, w_out,
                   ln_g, ln_b):
    lane_pad = lambda v: jnp.zeros((1, _SMALL_W), F32).at[0, GDN_HEADS:2 * GDN_HEADS].set(v)
    pe, w1bd, w2bd = _compress_weights(cmp_pos[l], cmp_w1[l], cmp_w2[l])
    return dict(
        w_in=_permute_w_in(w_in[l]), conv_w=conv_w[l], alog=lane_pad(a_log[l]), dt=lane_pad(dt_bias[l]),
        norm_w=norm_w_a[l].reshape(1, GDN_DV), pe=pe, w1bd=w1bd, w2bd=w2bd,
        w_up_a=w_up_a[l].astype(BF16), w_up_b=w_up_b[l].astype(BF16), w_out=w_out[l].astype(BF16),
        ln_g0=ln_g[l, 0].reshape(1, D_MODEL), ln_b0=ln_b[l, 0].reshape(1, D_MODEL),
        ln_g1=ln_g[l, 1].reshape(1, D_MODEL), ln_b1=ln_b[l, 1].reshape(1, D_MODEL))


def _mixer_common(x2d, mods, wts, b, t, t_pad, conv_buf, s0, tm, rows_per_mod):
    sh_m, sc_m = mods[0], mods[1]
    qkv, z, g_m, q_b, kv_c, kv_s, kv_w, small = _inproj(x2d, sc_m, sh_m, wts['w_in'], tm, rows_per_mod)
    seq = lambda a: a.reshape(b, t, a.shape[-1])
    padt = lambda a: jnp.pad(seq(a), ((0, 0), (0, t_pad - t), (0, 0)))
    buf8 = jnp.pad(conv_buf, ((0, 0), (SUBLANES - (CONV_TAPS - 1), 0), (0, 0)))
    qa, ka, va, bg = _gdn_prep(padt(qkv), buf8, wts['conv_w'], padt(small), wts['alog'], wts['dt'], t)
    o_a, s_new = _gdn(qa, ka, va, bg, padt(z), s0, wts['norm_w'])
    o_a = o_a[:, :t].reshape(b * t, GDN_HEADS * GDN_DV)
    conv_new = jnp.concatenate([conv_buf, seq(qkv)], axis=1)[:, -(CONV_TAPS - 1):]
    return (g_m, q_b, kv_c, kv_s, kv_w, small), o_a, s_new, conv_new


def _rows5(a, b, t):
    return a.reshape(b, t, 2, NSA_GROUPS, NSA_DH)


def _feature_major(rows):
    nd = rows.ndim
    return jnp.transpose(rows, tuple(range(nd - 4)) + (nd - 3, nd - 2, nd - 1, nd - 4))


def _prompt_layer(x, mods, wts, ffn_args, routed):
    b, t, _ = x.shape
    x2d = x.reshape(b * t, D_MODEL)
    tm = 256
    zeros_buf = jnp.zeros((b, CONV_TAPS - 1, C_QKV), F32)
    zeros_s = jnp.zeros((b, GDN_HEADS, GDN_DK, GDN_DV), F32)
    (g_m, q_b, kv_c, kv_s, kv_w, small), o_a, s_new, conv_new = _mixer_common(
        x2d, mods, wts, b, t, t, zeros_buf, zeros_s, tm, t)
    pages = (b * t) // PAGE_ROWS
    cmp_rows = _compress(kv_c.reshape(pages, PAGE_ROWS, KV_ROW), jnp.arange(pages, dtype=jnp.int32), b,
                         wts['pe'], wts['w1bd'], wts['w2bd'], False)
    seq = lambda a: a.reshape(b, t, a.shape[-1])
    o_b = _nsa_prompt(seq(q_b), seq(small), cmp_rows, seq(kv_s), seq(kv_w))
    x1 = _outproj(o_a, o_b.reshape(b * t, -1), g_m, x2d, mods[2], wts['w_up_a'], wts['w_up_b'], wts['w_out'],
                  wts['ln_g0'], wts['ln_b0'], 512, t)
    x2 = (_moe if routed else _ffn)(x1, mods[4], mods[3], mods[5], *ffn_args, wts['ln_g1'], wts['ln_b1'],
                                    min(1024, t), t)
    win_new = _rows5(kv_w, b, t)[:, t - min(WINDOW, t):]
    return x2.reshape(b, t, D_MODEL), (s_new, conv_new, _rows5(kv_c, b, t), _rows5(kv_s, b, t), win_new)


def _sample_layer(x, mods, wts, ffn_args, routed, conv_buf, s0, pool_c, pool_s, win_buf, table):
    b = x.shape[0]
    x2d = x.reshape(b, D_MODEL)
    (g_m, q_b, kv_c, kv_s, kv_w, small), o_a, s_new, conv_new = _mixer_common(
        x2d, mods, wts, b, 1, _GDN_SHORT_CHUNK, conv_buf, s0, b, b)
    cmp_rows = _compress(pool_c, table, b, wts['pe'], wts['w1bd'], wts['w2bd'], True)
    win_new = jnp.concatenate([win_buf.reshape(b, -1, KV_ROW)[:, 1:], kv_w[:, None, :]], axis=1)
    win_t = _feature_major(win_buf).reshape(b, KV_ROW, -1)
    q4 = q_b.reshape(b, NSA_GROUPS, NSA_REP, 1, NSA_DH)
    eye = jnp.eye(NSA_GROUPS, dtype=F32).reshape(NSA_GROUPS, 1, NSA_GROUPS, 1)
    qbd = (q4 * eye).reshape(b, NSA_HEADS, NSA_GROUPS * NSA_DH)
    o_b = _nsa_decode(qbd, small.reshape(b, 1, _SMALL_W), cmp_rows, win_t, kv_w.reshape(b, 1, KV_ROW),
                      kv_s.reshape(b, 1, KV_ROW), pool_s, table)
    x1 = _outproj(o_a, o_b, g_m, x2d, mods[2], wts['w_up_a'], wts['w_up_b'], wts['w_out'],
                  wts['ln_g0'], wts['ln_b0'], b, b)
    x2 = (_moe if routed else _ffn)(x1, mods[4], mods[3], mods[5], *ffn_args, wts['ln_g1'], wts['ln_b1'], b, b)
    return x2.reshape(b, 1, D_MODEL), (s_new, conv_new, _rows5(kv_c, b, 1), _rows5(kv_s, b, 1),
                                       _rows5(win_new, b, win_new.shape[1]))


def kernel(x_prompt, x_sample, state_delta, state_conv, cache_cmp_kv, cache_slc_kv, state_win_kv, page_table,
           c_prompt, c_sample, w_ada, b_ada, w_in, conv_w, a_log, dt_bias, norm_w_a, cmp_pos, cmp_w1, cmp_w2,
           w_up_a, w_up_b, w_out, ln_g, ln_b, ffn_w_gu, ffn_w_down, moe_router, moe_w_gu, moe_w_down):
    bp = x_prompt.shape[0]
    db = x_sample.shape[0]
    n_layers = w_in.shape[0]
    n_pool = cache_cmp_kv.shape[1]
    mod_all = _adaln(jnp.concatenate([c_prompt, c_sample], axis=0), w_ada, b_ada)
    table = page_table.reshape(-1)
    pool_c = _feature_major(cache_cmp_kv).reshape(n_layers * n_pool, KV_ROW, PAGE_ROWS)
    pool_s = _feature_major(cache_slc_kv).reshape(n_layers * n_pool, KV_ROW, PAGE_ROWS)
    xp, xs = x_prompt, x_sample
    st_p, st_s = [], []
    for l in range(n_layers):
        wts = _layer_weights(l, w_in, conv_w, a_log, dt_bias, norm_w_a, cmp_pos, cmp_w1, cmp_w2, w_up_a, w_up_b,
                             w_out, ln_g, ln_b)
        routed = l % 2 == 1
        if routed:
            router = jnp.pad(moe_router[l // 2], ((0, 0), (0, LANES - N_EXP)))
            ffn_args = (router, moe_w_gu[l // 2].astype(BF16), moe_w_down[l // 2].astype(BF16))
        else:
            ffn_args = (ffn_w_gu[l // 2].astype(BF16), ffn_w_down[l // 2].astype(BF16))
        mod6 = mod_all[l].reshape(bp + db, 6, D_MODEL)
        mods_p = [mod6[:bp, i].reshape(bp, 1, D_MODEL) for i in range(6)]
        mods_s = [mod6[bp:, i].reshape(1, db, D_MODEL) for i in range(6)]
        xp, sp = _prompt_layer(xp, mods_p, wts, ffn_args, routed)
        xs, ss = _sample_layer(xs, mods_s, wts, ffn_args, routed, state_conv[l], state_delta[l],
                               pool_c, pool_s, state_win_kv[l], table + l * n_pool)
        st_p.append(sp)
        st_s.append(ss)
    stack = lambda sts, i: jnp.stack([s[i] for s in sts])
    return (xp, xs) + tuple(stack(st_p, i) for i in range(5)) + tuple(stack(st_s, i) for i in range(5))
```

```python
import functools
import math

import jax
import jax.numpy as jnp
import numpy as np
from jax import lax
from jax.experimental import pallas as pl
from jax.experimental.pallas import tpu as pltpu

F32 = jnp.float32
BF16 = jnp.bfloat16
HIGHEST = lax.Precision.HIGHEST

D_MODEL = 1024
N_LAYERS = 2
PAGE_ROWS = 128
GDN_HEADS = 4
GDN_DK = 128
GDN_DV = 128
CONV_TAPS = 4
GDN_CHUNK = 64
C_QKV = 2 * GDN_HEADS * GDN_DK + GDN_HEADS * GDN_DV
NSA_HEADS = 8
NSA_GROUPS = 2
NSA_REP = NSA_HEADS // NSA_GROUPS
NSA_DH = 64
KV_ROW = 2 * NSA_GROUPS * NSA_DH
CMP_LEN = 32
CMP_STRIDE = 16
CMP_HID = 256
SLC_BLK = 64
N_SEL = 16
WINDOW = 512
Q_BLK = 128
D_FF = 2816
N_EXP = 8
ALPHA = (2 * N_LAYERS) ** 0.25
EPS = 1e-5
NEG = -1e30

LANES = 128
SUBLANES = 8
VMEM_LIMIT = 56 * 1024 * 1024

_SMALL_W = LANES
IN_OUT_WIDTHS = (C_QKV, GDN_HEADS * GDN_DV, 2 * D_MODEL, NSA_HEADS * NSA_DH, KV_ROW, KV_ROW, KV_ROW, _SMALL_W)


def _cparams(sem):
    return pltpu.CompilerParams(dimension_semantics=sem, vmem_limit_bytes=VMEM_LIMIT)


def _silu(x):
    return x * jax.nn.sigmoid(x)


def _ln_rows(x):
    mu = jnp.mean(x, axis=-1, keepdims=True)
    xc = x - mu
    var = jnp.mean(xc * xc, axis=-1, keepdims=True)
    return xc * lax.rsqrt(var + EPS)


def _dot(a, b, precision=None):
    return jnp.dot(a, b, preferred_element_type=F32, precision=precision)


def _dot_nt(a, b, precision=None):
    return lax.dot_general(a, b, (((1,), (1,)), ((), ())), preferred_element_type=F32, precision=precision)


def _dot_tn(a, b, precision=None):
    return lax.dot_general(a, b, (((0,), (0,)), ((), ())), preferred_element_type=F32, precision=precision)


def _masked_softmax(s, mask, axis=-1, exp=jnp.exp):
    s = jnp.where(mask, s, NEG)
    m = jnp.max(s, axis=axis, keepdims=True)
    e = jnp.where(mask, exp(s - m), 0.0)
    return e * (1.0 / jnp.maximum(jnp.sum(e, axis=axis, keepdims=True), 1e-30))


def _select_top(imp, lane_idx, n_lanes, axis=-1):
    sel = jnp.zeros(imp.shape, F32)
    lane_idx = lane_idx.astype(F32)
    for _ in range(N_SEL):
        mx = jnp.max(imp, axis=axis, keepdims=True)
        idx = jnp.min(jnp.where(imp == mx, lane_idx, float(n_lanes)), axis=axis, keepdims=True)
        hit = lane_idx == idx
        sel = jnp.where(hit, 1.0, sel)
        imp = jnp.where(hit, -3e38, imp)
    return sel


def _adaln_body(c_ref, w_ref, b_ref, o_ref):
    s = _silu(c_ref[...]).astype(BF16)
    o_ref[...] = _dot(s, w_ref[...].astype(BF16)) + b_ref[...]


def _adaln(c_all, w_ada, b_ada):
    n_l, d, n = w_ada.shape
    r = c_all.shape[0]
    tn = 1536
    return pl.pallas_call(
        _adaln_body,
        grid=(n_l, n // tn),
        in_specs=[pl.BlockSpec((r, d), lambda l, j: (0, 0)),
                  pl.BlockSpec((None, d, tn), lambda l, j: (l, 0, j)),
                  pl.BlockSpec((None, 1, tn), lambda l, j: (l, 0, j))],
        out_specs=pl.BlockSpec((None, r, tn), lambda l, j: (l, 0, j)),
        out_shape=jax.ShapeDtypeStruct((n_l, r, n), F32),
        compiler_params=_cparams(("parallel", "parallel")),
        name="adaln_mod",
    )(c_all, w_ada, b_ada.reshape(n_l, 1, n))


def _inproj_body(x_ref, sc_ref, sh_ref, w_ref, *o_refs):
    h = (_ln_rows(x_ref[...]) * (1.0 + sc_ref[...]) + sh_ref[...]).astype(BF16)
    off = 0
    for o_ref, n in zip(o_refs, IN_OUT_WIDTHS):
        o_ref[...] = _dot(h, w_ref[:, off:off + n])
        off += n


def _mod_spec(mod, tm, rows_per_mod):
    if mod.shape[1] == 1:
        return pl.BlockSpec((None, 1, D_MODEL), lambda i, *_: ((i * tm) // rows_per_mod, 0, 0))
    return pl.BlockSpec((None, tm, D_MODEL), lambda i, *_: (0, i, 0))


def _inproj(x2d, sc, sh, w_perm, tm, rows_per_mod):
    m = x2d.shape[0]
    n_tot = w_perm.shape[1]
    return pl.pallas_call(
        _inproj_body,
        grid=(m // tm,),
        in_specs=[pl.BlockSpec((tm, D_MODEL), lambda i: (i, 0)),
                  _mod_spec(sc, tm, rows_per_mod), _mod_spec(sh, tm, rows_per_mod),
                  pl.BlockSpec((D_MODEL, n_tot), lambda i: (0, 0))],
        out_specs=[pl.BlockSpec((tm, n), lambda i: (i, 0)) for n in IN_OUT_WIDTHS],
        out_shape=[jax.ShapeDtypeStruct((m, n), F32) for n in IN_OUT_WIDTHS],
        compiler_params=_cparams(("parallel",)),
        name="ln_inproj",
    )(x2d, sc, sh, w_perm)


def _prep_body(qkv_ref, buf_ref, cw_ref, sm_ref, alog_ref, dt_ref, q_ref, k_ref, v_ref, bg_ref, xp_sc,
               *, tb, t_valid):
    i = pl.program_id(1)

    @pl.when(i == 0)
    def _():
        xp_sc[0:SUBLANES, :] = buf_ref[...]

    x = qkv_ref[...]
    xp_sc[SUBLANES:SUBLANES + tb, :] = x
    y = cw_ref[3:4, :] * x
    for j in range(CONV_TAPS - 1):
        lo = SUBLANES - (CONV_TAPS - 1) + j
        y = y + cw_ref[j:j + 1, :] * xp_sc[lo:lo + tb, :]
    xp_sc[0:SUBLANES, :] = xp_sc[tb:tb + SUBLANES, :]
    a = _silu(y)
    rows = i * tb + lax.broadcasted_iota(jnp.int32, (tb, 1), 0)
    live = rows < t_valid
    hk = GDN_HEADS * GDN_DK
    for h in range(GDN_HEADS):
        qh = a[:, h * GDN_DK:(h + 1) * GDN_DK]
        kh = a[:, hk + h * GDN_DK:hk + (h + 1) * GDN_DK]
        qn = qh * lax.rsqrt(jnp.sum(qh * qh, axis=-1, keepdims=True) + 1e-6) * (GDN_DK ** -0.5)
        kn = kh * lax.rsqrt(jnp.sum(kh * kh, axis=-1, keepdims=True) + 1e-6)
        q_ref[:, h * GDN_DK:(h + 1) * GDN_DK] = jnp.where(live, qn, 0.0)
        k_ref[:, h * GDN_DK:(h + 1) * GDN_DK] = jnp.where(live, kn, 0.0)
    v_ref[...] = jnp.where(live, a[:, 2 * hk:], 0.0)
    sm = sm_ref[...]
    beta = jax.nn.sigmoid(sm)
    z = sm + dt_ref[...]
    softplus = jnp.maximum(z, 0.0) + jnp.log(1.0 + jnp.exp(-jnp.abs(z)))
    g = -jnp.exp(alog_ref[...]) * softplus
    lane = lax.broadcasted_iota(jnp.int32, sm.shape, 1)
    bg_ref[...] = jnp.where(live, jnp.where(lane < GDN_HEADS, beta, g), 0.0)


def _gdn_prep(qkv, buf8, conv_w, small, alog_vec, dt_vec, t_valid):
    b, t, _ = qkv.shape
    tb = min(t, 512)
    hd = GDN_HEADS * GDN_DK
    row = lambda bi, i: (bi, i, 0)
    fixed = lambda bi, i: (0, 0)
    return pl.pallas_call(
        functools.partial(_prep_body, tb=tb, t_valid=t_valid),
        grid=(b, t // tb),
        in_specs=[pl.BlockSpec((None, tb, C_QKV), row),
                  pl.BlockSpec((None, SUBLANES, C_QKV), lambda bi, i: (bi, 0, 0)),
                  pl.BlockSpec((CONV_TAPS, C_QKV), fixed),
                  pl.BlockSpec((None, tb, _SMALL_W), row),
                  pl.BlockSpec((1, _SMALL_W), fixed), pl.BlockSpec((1, _SMALL_W), fixed)],
        out_specs=[pl.BlockSpec((None, tb, hd), row), pl.BlockSpec((None, tb, hd), row),
                   pl.BlockSpec((None, tb, hd), row), pl.BlockSpec((None, tb, _SMALL_W), row)],
        out_shape=[jax.ShapeDtypeStruct((b, t, hd), F32)] * 3 + [jax.ShapeDtypeStruct((b, t, _SMALL_W), F32)],
        scratch_shapes=[pltpu.VMEM((tb + SUBLANES, C_QKV), F32)],
        compiler_params=_cparams(("parallel", "arbitrary")),
        name="gdn_prep",
    )(qkv, buf8, conv_w, small, alog_vec, dt_vec)


def _split_bf16(x):
    hi = x.astype(BF16)
    return hi, (x - hi.astype(F32)).astype(BF16)


def _dot3(a, b, dims):
    ah, al = _split_bf16(a)
    bh, bl = _split_bf16(b)
    d = lambda x, y: lax.dot_general(x, y, dims, preferred_element_type=F32)
    return d(ah, bh) + (d(ah, bl) + d(al, bh))


def _bmm(a, b):
    return _dot3(a, b, (((2,), (1,)), ((0,), (0,))))


def _bmm_nt(a, b):
    return _dot3(a, b, (((2,), (2,)), ((0,), (0,))))


def _gdn_ut_body(q_ref, k_ref, v_ref, bg_ref, u_ref, w_ref, qd_ref, kd_ref, qk_ref, gl_ref, *, n_chunks, c_len):
    nc = n_chunks
    nb = GDN_HEADS * nc
    tb = nc * c_len
    row = lax.broadcasted_iota(jnp.int32, (nb, c_len, c_len), 1)
    col = lax.broadcasted_iota(jnp.int32, (nb, c_len, c_len), 2)
    incl = row >= col
    strict = row > col
    incl_f = incl.astype(F32)
    strict_f = strict.astype(F32)
    eye = (row == col).astype(F32)
    bg = bg_ref[...]
    split = lambda x: x.reshape(nc, c_len, x.shape[-1])
    head_slices = [slice(h * GDN_DK, (h + 1) * GDN_DK) for h in range(GDN_HEADS)]
    stack = lambda ref: jnp.concatenate([split(ref[:, sl]) for sl in head_slices], axis=0)
    q = stack(q_ref)
    k = stack(k_ref)
    v = stack(v_ref)
    beta = jnp.concatenate([split(bg[:, h:h + 1]) for h in range(GDN_HEADS)], axis=0)
    g = jnp.concatenate([split(bg[:, GDN_HEADS + h:GDN_HEADS + h + 1]) for h in range(GDN_HEADS)], axis=0)
    dlog = lax.dot_general(incl_f, g * strict_f, (((2,), (1,)), ((0,), (0,))), preferred_element_type=F32,
                           precision=HIGHEST)
    gc = dlog[:, :, 0:1] + g[:, 0:1, :]
    decay = jnp.where(incl, jnp.exp(dlog), 0.0)
    egc = jnp.exp(gc)
    g_last = gc[:, c_len - 1:c_len, :]
    kb = k * beta
    a = jnp.where(strict, _bmm_nt(kb, k) * decay, 0.0)
    p = -a
    tinv = eye + p
    for _ in range(int(math.log2(c_len)) - 1):
        p = _bmm(p, p)
        tinv = tinv + _bmm(tinv, p)
    u = _bmm(tinv, v * beta)
    w = _bmm(tinv, kb * egc)
    qk = jnp.where(incl, _bmm_nt(q, k) * decay, 0.0)
    qd = q * egc
    kd = k * jnp.exp(g_last - gc)
    gl = jnp.broadcast_to(jnp.exp(g_last), (nb, c_len, 1))
    lane = lax.broadcasted_iota(jnp.int32, (tb, _SMALL_W), 1)
    gl_all = jnp.zeros((tb, _SMALL_W), F32)
    for h, sl in enumerate(head_slices):
        of_head = lambda x: x[h * nc:(h + 1) * nc].reshape(tb, x.shape[-1])
        u_ref[:, sl] = of_head(u)
        w_ref[:, sl] = of_head(w)
        qd_ref[:, sl] = of_head(qd)
        kd_ref[:, sl] = of_head(kd)
        qk_ref[:, h * c_len:(h + 1) * c_len] = of_head(qk)
        gl_all = jnp.where(lane == h, of_head(gl), gl_all)
    gl_ref[...] = gl_all


def _gdn_scan_body(u_ref, w_ref, qd_ref, kd_ref, qk_ref, gl_ref, z_ref, s0_ref, nw_ref, o_ref, sfin_ref, s_sc,
                   *, n_chunks, n_seq, c_len):
    i = pl.program_id(1)

    @pl.when(i == 0)
    def _():
        s_sc[...] = s0_ref[...]

    nw = nw_ref[...]

    def chunk(c, carry):
        r0 = pl.multiple_of(c * c_len, c_len)
        rows = pl.ds(r0, c_len)
        gl_row = gl_ref[:, pl.ds(r0, 1), :]
        for h in range(GDN_HEADS):
            sl = slice(h * GDN_DK, (h + 1) * GDN_DK)
            s = s_sc[:, h]
            v_new = u_ref[:, rows, sl] - _bmm(w_ref[:, rows, sl], s)
            o = _bmm(qd_ref[:, rows, sl], s) + _bmm(qk_ref[:, rows, h * c_len:(h + 1) * c_len], v_new)
            kd = kd_ref[:, rows, sl]
            upd = jnp.stack([_dot3(kd[b], v_new[b], (((0,), (0,)), ((), ()))) for b in range(n_seq)])
            s_sc[:, h] = s * gl_row[:, :, h:h + 1] + upd
            z = z_ref[:, rows, sl]
            on = o * lax.rsqrt(jnp.mean(o * o, axis=-1, keepdims=True) + EPS) * nw
            o_ref[:, rows, sl] = on * _silu(z)
        return carry

    lax.fori_loop(0, n_chunks, chunk, 0)

    @pl.when(i == pl.num_programs(1) - 1)
    def _():
        sfin_ref[...] = s_sc[...]


_GDN_SEQ_PER_STEP = 8
_GDN_SHORT_CHUNK = 16


def _gdn(q, k, v, bg, z, s0, norm_w):
    b, t, hd = q.shape
    tb = min(t, 512)
    c_len = min(GDN_CHUNK, t)
    nc = tb // c_len
    row = lambda bi, i: (bi, i, 0)
    wide = lambda n: pl.BlockSpec((None, tb, n), row)
    qkw = GDN_HEADS * c_len
    u, w, qd, kd, qk, gl = pl.pallas_call(
        functools.partial(_gdn_ut_body, n_chunks=nc, c_len=c_len),
        grid=(b, t // tb),
        in_specs=[wide(hd)] * 3 + [wide(_SMALL_W)],
        out_specs=[wide(hd)] * 4 + [wide(qkw), wide(_SMALL_W)],
        out_shape=[jax.ShapeDtypeStruct((b, t, hd), F32)] * 4
        + [jax.ShapeDtypeStruct((b, t, qkw), F32), jax.ShapeDtypeStruct((b, t, _SMALL_W), F32)],
        compiler_params=_cparams(("parallel", "parallel")),
        name="gdn_ut",
    )(q, k, v, bg)
    nb = min(b, _GDN_SEQ_PER_STEP)
    grp = lambda gi, i: (gi, i, 0)
    st = lambda gi, i: (gi, 0, 0, 0)
    seqs = lambda n: pl.BlockSpec((nb, tb, n), grp)
    return pl.pallas_call(
        functools.partial(_gdn_scan_body, n_chunks=nc, n_seq=nb, c_len=c_len),
        grid=(b // nb, t // tb),
        in_specs=[seqs(hd)] * 4 + [seqs(qkw), seqs(_SMALL_W), seqs(hd),
                                   pl.BlockSpec((nb, GDN_HEADS, GDN_DK, GDN_DV), st),
                                   pl.BlockSpec((1, GDN_DV), lambda gi, i: (0, 0))],
        out_specs=[seqs(hd), pl.BlockSpec((nb, GDN_HEADS, GDN_DK, GDN_DV), st)],
        out_shape=[jax.ShapeDtypeStruct((b, t, hd), F32),
                   jax.ShapeDtypeStruct((b, GDN_HEADS, GDN_DK, GDN_DV), F32)],
        scratch_shapes=[pltpu.VMEM((nb, GDN_HEADS, GDN_DK, GDN_DV), F32)],
        compiler_params=_cparams(("parallel", "arbitrary")),
        name="gdn_scan",
    )(u, w, qd, kd, qk, gl, z, s0, norm_w)


_SEG_PER_PAGE = PAGE_ROWS // CMP_STRIDE
_CMP_PAGES_PER_STEP = 32


def _compress_body(pt_ref, *refs, pages_per_seq, npg, feature_major):
    page_refs = refs[:npg]
    pe_ref, w1_ref, w2_ref, o_ref, carry_sc, rows_sc = refs[npg:]
    step = pl.program_id(0)
    first = (step % (pages_per_seq // npg)) == 0
    for p, r in enumerate(page_refs):
        for kv in range(2):
            half = slice(kv * LANES, (kv + 1) * LANES)
            rows_sc[kv, p * PAGE_ROWS:(p + 1) * PAGE_ROWS, :] = r[half, :].T if feature_major else r[:, half]
    n_rows = npg * _SEG_PER_PAGE
    n_m = CMP_LEN // CMP_STRIDE
    gh = NSA_GROUPS * CMP_HID
    acc = [[jnp.zeros((n_rows, gh), F32) for _ in range(2)] for _ in range(n_m)]
    for sp in range(CMP_STRIDE // 2):
        for kv in range(2):
            x = jnp.concatenate([rows_sc[kv, pl.ds(2 * sp + i, n_rows, stride=CMP_STRIDE), :] for i in range(2)],
                                axis=1)
            for m in range(n_m):
                xm = (x + pe_ref[m, sp, kv:kv + 1, :]).astype(BF16)
                acc[m][kv] = acc[m][kv] + _dot(xm, w1_ref[m, sp, kv])
    part = [jnp.concatenate(a, axis=1) for a in acc]
    prev = jnp.where(first, 0.0, carry_sc[...])
    rows = lax.broadcasted_iota(jnp.int32, (n_rows, 1), 0)
    shifted = jnp.where(rows == 0, prev, pltpu.roll(part[0], 1, 0))
    carry_sc[...] = part[0][n_rows - 1:n_rows, :]
    hid = _silu(shifted + part[1]).astype(BF16)
    for kv in range(2):
        o_ref[:, kv * LANES:(kv + 1) * LANES] = _dot(hid[:, kv * gh:(kv + 1) * gh], w2_ref[kv]).astype(BF16)


def _compress(pool, table, n_seq, pe, w1bd, w2bd, feature_major):
    pages_per_seq = table.shape[0] // n_seq
    npg = min(_CMP_PAGES_PER_STEP, pages_per_seq)
    n_steps = table.shape[0] // npg
    page_spec = lambda p: pl.BlockSpec((None,) + pool.shape[1:], lambda i, pt: (pt[i * npg + p], 0, 0))
    fixed = lambda nd: (lambda i, pt: (0,) * nd)
    out = pl.pallas_call(
        functools.partial(_compress_body, pages_per_seq=pages_per_seq, npg=npg, feature_major=feature_major),
        grid_spec=pltpu.PrefetchScalarGridSpec(
            num_scalar_prefetch=1,
            grid=(n_steps,),
            in_specs=[page_spec(p) for p in range(npg)]
            + [pl.BlockSpec(pe.shape, fixed(4)), pl.BlockSpec(w1bd.shape, fixed(5)),
               pl.BlockSpec(w2bd.shape, fixed(3))],
            out_specs=pl.BlockSpec((npg * _SEG_PER_PAGE, KV_ROW), lambda i, pt: (i, 0)),
            scratch_shapes=[pltpu.VMEM((1, 2 * NSA_GROUPS * CMP_HID), F32),
                            pltpu.VMEM((2, npg * PAGE_ROWS, LANES), F32)],
        ),
        out_shape=jax.ShapeDtypeStruct((n_steps * npg * _SEG_PER_PAGE, KV_ROW), BF16),
        compiler_params=_cparams(("arbitrary",)),
        name="nsa_compress",
    )(table, *([pool] * npg), pe, w1bd, w2bd)
    return out.reshape(n_seq, pages_per_seq * _SEG_PER_PAGE, KV_ROW)


def _compress_weights(cmp_pos, cmp_w1, cmp_w2):
    n_m = CMP_LEN // CMP_STRIDE
    pe = cmp_pos.reshape(2, n_m, CMP_STRIDE, 1, NSA_DH)
    pe = jnp.broadcast_to(pe, (2, n_m, CMP_STRIDE, NSA_GROUPS, NSA_DH))
    gd = NSA_GROUPS * NSA_DH
    pe = jnp.transpose(pe.reshape(2, n_m, CMP_STRIDE // 2, 2, gd), (1, 2, 0, 3, 4)).reshape(
        n_m, CMP_STRIDE // 2, 2, 2 * gd)
    w1 = cmp_w1.reshape(2, n_m, CMP_STRIDE // 2, 2, NSA_DH, CMP_HID)
    w1 = jnp.transpose(w1, (1, 2, 0, 3, 4, 5))
    eye = jnp.eye(NSA_GROUPS, dtype=F32)
    w1bd = jnp.einsum('mskidh,gG->mskigdGh', w1, eye).reshape(
        n_m, CMP_STRIDE // 2, 2, 2 * gd, NSA_GROUPS * CMP_HID).astype(BF16)
    w2bd = jnp.einsum('khd,gG->kghGd', cmp_w2, eye).reshape(
        2, NSA_GROUPS * CMP_HID, NSA_GROUPS * NSA_DH).astype(BF16)
    return pe, w1bd, w2bd


def _overlap_matrix(n_rows, n_blk_lanes):
    ci = (np.arange(n_rows)[:, None] - 1) * CMP_STRIDE
    sj = np.arange(n_blk_lanes)[None, :] * SLC_BLK
    ov = (ci < sj + SLC_BLK) & (ci + CMP_LEN > sj) & (ci >= 0)
    return jnp.asarray(ov.astype(np.float32))


_SEL_KEY_BLOCK = 1024


_MASK_BIG = 1e30
_VAL_ROWS = NSA_DH + 16


def _nsa_prompt_body(qt_ref, smt_ref, ck_ref, cvt_ref, ska_ref, sva_ref, wk_ref, wva_ref, ovt_ref, o_ref,
                     m_sc, acc_sc, s_sc):
    iq = pl.program_id(1)
    qb = Q_BLK
    t0 = iq * qb
    n_cmp_rows = ck_ref.shape[0]
    n_blk = ovt_ref.shape[0]
    w_rows = WINDOW + qb
    gd = NSA_GROUPS * NSA_DH
    t_row = t0 + lax.broadcasted_iota(jnp.int32, (1, qb), 1)
    gates = jax.nn.sigmoid(smt_ref[...])
    ci = lax.broadcasted_iota(jnp.int32, (n_cmp_rows, 1), 0)
    cmp_mask = (ci >= 1) & (CMP_STRIDE * ci + (CMP_LEN - CMP_STRIDE - 1) <= t_row)
    blk = lax.broadcasted_iota(jnp.int32, (n_blk, 1), 0)
    forced = (blk == t_row // SLC_BLK) | (blk == 0)
    valid = blk * SLC_BLK <= t_row
    start = pl.multiple_of(jnp.maximum(t0 - WINDOW, 0), qb)
    wpos = start + lax.broadcasted_iota(jnp.int32, (w_rows, 1), 0)
    dlt = t_row - wpos
    win_mask = (dlt >= 0) & (dlt < WINDOW)
    kblk = min(_SEL_KEY_BLOCK, ska_ref.shape[0])
    key_col = lax.broadcasted_iota(jnp.int32, (kblk, 1), 0)
    scale = NSA_DH ** -0.5 * math.log2(math.e)
    heads = lambda x: [x[:, r * qb:(r + 1) * qb] for r in range(NSA_REP)]

    qts, o_cs, imps = [], [], []
    for g in range(NSA_GROUPS):
        vrows = slice(g * NSA_DH, (g + 1) * NSA_DH)
        qt = jnp.concatenate([qt_ref[(g * NSA_REP + r) * NSA_DH:(g * NSA_REP + r + 1) * NSA_DH, :]
                              for r in range(NSA_REP)], axis=1) * scale
        qt = jnp.concatenate([qt, jnp.zeros_like(qt)] if g == 0 else [jnp.zeros_like(qt), qt], axis=0)
        qt = qt.astype(BF16)
        s_c = _dot(ck_ref[...], qt)
        p_c = [_masked_softmax(s, cmp_mask, axis=0, exp=jnp.exp2) for s in heads(s_c)]
        o_cs.append(_dot(cvt_ref[vrows, :], jnp.concatenate(p_c, axis=1).astype(BF16)))
        imp = _dot(ovt_ref[...], p_c[0] + p_c[1] + p_c[2] + p_c[3], HIGHEST)
        imps.append(jnp.where(forced, 1e9, jnp.where(valid, imp, -1.0)))
        qts.append(qt)
    sel_all = _select_top(jnp.concatenate(imps, axis=1), blk, n_blk, axis=0)

    for g in range(NSA_GROUPS):
        vrows = slice(g * NSA_DH, (g + 1) * NSA_DH)
        qt = qts[g]
        o_c = o_cs[g]
        sel = sel_all[:, g * qb:(g + 1) * qb]

        bias = ((sel - 1.0) * _MASK_BIG).astype(BF16)
        rhs = jnp.concatenate([qt, jnp.concatenate([bias] * NSA_REP, axis=1)], axis=0)
        m_sc[...] = jnp.full(m_sc.shape, NEG, F32)
        acc_sc[...] = jnp.zeros(acc_sc.shape, F32)

        def scores(kb):
            return _dot(ska_ref[pl.ds(pl.multiple_of(kb * kblk, kblk), kblk), :], rhs)

        def absorb(kb, s_s, causal):
            r0 = pl.multiple_of(kb * kblk, kblk)
            m_old = m_sc[...]
            m_new, es = [], []
            for r, s in enumerate(heads(s_s)):
                if causal:
                    s = jnp.where(r0 + key_col <= t_row, s, NEG)
                m_r = jnp.maximum(m_old[:, r * qb:(r + 1) * qb], jnp.max(s, axis=0, keepdims=True))
                es.append(jnp.exp2(s - m_r).astype(BF16))
                m_new.append(m_r)
            m_new = jnp.concatenate(m_new, axis=1)
            corr = jnp.exp2(m_old - m_new)
            m_sc[...] = m_new
            acc_sc[...] = corr * acc_sc[...] + _dot(sva_ref[g, :, pl.ds(r0, kblk)], jnp.concatenate(es, axis=1))

        n_steps = (t0 + qb + kblk - 1) // kblk
        s_sc[0] = scores(0)

        def kv_step(kb, carry):
            s_next = scores(kb + 1)
            absorb(kb, s_sc[kb % 2], False)
            s_sc[(kb + 1) % 2] = s_next
            return carry

        lax.fori_loop(0, n_steps - 1, kv_step, 0)
        absorb(n_steps - 1, s_sc[(n_steps - 1) % 2], True)
        o_s = acc_sc[0:NSA_DH, :] * (1.0 / jnp.maximum(acc_sc[NSA_DH:NSA_DH + 1, :], 1e-30))
        s_w = _dot(wk_ref[pl.ds(start, w_rows), :], qt)
        e_w = []
        for s in heads(s_w):
            s = jnp.where(win_mask, s, NEG)
            e_w.append(jnp.exp2(s - jnp.max(s, axis=0, keepdims=True)).astype(BF16))
        o_w = _dot(wva_ref[g, :, pl.ds(start, w_rows)], jnp.concatenate(e_w, axis=1))
        o_w = o_w[0:NSA_DH, :] * (1.0 / o_w[NSA_DH:NSA_DH + 1, :])
        for r in range(NSA_REP):
            h = g * NSA_REP + r
            row0 = 2 * GDN_HEADS + 3 * h
            cs = slice(r * qb, (r + 1) * qb)
            o_ref[h * NSA_DH:(h + 1) * NSA_DH, :] = (
                gates[row0:row0 + 1, :] * o_c[:, cs] + gates[row0 + 1:row0 + 2, :] * o_s[:, cs]
                + gates[row0 + 2:row0 + 3, :] * o_w[:, cs])


def _nsa_prompt(q_b, small, cmp_rows, kv_s, kv_w):
    b, t, _ = q_b.shape
    gd = NSA_GROUPS * NSA_DH
    n_blk = max(t // SLC_BLK, LANES)
    ovt = _overlap_matrix(cmp_rows.shape[1], n_blk).T
    tr = lambda a: jnp.swapaxes(a, 1, 2)
    keys = lambda kv: kv[:, :, :gd].astype(BF16)
    vals_t = lambda kv: tr(kv[:, :, gd:]).astype(BF16)
    blk_onehot = jnp.asarray(np.arange(t)[:, None] // SLC_BLK == np.arange(n_blk)[None, :], BF16)
    keys_aug = jnp.concatenate([keys(kv_s), jnp.broadcast_to(blk_onehot, (b, t, n_blk))], axis=2)
    extra = jnp.zeros((b, NSA_GROUPS, _VAL_ROWS - NSA_DH, t), BF16).at[:, :, 0].set(1.0)
    vals_aug = lambda kv: jnp.concatenate([vals_t(kv).reshape(b, NSA_GROUPS, NSA_DH, t), extra], axis=2)
    whole = lambda bi, i: (bi, 0, 0)
    per_group = pl.BlockSpec((None, NSA_GROUPS, _VAL_ROWS, t), lambda bi, i: (bi, 0, 0, 0))
    tok = lambda bi, i: (bi, 0, i)
    rq = NSA_REP * Q_BLK
    out_t = pl.pallas_call(
        _nsa_prompt_body,
        grid=(b, t // Q_BLK),
        in_specs=[pl.BlockSpec((None, NSA_HEADS * NSA_DH, Q_BLK), tok),
                  pl.BlockSpec((None, _SMALL_W, Q_BLK), tok),
                  pl.BlockSpec((None, cmp_rows.shape[1], gd), whole),
                  pl.BlockSpec((None, gd, cmp_rows.shape[1]), whole),
                  pl.BlockSpec((None, t, gd + n_blk), whole), per_group,
                  pl.BlockSpec((None, t, gd), whole), per_group,
                  pl.BlockSpec(ovt.shape, lambda bi, i: (0, 0))],
        out_specs=pl.BlockSpec((None, NSA_HEADS * NSA_DH, Q_BLK), tok),
        out_shape=jax.ShapeDtypeStruct((b, NSA_HEADS * NSA_DH, t), F32),
        scratch_shapes=[pltpu.VMEM((1, rq), F32), pltpu.VMEM((_VAL_ROWS, rq), F32),
                        pltpu.VMEM((2, min(_SEL_KEY_BLOCK, t), rq), F32)],
        compiler_params=_cparams(("parallel", "arbitrary")),
        name="nsa_prompt",
    )(tr(q_b), tr(small), cmp_rows[:, :, :gd], tr(cmp_rows[:, :, gd:]), keys_aug, vals_aug(kv_s),
      keys(kv_w), vals_aug(kv_w), ovt)
    return tr(out_t)


_DEC_PAGES_PER_STEP = 16


def _nsa_decode_body(pt_ref, qbd_ref, sm_ref, cmp_ref, wint_ref, neww_ref, news_ref, ov_ref, *refs, n_past_blk,
                     npg):
    page_refs = refs[:npg]
    o_ref, sel_sc, m_sc, l_sc, acc_sc, oc_sc, ow_sc = refs[npg:]
    j = pl.program_id(1)
    n_blk = ov_ref.shape[1]
    gd = NSA_GROUPS * NSA_DH
    qf = qbd_ref[...] * (NSA_DH ** -0.5)
    q = qf.astype(BF16)
    head = lax.broadcasted_iota(jnp.int32, (NSA_HEADS, gd), 0)
    lane = lax.broadcasted_iota(jnp.int32, (NSA_HEADS, gd), 1)
    own = (lane // NSA_DH == head // NSA_REP).astype(F32)

    @pl.when(j == 0)
    def _():
        n_rows = cmp_ref.shape[0]
        ci = lax.broadcasted_iota(jnp.int32, (1, n_rows), 1)
        p_c = _masked_softmax(_dot_nt(q, cmp_ref[:, 0:gd]), ci >= 1)
        oc_sc[...] = _dot(p_c.astype(BF16), cmp_ref[:, gd:2 * gd]) * own
        hr = lax.broadcasted_iota(jnp.int32, (NSA_HEADS, NSA_HEADS), 0) // NSA_REP
        hc = lax.broadcasted_iota(jnp.int32, (NSA_HEADS, NSA_HEADS), 1) // NSA_REP
        p_grp = _dot((hr == hc).astype(F32), p_c, HIGHEST)
        imp = _dot(p_grp, ov_ref[...], HIGHEST)
        blk = lax.broadcasted_iota(jnp.int32, (1, n_blk), 1)
        imp = jnp.where((blk == n_past_blk) | (blk == 0), 1e9, jnp.where(blk < n_past_blk, imp, -2e38))
        sel_sc[...] = _select_top(imp, blk, n_blk)
        n_win = wint_ref.shape[1]
        wi = lax.broadcasted_iota(jnp.int32, (1, n_win), 1)
        neww = neww_ref[...]
        s_w = jnp.where(wi >= 1, _dot(q, wint_ref[0:gd, :].astype(BF16)), NEG)
        s_cur = jnp.sum(qf * neww[:, 0:gd], axis=-1, keepdims=True)
        m_w = jnp.maximum(jnp.max(s_w, axis=-1, keepdims=True), s_cur)
        e_w = jnp.where(wi >= 1, jnp.exp(s_w - m_w), 0.0)
        e_cur = jnp.exp(s_cur - m_w)
        den = jnp.maximum(jnp.sum(e_w, axis=-1, keepdims=True) + e_cur, 1e-30)
        pv_w = _dot_nt(e_w.astype(BF16), wint_ref[gd:2 * gd, :].astype(BF16)) + e_cur * neww[:, gd:2 * gd]
        ow_sc[...] = pv_w * (1.0 / den) * own
        m_sc[...] = jnp.full(m_sc.shape, NEG, F32)
        l_sc[...] = jnp.zeros(l_sc.shape, F32)
        acc_sc[...] = jnp.zeros(acc_sc.shape, F32)

    n_keys = npg * PAGE_ROWS
    kt = jnp.concatenate([r[0:gd, :] for r in page_refs], axis=1).astype(BF16)
    vt = jnp.concatenate([r[gd:2 * gd, :] for r in page_refs], axis=1).astype(BF16)
    e_row = lax.broadcasted_iota(jnp.int32, (n_blk, n_keys), 0)
    e_col = lax.broadcasted_iota(jnp.int32, (n_blk, n_keys), 1) // SLC_BLK
    first_blk = (PAGE_ROWS // SLC_BLK) * (j * npg)
    msk = _dot(sel_sc[...].astype(BF16), (e_row == first_blk + e_col).astype(BF16)) > 0.5
    sc = jnp.where(msk, _dot(q, kt), NEG)
    m_old = m_sc[...]
    m_new = jnp.maximum(m_old, jnp.max(sc, axis=-1, keepdims=True))
    corr = jnp.exp(m_old - m_new)
    e = jnp.where(msk, jnp.exp(sc - m_new), 0.0)
    l_sc[...] = corr * l_sc[...] + jnp.sum(e, axis=-1, keepdims=True)
    acc_sc[...] = corr * acc_sc[...] + _dot_nt(e.astype(BF16), vt)
    m_sc[...] = m_new

    @pl.when(j == pl.num_programs(1) - 1)
    def _():
        new = news_ref[...]
        s_new = jnp.sum(qf * new[:, 0:gd], axis=-1, keepdims=True)
        m_old = m_sc[...]
        m_new = jnp.maximum(m_old, s_new)
        corr = jnp.exp(m_old - m_new)
        e = jnp.exp(s_new - m_new)
        l = corr * l_sc[...] + e
        acc = corr * acc_sc[...] + e * new[:, gd:2 * gd]
        o_s = acc / jnp.maximum(l, 1e-30) * own
        gates = jax.nn.sigmoid(sm_ref[...])
        glane = lax.broadcasted_iota(jnp.int32, (NSA_HEADS, _SMALL_W), 1)
        ghead = lax.broadcasted_iota(jnp.int32, (NSA_HEADS, _SMALL_W), 0)
        gate = [jnp.sum(jnp.where(glane == 2 * GDN_HEADS + 3 * ghead + br, gates, 0.0), axis=-1, keepdims=True)
                for br in range(3)]
        o = gate[0] * oc_sc[...] + gate[1] * o_s + gate[2] * ow_sc[...]
        o_ref[...] = o[:, 0:NSA_DH] + o[:, NSA_DH:gd]


def _nsa_decode(qbd, small, cmp_rows, win_t, new_w, new_s, pool, table):
    b = qbd.shape[0]
    pages_per_seq = table.shape[0] // b
    npg = min(_DEC_PAGES_PER_STEP, pages_per_seq)
    n_past_blk = pages_per_seq * (PAGE_ROWS // SLC_BLK)
    n_blk = -(-(n_past_blk + 1) // LANES) * LANES
    ov = _overlap_matrix(cmp_rows.shape[1], n_blk)
    per_seq = lambda bi, j, pt: (bi, 0, 0)
    page_spec = lambda p: pl.BlockSpec(
        (None, KV_ROW, PAGE_ROWS), lambda bi, j, pt: (pt[bi * pages_per_seq + j * npg + p], 0, 0))
    gd = NSA_GROUPS * NSA_DH
    out = pl.pallas_call(
        functools.partial(_nsa_decode_body, n_past_blk=n_past_blk, npg=npg),
        grid_spec=pltpu.PrefetchScalarGridSpec(
            num_scalar_prefetch=1,
            grid=(b, pages_per_seq // npg),
            in_specs=[pl.BlockSpec((None, NSA_HEADS, gd), per_seq),
                      pl.BlockSpec((None, 1, _SMALL_W), per_seq),
                      pl.BlockSpec((None,) + cmp_rows.shape[1:], per_seq),
                      pl.BlockSpec((None,) + win_t.shape[1:], per_seq),
                      pl.BlockSpec((None, 1, KV_ROW), per_seq), pl.BlockSpec((None, 1, KV_ROW), per_seq),
                      pl.BlockSpec(ov.shape, lambda bi, j, pt: (0, 0))]
            + [page_spec(p) for p in range(npg)],
            out_specs=pl.BlockSpec((None, NSA_HEADS, NSA_DH), per_seq),
            scratch_shapes=[pltpu.VMEM((NSA_HEADS, n_blk), F32), pltpu.VMEM((NSA_HEADS, 1), F32),
                            pltpu.VMEM((NSA_HEADS, 1), F32), pltpu.VMEM((NSA_HEADS, gd), F32),
                            pltpu.VMEM((NSA_HEADS, gd), F32), pltpu.VMEM((NSA_HEADS, gd), F32)],
        ),
        out_shape=jax.ShapeDtypeStruct((b, NSA_HEADS, NSA_DH), F32),
        compiler_params=_cparams(("parallel", "arbitrary")),
        name="nsa_decode",
    )(table, qbd, small, cmp_rows, win_t, new_w, new_s, ov, *([pool] * npg))
    return out.reshape(b, NSA_HEADS * NSA_DH)


def _outproj_body(oa_ref, ob_ref, gm_ref, x_ref, gt_ref, wa_ref, wb_ref, wo_ref, lg_ref, lb_ref, o_ref):
    ya = _dot(oa_ref[...].astype(BF16), wa_ref[...])
    yb = _dot(ob_ref[...].astype(BF16), wb_ref[...])
    gm = jax.nn.sigmoid(gm_ref[...])
    u = (gm[:, :D_MODEL] * ya + gm[:, D_MODEL:] * yb).astype(BF16)
    y = _dot(u, wo_ref[...])
    xr = ALPHA * x_ref[...] + gt_ref[...] * y
    o_ref[...] = _ln_rows(xr) * lg_ref[...] + lb_ref[...]


def _outproj(o_a, o_b, g_m, x2d, gt, w_up_a, w_up_b, w_out, ln_g, ln_b, tm, rows_per_mod):
    m = x2d.shape[0]
    row = lambda n: pl.BlockSpec((tm, n), lambda i: (i, 0))
    fixed = lambda shape: pl.BlockSpec(shape, lambda i: (0, 0))
    return pl.pallas_call(
        _outproj_body,
        grid=(m // tm,),
        in_specs=[row(o_a.shape[1]), row(o_b.shape[1]), row(2 * D_MODEL), row(D_MODEL),
                  _mod_spec(gt, tm, rows_per_mod),
                  fixed(w_up_a.shape), fixed(w_up_b.shape), fixed(w_out.shape),
                  fixed((1, D_MODEL)), fixed((1, D_MODEL))],
        out_specs=row(D_MODEL),
        out_shape=jax.ShapeDtypeStruct((m, D_MODEL), F32),
        compiler_params=_cparams(("parallel",)),
        name="mixer_outproj",
    )(o_a, o_b, g_m, x2d, gt, w_up_a, w_up_b, w_out, ln_g, ln_b)


_FF_TILE = 256


def _ffn_body(x_ref, sc_ref, sh_ref, gt_ref, wg_ref, wu_ref, wd_ref, lg_ref, lb_ref, o_ref, h_sc, acc_sc):
    f = pl.program_id(1)

    @pl.when(f == 0)
    def _():
        h = _ln_rows(x_ref[...]) * (1.0 + sc_ref[...]) + sh_ref[...]
        h_sc[...] = h.astype(BF16)
        acc_sc[...] = jnp.zeros(acc_sc.shape, F32)

    hb = h_sc[...]
    hid = _silu(_dot(hb, wg_ref[...])) * _dot(hb, wu_ref[...])
    acc_sc[...] += _dot(hid.astype(BF16), wd_ref[...])

    @pl.when(f == pl.num_programs(1) - 1)
    def _():
        xr = ALPHA * x_ref[...] + gt_ref[...] * acc_sc[...]
        o_ref[...] = _ln_rows(xr) * lg_ref[...] + lb_ref[...]


def _ffn(x2d, sc, sh, gt, w_gu, w_down, ln_g, ln_b, tm, rows_per_mod):
    m = x2d.shape[0]
    tf = _FF_TILE
    n_f = D_FF // tf
    row = pl.BlockSpec((tm, D_MODEL), lambda i, f: (i, 0))
    fixed = lambda shape: pl.BlockSpec(shape, lambda i, f: (0, 0))
    mod = lambda a: _mod_spec(a, tm, rows_per_mod)
    return pl.pallas_call(
        _ffn_body,
        grid=(m // tm, n_f),
        in_specs=[row, mod(sc), mod(sh), mod(gt),
                  pl.BlockSpec((D_MODEL, tf), lambda i, f: (0, f)),
                  pl.BlockSpec((D_MODEL, tf), lambda i, f: (0, f + n_f)),
                  pl.BlockSpec((tf, D_MODEL), lambda i, f: (f, 0)),
                  fixed((1, D_MODEL)), fixed((1, D_MODEL))],
        out_specs=row,
        out_shape=jax.ShapeDtypeStruct((m, D_MODEL), F32),
        scratch_shapes=[pltpu.VMEM((tm, D_MODEL), BF16), pltpu.VMEM((tm, D_MODEL), F32)],
        compiler_params=_cparams(("parallel", "arbitrary")),
        name="ffn_dense",
    )(x2d, sc, sh, gt, w_gu, w_gu, w_down, ln_g, ln_b)


_MOE_CHUNK = 288
_MOE_FF_TILE = 1408


def _moe_body(x_ref, sc_ref, sh_ref, gt_ref, wr_ref, wg_ref, wu_ref, wd_ref, lg_ref, lb_ref, o_ref,
              h_sc, w_sc, sel_sc, rank_sc, selt_sc, rankt_sc, xg_sc, yg_sc, y_sc, *, chunk):
    e = pl.program_id(1)
    f = pl.program_id(2)
    tb = x_ref.shape[0]
    lane = lax.broadcasted_iota(jnp.int32, (1, LANES), 1)

    @pl.when((e == 0) & (f == 0))
    def _():
        h = _ln_rows(x_ref[...]) * (1.0 + sc_ref[...]) + sh_ref[...]
        h_sc[...] = h.astype(BF16)
        logits = jnp.where(lane < N_EXP, _dot(h, wr_ref[...], HIGHEST), NEG)
        ex = jnp.exp(logits - jnp.max(logits, axis=-1, keepdims=True))
        probs = ex / jnp.sum(ex, axis=-1, keepdims=True)
        lane_f = lane.astype(F32)
        p1 = jnp.max(probs, axis=-1, keepdims=True)
        i1 = jnp.min(jnp.where(probs == p1, lane_f, float(LANES)), axis=-1, keepdims=True)
        rest = jnp.where(lane_f == i1, -1.0, probs)
        p2 = jnp.max(rest, axis=-1, keepdims=True)
        i2 = jnp.min(jnp.where(rest == p2, lane_f, float(LANES)), axis=-1, keepdims=True)
        w_sc[...] = (jnp.where(lane_f == i1, p1, 0.0) + jnp.where(lane_f == i2, p2, 0.0)) / (p1 + p2)
        sel = ((lane_f == i1) | (lane_f == i2)).astype(F32)
        earlier = (lax.broadcasted_iota(jnp.int32, (tb, tb), 1)
                   < lax.broadcasted_iota(jnp.int32, (tb, tb), 0)).astype(BF16)
        rank = _dot(earlier, sel.astype(BF16))
        sel_sc[...] = sel
        rank_sc[...] = rank
        selt_sc[...] = sel.T
        rankt_sc[...] = rank.T
        y_sc[...] = jnp.zeros(y_sc.shape, F32)

    sel_row = selt_sc[pl.ds(e, 1), :]
    rank_row = rankt_sc[pl.ds(e, 1), :]
    count = jnp.sum(sel_row).astype(jnp.int32)
    n_chunks = (count + chunk - 1) // chunk

    @pl.when(f == 0)
    def _():
        def gather(c, carry):
            slot = (c * chunk + lax.broadcasted_iota(jnp.int32, (chunk, 1), 0)).astype(F32)
            pick = ((rank_row == slot) & (sel_row > 0.5)).astype(BF16)
            xg_sc[pl.ds(pl.multiple_of(c * chunk, chunk), chunk), :] = _dot(pick, h_sc[...]).astype(BF16)
            return carry

        lax.fori_loop(0, n_chunks, gather, 0)

    def expert(c, carry):
        rows = pl.ds(pl.multiple_of(c * chunk, chunk), chunk)
        xg = xg_sc[rows, :]
        hid = _silu(_dot(xg, wg_ref[...])) * _dot(xg, wu_ref[...])
        part = _dot(hid.astype(BF16), wd_ref[...])

        @pl.when(f == 0)
        def _():
            yg_sc[rows, :] = part

        @pl.when(f > 0)
        def _():
            yg_sc[rows, :] += part

        return carry

    lax.fori_loop(0, n_chunks, expert, 0)

    @pl.when(f == pl.num_programs(2) - 1)
    def _():
        pick_e = lane == e
        sel_col = jnp.sum(jnp.where(pick_e, sel_sc[...], 0.0), axis=-1, keepdims=True)
        rank_col = jnp.sum(jnp.where(pick_e, rank_sc[...], 0.0), axis=-1, keepdims=True)
        w_col = jnp.sum(jnp.where(pick_e, w_sc[...], 0.0), axis=-1, keepdims=True)

        def scatter(c, carry):
            slot = (c * chunk + lax.broadcasted_iota(jnp.int32, (1, chunk), 1)).astype(F32)
            place = ((rank_col == slot) & (sel_col > 0.5)).astype(BF16)
            rows = pl.ds(pl.multiple_of(c * chunk, chunk), chunk)
            y_sc[...] += w_col * _dot(place, yg_sc[rows, :].astype(BF16))
            return carry

        lax.fori_loop(0, n_chunks, scatter, 0)

    @pl.when((e == pl.num_programs(1) - 1) & (f == pl.num_programs(2) - 1))
    def _():
        xr = ALPHA * x_ref[...] + gt_ref[...] * y_sc[...]
        o_ref[...] = _ln_rows(xr) * lg_ref[...] + lb_ref[...]


def _moe(x2d, sc, sh, gt, w_router, w_gu, w_down, ln_g, ln_b, tm, rows_per_mod):
    m = x2d.shape[0]
    n_e = w_gu.shape[0]
    tf = _MOE_FF_TILE
    n_f = D_FF // tf
    chunk = min(_MOE_CHUNK, tm)
    slot_rows = -(-tm // chunk) * chunk
    row = pl.BlockSpec((tm, D_MODEL), lambda i, e, f: (i, 0), pipeline_mode=pl.Buffered(1))
    fixed = lambda shape: pl.BlockSpec(shape, lambda i, e, f: (0, 0))
    mod = lambda a: _mod_spec(a, tm, rows_per_mod)
    per_tok = lambda: pltpu.VMEM((tm, LANES), F32)
    per_exp = lambda: pltpu.VMEM((LANES, tm), F32)
    return pl.pallas_call(
        functools.partial(_moe_body, chunk=chunk),
        grid=(m // tm, n_e, n_f),
        in_specs=[row, mod(sc), mod(sh), mod(gt), fixed(w_router.shape),
                  pl.BlockSpec((None, D_MODEL, tf), lambda i, e, f: (e, 0, f)),
                  pl.BlockSpec((None, D_MODEL, tf), lambda i, e, f: (e, 0, f + n_f)),
                  pl.BlockSpec((None, tf, D_MODEL), lambda i, e, f: (e, f, 0)),
                  fixed((1, D_MODEL)), fixed((1, D_MODEL))],
        out_specs=row,
        out_shape=jax.ShapeDtypeStruct((m, D_MODEL), F32),
        scratch_shapes=[pltpu.VMEM((tm, D_MODEL), BF16), per_tok(), per_tok(), per_tok(), per_exp(), per_exp(),
                        pltpu.VMEM((slot_rows, D_MODEL), BF16), pltpu.VMEM((slot_rows, D_MODEL), F32),
                        pltpu.VMEM((tm, D_MODEL), F32)],
        compiler_params=_cparams(("parallel", "arbitrary", "arbitrary")),
        name="moe_routed",
    )(x2d, sc, sh, gt, w_router, w_gu, w_gu, w_down, ln_g, ln_b)


def _permute_w_in(w):
    sizes = [C_QKV, GDN_HEADS * GDN_DV, GDN_HEADS, GDN_HEADS, NSA_HEADS * NSA_DH, KV_ROW, KV_ROW, KV_ROW,
             3 * NSA_HEADS, 2 * D_MODEL]
    offs = np.cumsum([0] + sizes)
    qkv, z, b_a, a_a, q_b, kv_c, kv_s, kv_w, g_b, g_m = (w[:, offs[i]:offs[i + 1]] for i in range(len(sizes)))
    pad = jnp.zeros((w.shape[0], _SMALL_W - 2 * GDN_HEADS - 3 * NSA_HEADS), w.dtype)
    return jnp.concatenate([qkv, z, g_m, q_b, kv_c, kv_s, kv_w, b_a, a_a, g_b, pad], axis=1).astype(BF16)


def _layer_weights(l, w_in, conv_w, a_log, dt_bias, norm_w_a, cmp_pos, cmp_w1, cmp_w2, w_up_a, w_up_b, w_out,
                   ln_g, ln_b):
    lane_pad = lambda v: jnp.zeros((1, _SMALL_W), F32).at[0, GDN_HEADS:2 * GDN_HEADS].set(v)
    pe, w1bd, w2bd = _compress_weights(cmp_pos[l], cmp_w1[l], cmp_w2[l])
    return dict(
        w_in=_permute_w_in(w_in[l]), conv_w=conv_w[l], alog=lane_pad(a_log[l]), dt=lane_pad(dt_bias[l]),
        norm_w=norm_w_a[l].reshape(1, GDN_DV), pe=pe, w1bd=w1bd, w2bd=w2bd,
        w_up_a=w_up_a[l].astype(BF16), w_up_b=w_up_b[l].astype(BF16), w_out=w_out[l].astype(BF16),
        ln_g0=ln_g[l, 0].reshape(1, D_MODEL), ln_b0=ln_b[l, 0].reshape(1, D_MODEL),
        ln_g1=ln_g[l, 1].reshape(1, D_MODEL), ln_b1=ln_b[l, 1].reshape(1, D_MODEL))


def _mixer_common(x2d, mods, wts, b, t, t_pad, conv_buf, s0, tm, rows_per_mod):
    sh_m, sc_m = mods[0], mods[1]
    qkv, z, g_m, q_b, kv_c, kv_s, kv_w, small = _inproj(x2d, sc_m, sh_m, wts['w_in'], tm, rows_per_mod)
    seq = lambda a: a.reshape(b, t, a.shape[-1])
    padt = lambda a: jnp.pad(seq(a), ((0, 0), (0, t_pad - t), (0, 0)))
    buf8 = jnp.pad(conv_buf, ((0, 0), (SUBLANES - (CONV_TAPS - 1), 0), (0, 0)))
    qa, ka, va, bg = _gdn_prep(padt(qkv), buf8, wts['conv_w'], padt(small), wts['alog'], wts['dt'], t)
    o_a, s_new = _gdn(qa, ka, va, bg, padt(z), s0, wts['norm_w'])
    o_a = o_a[:, :t].reshape(b * t, GDN_HEADS * GDN_DV)
    conv_new = jnp.concatenate([conv_buf, seq(qkv)], axis=1)[:, -(CONV_TAPS - 1):]
    return (g_m, q_b, kv_c, kv_s, kv_w, small), o_a, s_new, conv_new


def _rows5(a, b, t):
    return a.reshape(b, t, 2, NSA_GROUPS, NSA_DH)


def _feature_major(rows):
    nd = rows.ndim
    return jnp.transpose(rows, tuple(range(nd - 4)) + (nd - 3, nd - 2, nd - 1, nd - 4))


def _prompt_layer(x, mods, wts, ffn_args, routed):
    b, t, _ = x.shape
    x2d = x.reshape(b * t, D_MODEL)
    tm = 256
    zeros_buf = jnp.zeros((b, CONV_TAPS - 1, C_QKV), F32)
    zeros_s = jnp.zeros((b, GDN_HEADS, GDN_DK, GDN_DV), F32)
    (g_m, q_b, kv_c, kv_s, kv_w, small), o_a, s_new, conv_new = _mixer_common(
        x2d, mods, wts, b, t, t, zeros_buf, zeros_s, tm, t)
    pages = (b * t) // PAGE_ROWS
    cmp_rows = _compress(kv_c.reshape(pages, PAGE_ROWS, KV_ROW), jnp.arange(pages, dtype=jnp.int32), b,
                         wts['pe'], wts['w1bd'], wts['w2bd'], False)
    seq = lambda a: a.reshape(b, t, a.shape[-1])
    o_b = _nsa_prompt(seq(q_b), seq(small), cmp_rows, seq(kv_s), seq(kv_w))
    x1 = _outproj(o_a, o_b.reshape(b * t, -1), g_m, x2d, mods[2], wts['w_up_a'], wts['w_up_b'], wts['w_out'],
                  wts['ln_g0'], wts['ln_b0'], 512, t)
    x2 = (_moe if routed else _ffn)(x1, mods[4], mods[3], mods[5], *ffn_args, wts['ln_g1'], wts['ln_b1'],
                                    min(1024, t), t)
    win_new = _rows5(kv_w, b, t)[:, t - min(WINDOW, t):]
    return x2.reshape(b, t, D_MODEL), (s_new, conv_new, _rows5(kv_c, b, t), _rows5(kv_s, b, t), win_new)


def _sample_layer(x, mods, wts, ffn_args, routed, conv_buf, s0, pool_c, pool_s, win_buf, table):
    b = x.shape[0]
    x2d = x.reshape(b, D_MODEL)
    (g_m, q_b, kv_c, kv_s, kv_w, small), o_a, s_new, conv_new = _mixer_common(
        x2d, mods, wts, b, 1, _GDN_SHORT_CHUNK, conv_buf, s0, b, b)
    cmp_rows = _compress(pool_c, table, b, wts['pe'], wts['w1bd'], wts['w2bd'], True)
    win_new = jnp.concatenate([win_buf.reshape(b, -1, KV_ROW)[:, 1:], kv_w[:, None, :]], axis=1)
    win_t = _feature_major(win_buf).reshape(b, KV_ROW, -1)
    q4 = q_b.reshape(b, NSA_GROUPS, NSA_REP, 1, NSA_DH)
    eye = jnp.eye(NSA_GROUPS, dtype=F32).reshape(NSA_GROUPS, 1, NSA_GROUPS, 1)
    qbd = (q4 * eye).reshape(b, NSA_HEADS, NSA_GROUPS * NSA_DH)
    o_b = _nsa_decode(qbd, small.reshape(b, 1, _SMALL_W), cmp_rows, win_t, kv_w.reshape(b, 1, KV_ROW),
                      kv_s.reshape(b, 1, KV_ROW), pool_s, table)
    x1 = _outproj(o_a, o_b, g_m, x2d, mods[2], wts['w_up_a'], wts['w_up_b'], wts['w_out'],
                  wts['ln_g0'], wts['ln_b0'], b, b)
    x2 = (_moe if routed else _ffn)(x1, mods[4], mods[3], mods[5], *ffn_args, wts['ln_g1'], wts['ln_b1'], b, b)
    return x2.reshape(b, 1, D_MODEL), (s_new, conv_new, _rows5(kv_c, b, 1), _rows5(kv_s, b, 1),
                                       _rows5(win_new, b, win_new.shape[1]))


def kernel(x_prompt, x_sample, state_delta, state_conv, cache_cmp_kv, cache_slc_kv, state_win_kv, page_table,
           c_prompt, c_sample, w_ada, b_ada, w_in, conv_w, a_log, dt_bias, norm_w_a, cmp_pos, cmp_w1, cmp_w2,
           w_up_a, w_up_b, w_out, ln_g, ln_b, ffn_w_gu, ffn_w_down, moe_router, moe_w_gu, moe_w_down):
    bp = x_prompt.shape[0]
    db = x_sample.shape[0]
    n_layers = w_in.shape[0]
    n_pool = cache_cmp_kv.shape[1]
    mod_all = _adaln(jnp.concatenate([c_prompt, c_sample], axis=0), w_ada, b_ada)
    table = page_table.reshape(-1)
    pool_c = _feature_major(cache_cmp_kv).reshape(n_layers * n_pool, KV_ROW, PAGE_ROWS)
    pool_s = _feature_major(cache_slc_kv).reshape(n_layers * n_pool, KV_ROW, PAGE_ROWS)
    xp, xs = x_prompt, x_sample
    st_p, st_s = [], []
    for l in range(n_layers):
        wts = _layer_weights(l, w_in, conv_w, a_log, dt_bias, norm_w_a, cmp_pos, cmp_w1, cmp_w2, w_up_a, w_up_b,
                             w_out, ln_g, ln_b)
        routed = l % 2 == 1
        if routed:
            router = jnp.pad(moe_router[l // 2], ((0, 0), (0, LANES - N_EXP)))
            ffn_args = (router, moe_w_gu[l // 2].astype(BF16), moe_w_down[l // 2].astype(BF16))
        else:
            ffn_args = (ffn_w_gu[l // 2].astype(BF16), ffn_w_down[l // 2].astype(BF16))
        mod6 = mod_all[l].reshape(bp + db, 6, D_MODEL)
        mods_p = [mod6[:bp, i].reshape(bp, 1, D_MODEL) for i in range(6)]
        mods_s = [mod6[bp:, i].reshape(1, db, D_MODEL) for i in range(6)]
        xp, sp = _prompt_layer(xp, mods_p, wts, ffn_args, routed)
        xs, ss = _sample_layer(xs, mods_s, wts, ffn_args, routed, state_conv[l], state_delta[l],
                               pool_c, pool_s, state_win_kv[l], table + l * n_pool)
        st_p.append(sp)
        st_s.append(ss)
    stack = lambda sts, i: jnp.stack([s[i] for s in sts])
    return (xp, xs) + tuple(stack(st_p, i) for i in range(5)) + tuple(stack(st_s, i) for i in range(5))
```
